```python
import jax, jax.numpy as jnp
from jax import lax
import numpy as np

D_MODEL = 1024
BATCH = 4
SEQ = 8192
DEPTH = 2
DEC_BATCH = 128
DEC_SEQ = 4
PAST_LEN = 16384
PAGE_SIZE = 128

HEAD_DIM = 64
MIX_WIDTH = D_MODEL
ATTN_WIDTH = MIX_WIDTH // 2
RET_WIDTH = MIX_WIDTH - ATTN_WIDTH
N_ATTN_HEADS = ATTN_WIDTH // HEAD_DIM
N_KV_HEADS = 2
GQA_GROUP = N_ATTN_HEADS // N_KV_HEADS
WINDOW = 128
ATTN_BLOCK = WINDOW
ROPE_THETA = 500000.0
ROPE_DIMS = HEAD_DIM // 4
N_RET_HEADS = RET_WIDTH // HEAD_DIM
RET_KEY_DIM = HEAD_DIM
RET_VAL_DIM = RET_WIDTH // N_RET_HEADS
RET_CHUNK = 128
RET_THETA = 10000.0
N_EXPERTS = 32
TOP_K = 4
D_FF = D_MODEL
SWIGLU_LIMIT = 7.0
SWIGLU_ALPHA = 1.702
NORM_EPS = 1e-5
Q_COLS = N_ATTN_HEADS * HEAD_DIM
KV_COLS = N_KV_HEADS * HEAD_DIM
RET_QK_COLS = N_RET_HEADS * RET_KEY_DIM
RET_V_COLS = N_RET_HEADS * RET_VAL_DIM
IN_COLS = Q_COLS + 2 * KV_COLS + 2 * RET_QK_COLS + 2 * RET_V_COLS

kernel_name = "hymba_swa_sink_retention_moe_adaln_step"


def rms_norm(x, w):
    xf = x.astype(jnp.float32)
    y = xf * lax.rsqrt(jnp.mean(xf * xf, axis=-1, keepdims=True) + NORM_EPS)
    return (y * w.astype(jnp.float32)).astype(x.dtype)


def modulate(h, shift, scale):
    return h * (1 + scale[:, None, :]) + shift[:, None, :]


def rotate(x, pos, n_dims, theta):
    half = n_dims // 2
    freqs = jnp.power(jnp.float32(theta), -jnp.arange(half, dtype=jnp.float32) / half)
    ang = pos.astype(jnp.float32)[:, None] * freqs[None, :]
    cos = jnp.cos(ang)[None, :, None, :]
    sin = jnp.sin(ang)[None, :, None, :]
    xf = x.astype(jnp.float32)
    x1 = xf[..., :half]
    x2 = xf[..., half:n_dims]
    out = jnp.concatenate([x1 * cos - x2 * sin, x2 * cos + x1 * sin, xf[..., n_dims:]], axis=-1)
    return out.astype(x.dtype)


def split_projection(h, w_in):
    B, T, _ = h.shape
    proj = jnp.einsum('btd,dc->btc', h, w_in)
    cuts = [int(c) for c in np.cumsum([Q_COLS, KV_COLS, KV_COLS, RET_QK_COLS, RET_QK_COLS, RET_V_COLS])]
    q_a, k_a, v_a, q_r, k_r, v_r, g_r = jnp.split(proj, cuts, axis=-1)
    return (q_a.reshape(B, T, N_ATTN_HEADS, HEAD_DIM),
            k_a.reshape(B, T, N_KV_HEADS, HEAD_DIM),
            v_a.reshape(B, T, N_KV_HEADS, HEAD_DIM),
            q_r.reshape(B, T, N_RET_HEADS, RET_KEY_DIM),
            k_r.reshape(B, T, N_RET_HEADS, RET_KEY_DIM),
            v_r.reshape(B, T, N_RET_HEADS, RET_VAL_DIM),
            g_r)


def sink_attention(q, k, v, mask, sinks):
    s = jnp.einsum('...qkgd,...skd->...kgqs', q.astype(jnp.float32), k.astype(jnp.float32)) * (HEAD_DIM ** -0.5)
    s = jnp.where(mask, s, jnp.float32(-1e30))
    sink = jnp.broadcast_to(sinks.astype(jnp.float32)[:, :, None, None], s.shape[:-1] + (1,))
    p = jax.nn.softmax(jnp.concatenate([s, sink], axis=-1), axis=-1)[..., :-1]
    return jnp.einsum('...kgqs,...skd->...qkgd', p, v.astype(jnp.float32))


def swa_prompt(q, k, v, sinks):
    B, T = q.shape[0], q.shape[1]
    nb = T // ATTN_BLOCK
    qb = q.reshape(B, nb, ATTN_BLOCK, N_KV_HEADS, GQA_GROUP, HEAD_DIM)
    kb = k.reshape(B, nb, ATTN_BLOCK, N_KV_HEADS, HEAD_DIM)
    vb = v.reshape(B, nb, ATTN_BLOCK, N_KV_HEADS, HEAD_DIM)
    pad = ((0, 0), (1, 0), (0, 0), (0, 0), (0, 0))
    k_band = jnp.concatenate([jnp.pad(kb, pad)[:, :-1], kb], axis=2)
    v_band = jnp.concatenate([jnp.pad(vb, pad)[:, :-1], vb], axis=2)
    i = jnp.arange(ATTN_BLOCK)
    j = jnp.arange(2 * ATTN_BLOCK)
    blk = jnp.arange(nb)
    diff = ATTN_BLOCK + i[:, None] - j[None, :]
    key_pos = (blk[:, None] - 1) * ATTN_BLOCK + j[None, :]
    mask = ((diff >= 0) & (diff < WINDOW))[None] & (key_pos >= 0)[:, None, :]
    mask = mask.reshape(nb, 1, 1, ATTN_BLOCK, 2 * ATTN_BLOCK)
    o = sink_attention(qb, k_band, v_band, mask, sinks).reshape(B, T, ATTN_WIDTH)
    w = min(WINDOW, T)
    return o.astype(q.dtype), (k[:, T - w:], v[:, T - w:])


def swa_sample(q, k_new, v_new, k_buf, v_buf, sinks, past_len):
    B, tn = q.shape[0], q.shape[1]
    w = k_buf.shape[1]
    k_all = jnp.concatenate([k_buf, k_new.astype(k_buf.dtype)], axis=1)
    v_all = jnp.concatenate([v_buf, v_new.astype(v_buf.dtype)], axis=1)
    q_pos = past_len + jnp.arange(tn)
    k_pos = past_len - w + jnp.arange(w + tn)
    diff = q_pos[:, None] - k_pos[None, :]
    mask = (diff >= 0) & (diff < WINDOW)
    qg = q.reshape(B, tn, N_KV_HEADS, GQA_GROUP, HEAD_DIM)
    o = sink_attention(qg, k_all, v_all, mask, sinks).reshape(B, tn, ATTN_WIDTH)
    return o.astype(q.dtype), (k_all[:, tn:], v_all[:, tn:])


def retention(q, k, v, r0):
    B, T, H, dk = q.shape
    dv = v.shape[-1]
    c = min(RET_CHUNK, T)
    n = T // c
    log_gamma = jnp.log(1.0 - jnp.power(2.0, -5.0 - jnp.arange(H, dtype=jnp.float32)))
    qf = q.astype(jnp.float32).reshape(B, n, c, H, dk)
    kf = (k.astype(jnp.float32) * (dk ** -0.5)).reshape(B, n, c, H, dk)
    vf = v.astype(jnp.float32).reshape(B, n, c, H, dv)
    idx = jnp.arange(c, dtype=jnp.float32)
    diff = idx[:, None] - idx[None, :]
    decay_mask = jnp.where(diff >= 0, jnp.exp(log_gamma[:, None, None] * jnp.maximum(diff, 0.0)), 0.0)
    scores = jnp.einsum('bnihd,bnjhd->bnhij', qf, kf) * decay_mask
    o_inner = jnp.einsum('bnhij,bnjhe->bnihe', scores, vf)
    k_dec = kf * jnp.exp(log_gamma[None, :] * (c - 1 - idx)[:, None])[None, None, :, :, None]
    kv_chunk = jnp.einsum('bnjhd,bnjhe->nbhde', k_dec, vf)
    chunk_decay = jnp.exp(log_gamma * c)[None, :, None, None]

    def step(r, kv_n):
        return chunk_decay * r + kv_n, r

    r_final, r_prev = lax.scan(step, r0.astype(jnp.float32), kv_chunk)
    q_dec = qf * jnp.exp(log_gamma[None, :] * (idx + 1.0)[:, None])[None, None, :, :, None]
    o_cross = jnp.einsum('bnihd,nbhde->bnihe', q_dec, r_prev)
    return (o_inner + o_cross).reshape(B, T, H, dv), r_final


def retention_output(o, g, gn_w, dtype):
    B, T = o.shape[0], o.shape[1]
    mu = jnp.mean(o, axis=-1, keepdims=True)
    var = jnp.mean(jnp.square(o - mu), axis=-1, keepdims=True)
    y = ((o - mu) * lax.rsqrt(var + NORM_EPS)).reshape(B, T, RET_WIDTH) * gn_w.astype(jnp.float32)
    return (y * jax.nn.silu(g.astype(jnp.float32))).astype(dtype)


def token_mixer(h, pos, r0, attn_fn, w_in, ret_gn_w, w_out):
    q_a, k_a, v_a, q_r, k_r, v_r, g_r = split_projection(h, w_in)
    q_a = rotate(q_a, pos, ROPE_DIMS, ROPE_THETA)
    k_a = rotate(k_a, pos, ROPE_DIMS, ROPE_THETA)
    o_a, bufs = attn_fn(q_a, k_a, v_a)
    q_r = rotate(q_r, pos, RET_KEY_DIM, RET_THETA)
    k_r = rotate(k_r, pos, RET_KEY_DIM, RET_THETA)
    o_r, r_new = retention(q_r, k_r, v_r, r0)
    y_r = retention_output(o_r, g_r, ret_gn_w, h.dtype)
    out = jnp.einsum('btc,cd->btd', jnp.concatenate([o_a, y_r], axis=-1), w_out)
    return out, bufs, r_new


def moe_ffn(h, w_router, b_router, w_gate_up, b_gate_up, w_down, b_down):
    B, T, D = h.shape
    x = h.reshape(B * T, D)
    logits = (x @ w_router + b_router).astype(jnp.float32)
    top_vals, top_idx = lax.top_k(logits, TOP_K)
    top_w = jax.nn.softmax(top_vals, axis=-1)
    gates = jnp.einsum('nk,nke->ne', top_w, jax.nn.one_hot(top_idx, N_EXPERTS, dtype=jnp.float32)).astype(x.dtype)
    y = jnp.zeros_like(x)
    for e in range(N_EXPERTS):
        gu = x @ w_gate_up[e] + b_gate_up[e]
        glu = jnp.minimum(gu[:, 0::2], SWIGLU_LIMIT)
        lin = jnp.clip(gu[:, 1::2], -SWIGLU_LIMIT, SWIGLU_LIMIT)
        act = glu * jax.nn.sigmoid(SWIGLU_ALPHA * glu) * (lin + 1)
        y = y + gates[:, e:e + 1] * (act @ w_down[e] + b_down[e])
    return y.reshape(B, T, D)


def trunk_layer(x, c, pos, r0, attn_fn, norm_mix_w, w_ada, b_ada, w_in, ret_gn_w, w_out,
                norm_ffn_w, w_router, b_router, w_gate_up, b_gate_up, w_down, b_down):
    mod = jnp.einsum('bd,de->be', jax.nn.silu(c), w_ada) + b_ada
    sh1, sc1, g1, sh2, sc2, g2 = jnp.split(mod, 6, axis=-1)
    h = modulate(rms_norm(x, norm_mix_w), sh1, sc1)
    mix, bufs, r_new = token_mixer(h, pos, r0, attn_fn, w_in, ret_gn_w, w_out)
    x = x + g1[:, None, :] * mix
    h = modulate(rms_norm(x, norm_ffn_w), sh2, sc2)
    x = x + g2[:, None, :] * moe_ffn(h, w_router, b_router, w_gate_up, b_gate_up, w_down, b_down)
    return x, bufs, r_new


def setup_inputs(seed: int = 0) -> dict:
    key = jax.random.key(seed)
    ks = jax.random.split(key, 24)
    f32 = jnp.float32
    win_buf = min(WINDOW, PAST_LEN)
    nrm = lambda k, shape, s: jax.random.normal(k, shape, f32) * s
    return {
        'x_prompt': nrm(ks[0], (BATCH, SEQ, D_MODEL), 1.0),
        'x_sample': nrm(ks[1], (DEC_BATCH, DEC_SEQ, D_MODEL), 1.0),
        'state_swa_k': nrm(ks[2], (DEPTH, DEC_BATCH, win_buf, N_KV_HEADS, HEAD_DIM), 1.0),
        'state_swa_v': nrm(ks[3], (DEPTH, DEC_BATCH, win_buf, N_KV_HEADS, HEAD_DIM), 1.0),
        'state_ret': nrm(ks[4], (DEPTH, DEC_BATCH, N_RET_HEADS, RET_KEY_DIM, RET_VAL_DIM), 0.3),
        'c_prompt': nrm(ks[5], (BATCH, D_MODEL), 1.0),
        'c_sample': nrm(ks[6], (DEC_BATCH, D_MODEL), 1.0),
        'norm_mix_w': 1.0 + nrm(ks[7], (DEPTH, D_MODEL), 0.01),
        'w_ada': nrm(ks[8], (DEPTH, D_MODEL, 6 * D_MODEL), 0.5 * D_MODEL ** -0.5),
        'b_ada': nrm(ks[9], (DEPTH, 6 * D_MODEL), 0.01),
        'w_in': nrm(ks[10], (DEPTH, D_MODEL, IN_COLS), D_MODEL ** -0.5),
        'attn_sinks': nrm(ks[11], (DEPTH, N_ATTN_HEADS), 1.0),
        'ret_gn_w': 1.0 + nrm(ks[12], (DEPTH, RET_WIDTH), 0.01),
        'w_out': nrm(ks[13], (DEPTH, MIX_WIDTH, D_MODEL), MIX_WIDTH ** -0.5),
        'norm_ffn_w': 1.0 + nrm(ks[14], (DEPTH, D_MODEL), 0.01),
        'w_router': nrm(ks[15], (DEPTH, D_MODEL, N_EXPERTS), D_MODEL ** -0.5),
        'b_router': nrm(ks[16], (DEPTH, N_EXPERTS), 0.01),
        'w_gate_up': nrm(ks[17], (DEPTH, N_EXPERTS, D_MODEL, 2 * D_FF), D_MODEL ** -0.5),
        'b_gate_up': nrm(ks[18], (DEPTH, N_EXPERTS, 2 * D_FF), 0.01),
        'w_down': nrm(ks[19], (DEPTH, N_EXPERTS, D_FF, D_MODEL), D_FF ** -0.5),
        'b_down': nrm(ks[20], (DEPTH, N_EXPERTS, D_MODEL), 0.01),
        'norm_final_w': 1.0 + nrm(ks[21], (D_MODEL,), 0.01),
    }


def reference(x_prompt, x_sample, state_swa_k, state_swa_v, state_ret, c_prompt, c_sample,
              norm_mix_w, w_ada, b_ada, w_in, attn_sinks, ret_gn_w, w_out, norm_ffn_w,
              w_router, b_router, w_gate_up, b_gate_up, w_down, b_down, norm_final_w):
    t_p = x_prompt.shape[1]
    t_s = x_sample.shape[1]
    pos_p = jnp.arange(t_p, dtype=jnp.int32)
    pos_s = PAST_LEN + jnp.arange(t_s, dtype=jnp.int32)
    xp, xs = x_prompt, x_sample
    kp_l, vp_l, rp_l, ks_l, vs_l, rs_l = [], [], [], [], [], []
    for l in range(DEPTH):
        sinks = attn_sinks[l].reshape(N_KV_HEADS, GQA_GROUP)
        weights = (norm_mix_w[l], w_ada[l], b_ada[l], w_in[l], ret_gn_w[l], w_out[l], norm_ffn_w[l],
                   w_router[l], b_router[l], w_gate_up[l], b_gate_up[l], w_down[l], b_down[l])
        r0_p = jnp.zeros((xp.shape[0], N_RET_HEADS, RET_KEY_DIM, RET_VAL_DIM), jnp.float32)
        xp, (kp, vp), rp = trunk_layer(xp, c_prompt, pos_p, r0_p,
                                       lambda q, k, v: swa_prompt(q, k, v, sinks), *weights)
        kbuf, vbuf = state_swa_k[l], state_swa_v[l]
        xs, (ksn, vsn), rs = trunk_layer(xs, c_sample, pos_s, state_ret[l],
                                         lambda q, k, v: swa_sample(q, k, v, kbuf, vbuf, sinks, PAST_LEN), *weights)
        kp_l.append(kp)
        vp_l.append(vp)
        rp_l.append(rp.astype(x_prompt.dtype))
        ks_l.append(ksn)
        vs_l.append(vsn)
        rs_l.append(rs.astype(state_ret.dtype))
    y_prompt = rms_norm(xp, norm_final_w)
    y_sample = rms_norm(xs, norm_final_w)
    new_swa_k_prompt = jnp.stack(kp_l)
    new_swa_v_prompt = jnp.stack(vp_l)
    new_ret_prompt = jnp.stack(rp_l)
    new_swa_k_sample = jnp.stack(ks_l)
    new_swa_v_sample = jnp.stack(vs_l)
    new_ret_sample = jnp.stack(rs_l)
    return (y_prompt, y_sample, new_swa_k_prompt, new_swa_v_prompt, new_ret_prompt, new_swa_k_sample, new_swa_v_sample, new_ret_sample)
```

```python
import functools

import numpy as np
import jax
import jax.numpy as jnp
from jax import lax
from jax.experimental import pallas as pl
from jax.experimental.pallas import tpu as pltpu

F32 = jnp.float32
BF16 = jnp.bfloat16
HIGHEST = lax.Precision.HIGHEST

HEAD_DIM = 64
N_ATTN_HEADS = 8
N_KV_HEADS = 2
GQA_GROUP = N_ATTN_HEADS // N_KV_HEADS
WINDOW = 128
ROPE_THETA = 500000.0
ROPE_DIMS = HEAD_DIM // 4
N_RET_HEADS = 8
RET_CHUNK = 128
RET_THETA = 10000.0
N_EXPERTS = 32
TOP_K = 4
SWIGLU_LIMIT = 7.0
SWIGLU_ALPHA = 1.702
NORM_EPS = 1e-5
PAST_LEN = 16384

LANES = 128
SUBLANES = 8
VMEM_LIMIT_BYTES = 56 * 1024 * 1024

TOKEN_TILE = 256
EXPERT_TILE = 256
SAMPLE_BATCH_TILE = 8

NEG_INF = -1e30


def _cparams(semantics):
    return pltpu.CompilerParams(dimension_semantics=semantics, vmem_limit_bytes=VMEM_LIMIT_BYTES)


def _lane_is_low_half(shape):
    return lax.broadcasted_iota(jnp.int32, shape, len(shape) - 1) < HEAD_DIM


def _dot(a, b, precision=None):
    return jnp.dot(a, b, preferred_element_type=F32, precision=precision)


def _dot_nt(a, b, precision=None):
    return lax.dot_general(a, b, (((1,), (1,)), ((), ())), preferred_element_type=F32, precision=precision)


def _dot_tn(a, b, precision=None):
    return lax.dot_general(a, b, (((0,), (0,)), ((), ())), preferred_element_type=F32, precision=precision)


def _rms_norm(x, w):
    return x * lax.rsqrt(jnp.mean(x * x, axis=-1, keepdims=True) + NORM_EPS) * w


def _silu(x):
    return x * jax.nn.sigmoid(x)


def _ada_body(c_ref, w_ref, b_ref, o_ref):
    a = _silu(c_ref[...])
    o_ref[0] = _dot(a, w_ref[0], HIGHEST) + b_ref[0]


def _ada_modulation(c_all, w_ada, b_ada):
    depth, d, cols = w_ada.shape
    rows = c_all.shape[0]
    tn = 1024
    return pl.pallas_call(
        _ada_body,
        grid=(depth, cols // tn),
        in_specs=[
            pl.BlockSpec((rows, d), lambda l, j: (0, 0)),
            pl.BlockSpec((1, d, tn), lambda l, j: (l, 0, j)),
            pl.BlockSpec((1, 1, tn), lambda l, j: (l, 0, j)),
        ],
        out_specs=pl.BlockSpec((1, rows, tn), lambda l, j: (l, 0, j)),
        out_shape=jax.ShapeDtypeStruct((depth, rows, cols), F32),
        compiler_params=_cparams(("arbitrary", "arbitrary")),
        name="ada_modulation",
    )(c_all, w_ada, b_ada.reshape(depth, 1, cols))


def _tile_vec(i, n_ptiles, tiles_per_batch, n_pbatch, prompt_ref, sample_ref):
    b = jnp.minimum(i // tiles_per_batch, n_pbatch - 1)
    return jnp.where(i >= n_ptiles, sample_ref[...], prompt_ref[pl.ds(b, 1), :])


def _prompt_vec_spec(rows, d):
    return pl.BlockSpec((rows, d), lambda i: (0, 0))


def _sample_vec_spec(n_ptiles, d):
    return pl.BlockSpec((TOKEN_TILE, d), lambda i: (jnp.maximum(i - n_ptiles, 0), 0))


def _rotate(xc, tab_ref, base, shift):
    return (xc * tab_ref[base] + pltpu.roll(xc, shift, 1) * tab_ref[base + 1]
            + pltpu.roll(xc, LANES - shift, 1) * tab_ref[base + 2])


def _inproj_body(n_ptiles, tiles_per_batch, n_pbatch,
                 x_ref, nw_ref, shp_ref, scp_ref, shs_ref, scs_ref, w_ref, tab_ref, p_ref, kv_ref):
    i = pl.program_id(0)
    shift = _tile_vec(i, n_ptiles, tiles_per_batch, n_pbatch, shp_ref, shs_ref)
    scale = _tile_vec(i, n_ptiles, tiles_per_batch, n_pbatch, scp_ref, scs_ref)
    h = _rms_norm(x_ref[...], nw_ref[...]) * (1.0 + scale) + shift
    proj = _dot(h.astype(BF16), w_ref[...])
    q_cols = N_ATTN_HEADS * HEAD_DIM
    kv_cols = N_KV_HEADS * HEAD_DIM
    r_cols = N_RET_HEADS * HEAD_DIM
    o_ka = q_cols
    o_va = o_ka + kv_cols
    o_qr = o_va + kv_cols
    o_kr = o_qr + r_cols
    o_vr = o_kr + r_cols
    o_g = o_vr + r_cols
    attn_scale = HEAD_DIM ** -0.5
    ret_scale = HEAD_DIM ** -0.5
    half_a = ROPE_DIMS // 2
    half_r = HEAD_DIM // 2
    for c in range(q_cols // LANES):
        xc = proj[:, c * LANES:(c + 1) * LANES]
        p_ref[:, c * LANES:(c + 1) * LANES] = (_rotate(xc, tab_ref, 0, half_a) * attn_scale).astype(BF16)
    kv_ref[:, 0:kv_cols] = _rotate(proj[:, o_ka:o_ka + kv_cols], tab_ref, 0, half_a)
    kv_ref[:, kv_cols:2 * kv_cols] = proj[:, o_va:o_va + kv_cols]
    for c in range(r_cols // LANES):
        xq = proj[:, o_qr + c * LANES:o_qr + (c + 1) * LANES]
        xk = proj[:, o_kr + c * LANES:o_kr + (c + 1) * LANES]
        p_ref[:, q_cols + c * LANES:q_cols + (c + 1) * LANES] = _rotate(xq, tab_ref, 3, half_r).astype(BF16)
        p_ref[:, q_cols + r_cols + c * LANES:q_cols + r_cols + (c + 1) * LANES] = (
            _rotate(xk, tab_ref, 3, half_r) * ret_scale).astype(BF16)
    p_ref[:, q_cols + 2 * r_cols:q_cols + 3 * r_cols] = proj[:, o_vr:o_vr + r_cols].astype(BF16)
    p_ref[:, q_cols + 3 * r_cols:q_cols + 4 * r_cols] = proj[:, o_g:o_g + r_cols].astype(BF16)


def _input_projection(x_all, norm_w, sh_p, sc_p, sh_s, sc_s, w_in_bf16, rope_tab, n_ptiles, tiles_per_batch, n_pbatch):
    n, d = x_all.shape
    in_cols = w_in_bf16.shape[1]
    p_cols = in_cols - 2 * N_KV_HEADS * HEAD_DIM
    kv_cols = 2 * N_KV_HEADS * HEAD_DIM
    body = functools.partial(_inproj_body, n_ptiles, tiles_per_batch, n_pbatch)
    tab_index = lambda i: (0, jnp.where(i >= n_ptiles, tiles_per_batch, i % tiles_per_batch), 0)
    return pl.pallas_call(
        body,
        grid=(n // TOKEN_TILE,),
        in_specs=[
            pl.BlockSpec((TOKEN_TILE, d), lambda i: (i, 0)),
            pl.BlockSpec((1, d), lambda i: (0, 0)),
            _prompt_vec_spec(sh_p.shape[0], d),
            _prompt_vec_spec(sc_p.shape[0], d),
            _sample_vec_spec(n_ptiles, d),
            _sample_vec_spec(n_ptiles, d),
            pl.BlockSpec((d, in_cols), lambda i: (0, 0)),
            pl.BlockSpec((6, TOKEN_TILE, LANES), tab_index),
        ],
        out_specs=[
            pl.BlockSpec((TOKEN_TILE, p_cols), lambda i: (i, 0)),
            pl.BlockSpec((TOKEN_TILE, kv_cols), lambda i: (i, 0)),
        ],
        out_shape=[
            jax.ShapeDtypeStruct((n, p_cols), BF16),
            jax.ShapeDtypeStruct((n, kv_cols), F32),
        ],
        compiler_params=_cparams(("arbitrary",)),
        name="input_projection",
    )(x_all, norm_w.reshape(1, d), sh_p, sc_p, sh_s, sc_s, w_in_bf16, rope_tab)


def _rope_tables(t_prompt, t_sample):
    pos = jnp.concatenate([jnp.arange(t_prompt, dtype=jnp.int32),
                           PAST_LEN + (jnp.arange(TOKEN_TILE, dtype=jnp.int32) % t_sample)])
    d = np.arange(LANES) % HEAD_DIM

    def tables(n_dims, theta):
        half = n_dims // 2
        freqs = jnp.power(jnp.float32(theta), -jnp.arange(half, dtype=jnp.float32) / half)
        ang = pos.astype(jnp.float32)[:, None] * freqs[None, :]
        cos, sin = jnp.cos(ang), jnp.sin(ang)
        fidx = np.where(d < n_dims, d % half, 0)
        cos_l = jnp.where(jnp.asarray(d < n_dims)[None, :], cos[:, fidx], 1.0)
        sin_l = sin[:, fidx]
        upper = jnp.asarray((d >= half) & (d < n_dims))[None, :]
        lower = jnp.asarray(d < half)[None, :]
        return [cos_l, jnp.where(upper, sin_l, 0.0), jnp.where(lower, -sin_l, 0.0)]

    return jnp.stack(tables(ROPE_DIMS, ROPE_THETA) + tables(HEAD_DIM, RET_THETA)).astype(F32)


def _retention_tables(c):
    h = N_RET_HEADS
    log_gamma = jnp.log(1.0 - jnp.power(2.0, -5.0 - jnp.arange(h, dtype=jnp.float32)))
    idx = jnp.arange(c, dtype=jnp.float32)
    diff = idx[:, None] - idx[None, :]
    decay_mask = jnp.where(diff >= 0, jnp.exp(log_gamma[:, None, None] * jnp.maximum(diff, 0.0)), 0.0)
    k_dec = jnp.exp(log_gamma[None, :] * (c - 1 - idx)[:, None])
    q_dec = jnp.exp(log_gamma[None, :] * (idx + 1.0)[:, None])
    chunk_decay = jnp.exp(log_gamma * c)
    rep = lambda a: jnp.repeat(a, HEAD_DIM, axis=-1)
    return decay_mask.astype(F32), rep(q_dec).astype(F32), rep(k_dec).astype(F32), rep(chunk_decay[None, :]).astype(F32)


def _group_norm_pair(o, avg):
    mu = _dot(o, avg, HIGHEST)
    dlt = o - mu
    var = _dot(dlt * dlt, avg, HIGHEST)
    return dlt * lax.rsqrt(var + NORM_EPS)


def _pair_average_matrix():
    r = np.arange(LANES)
    return jnp.asarray(((r[:, None] // HEAD_DIM) == (r[None, :] // HEAD_DIM)).astype(np.float32) / HEAD_DIM)


def _block_diag_mask():
    r = np.arange(LANES)
    return jnp.asarray(((r[:, None] // HEAD_DIM) == (r[None, :] // HEAD_DIM)).astype(np.float32))


def _prompt_mixer_body(sink_ref, p_ref, kvc_ref, kvp_ref, dmask_ref, qdec_ref, kdec_ref, cdec_ref,
                       bd_ref, avg_ref, gnw_ref, cat_ref, r_ref):
    jb = pl.program_id(1)
    blk = WINDOW
    q_cols = N_ATTN_HEADS * HEAD_DIM
    r_cols = N_RET_HEADS * HEAD_DIM
    kvw = N_KV_HEADS * HEAD_DIM

    @pl.when(jb == 0)
    def _():
        r_ref[...] = jnp.zeros_like(r_ref)

    low = _lane_is_low_half((blk, LANES))
    kband = jnp.concatenate([kvp_ref[:, 0:kvw], kvc_ref[:, 0:kvw]], axis=0)
    vband = jnp.concatenate([kvp_ref[:, kvw:2 * kvw], kvc_ref[:, kvw:2 * kvw]], axis=0)
    low2 = _lane_is_low_half((2 * blk, LANES))
    kswap = pltpu.roll(kband, HEAD_DIM, 1)
    vswap = pltpu.roll(vband, HEAD_DIM, 1)
    qi = lax.broadcasted_iota(jnp.int32, (blk, 2 * blk), 0)
    kj = lax.broadcasted_iota(jnp.int32, (blk, 2 * blk), 1)
    dist = blk + qi - kj
    mask = (dist >= 0) & (dist < WINDOW) & ((kj >= blk) | (jb > 0))
    for kvh in range(N_KV_HEADS):
        keep = low2 if kvh == 0 else jnp.logical_not(low2)
        k2 = jnp.where(keep, kband, kswap).astype(BF16)
        v2 = jnp.where(keep, vband, vswap).astype(BF16)
        pieces = []
        for pr in range(GQA_GROUP // 2):
            c0 = (kvh * (GQA_GROUP // 2) + pr) * LANES
            qp = p_ref[:, c0:c0 + LANES]
            pieces.append(jnp.where(low, qp, jnp.zeros_like(qp)))
            pieces.append(jnp.where(low, jnp.zeros_like(qp), qp))
        qs = jnp.concatenate(pieces, axis=0)
        s = _dot_nt(qs, k2)
        es, inv = [], []
        for hh in range(GQA_GROUP):
            sink = sink_ref[kvh * GQA_GROUP + hh]
            sh = jnp.where(mask, s[hh * blk:(hh + 1) * blk], NEG_INF)
            m = jnp.maximum(jnp.max(sh, axis=-1, keepdims=True), sink)
            e = jnp.exp(sh - m)
            den = jnp.sum(e, axis=-1, keepdims=True) + jnp.exp(sink - m)
            es.append(e.astype(BF16))
            inv.append(1.0 / den)
        o = _dot(jnp.concatenate(es, axis=0), v2)
        for pr in range(GQA_GROUP // 2):
            o_lo = o[(2 * pr) * blk:(2 * pr + 1) * blk] * inv[2 * pr]
            o_hi = o[(2 * pr + 1) * blk:(2 * pr + 2) * blk] * inv[2 * pr + 1]
            c0 = (kvh * (GQA_GROUP // 2) + pr) * LANES
            cat_ref[:, c0:c0 + LANES] = jnp.where(low, o_lo, o_hi).astype(BF16)
    for pp in range(N_RET_HEADS // 2):
        c0 = q_cols + pp * LANES
        qp = p_ref[:, c0:c0 + LANES]
        kp = p_ref[:, c0 + r_cols:c0 + r_cols + LANES]
        vp = p_ref[:, c0 + 2 * r_cols:c0 + 2 * r_cols + LANES]
        gp = p_ref[:, c0 + 3 * r_cols:c0 + 3 * r_cols + LANES].astype(F32)
        zero = jnp.zeros_like(qp)
        sc_lo = (_dot_nt(jnp.where(low, qp, zero), kp) * dmask_ref[2 * pp]).astype(BF16)
        sc_hi = (_dot_nt(jnp.where(low, zero, qp), kp) * dmask_ref[2 * pp + 1]).astype(BF16)
        o_inner = jnp.where(low, _dot(sc_lo, vp), _dot(sc_hi, vp))
        r_prev = r_ref[0, pp]
        qd = (qp.astype(F32) * qdec_ref[:, pp * LANES:(pp + 1) * LANES]).astype(BF16)
        o_cross = _dot(qd, r_prev.astype(BF16))
        kd = (kp.astype(F32) * kdec_ref[:, pp * LANES:(pp + 1) * LANES]).astype(BF16)
        kv_new = _dot_tn(kd, vp) * bd_ref[...]
        r_ref[0, pp] = cdec_ref[pp] * r_prev + kv_new
        y = _group_norm_pair(o_inner + o_cross, avg_ref[...])
        y = y * gnw_ref[:, pp * LANES:(pp + 1) * LANES] * _silu(gp)
        cat_ref[:, q_cols + pp * LANES:q_cols + (pp + 1) * LANES] = y.astype(BF16)


def _prompt_mixer(p_all, kv_all, sinks, gn_w, n_pbatch, t_prompt, ret_tabs):
    blk = WINDOW
    nb = t_prompt // blk
    p_cols = p_all.shape[1]
    kv_cols = kv_all.shape[1]
    d_mix = (N_ATTN_HEADS + N_RET_HEADS) * HEAD_DIM
    decay_mask, q_dec, k_dec, chunk_decay = ret_tabs
    n_pairs = N_RET_HEADS // 2
    bd = _block_diag_mask()
    cdec = jnp.stack([bd * chunk_decay[0, pp * LANES:(pp + 1) * LANES][:, None] for pp in range(n_pairs)])
    const2 = lambda shape: pl.BlockSpec(shape, lambda b, j: (0,) * len(shape))
    return pl.pallas_call(
        _prompt_mixer_body,
        grid=(n_pbatch, nb),
        in_specs=[
            pl.BlockSpec(memory_space=pltpu.SMEM),
            pl.BlockSpec((blk, p_cols), lambda b, j: (b * nb + j, 0)),
            pl.BlockSpec((blk, kv_cols), lambda b, j: (b * nb + j, 0)),
            pl.BlockSpec((blk, kv_cols), lambda b, j: (b * nb + jnp.maximum(j - 1, 0), 0)),
            const2((N_RET_HEADS, blk, blk)),
            const2((blk, N_RET_HEADS * HEAD_DIM)),
            const2((blk, N_RET_HEADS * HEAD_DIM)),
            const2((n_pairs, LANES, LANES)),
            const2((LANES, LANES)),
            const2((LANES, LANES)),
            const2((1, N_RET_HEADS * HEAD_DIM)),
        ],
        out_specs=[
            pl.BlockSpec((blk, d_mix), lambda b, j: (b * nb + j, 0)),
            pl.BlockSpec((1, n_pairs, LANES, LANES), lambda b, j: (b, 0, 0, 0)),
        ],
        out_shape=[
            jax.ShapeDtypeStruct((n_pbatch * t_prompt, d_mix), BF16),
            jax.ShapeDtypeStruct((n_pbatch, n_pairs, LANES, LANES), F32),
        ],
        compiler_params=_cparams(("arbitrary", "arbitrary")),
        name="prompt_mixer",
    )(sinks, p_all, kv_all, kv_all, decay_mask, q_dec, k_dec, cdec, bd, _pair_average_matrix(), gn_w.reshape(1, -1))


def _sample_mixer_body(t_new, sink_ref, p_ref, kv_ref, kbuf_ref, vbuf_ref, ret_ref, d4_ref, qdec_ref, kdec_ref,
                       cdec_ref, avg_ref, gnw_ref, cat_ref, knew_ref, vnew_ref, rnew_ref, ka_ref, va_ref, qs_ref):
    w = kbuf_ref.shape[1]
    q_cols = N_ATTN_HEADS * HEAD_DIM
    r_cols = N_RET_HEADS * HEAD_DIM
    kvw = N_KV_HEADS * HEAD_DIM
    n_keys = ka_ref.shape[0]
    rows_per_head = SUBLANES
    ka_ref[...] = jnp.zeros_like(ka_ref)
    va_ref[...] = jnp.zeros_like(va_ref)
    qs_ref[...] = jnp.zeros_like(qs_ref)
    low4 = _lane_is_low_half((t_new, LANES))
    lowk = _lane_is_low_half((n_keys, LANES))
    n_rows = GQA_GROUP * rows_per_head
    row = lax.broadcasted_iota(jnp.int32, (n_rows, n_keys), 0)
    key = lax.broadcasted_iota(jnp.int32, (n_rows, n_keys), 1)
    t_of_row = row % rows_per_head
    mask = (t_of_row < t_new) & (key > t_of_row) & (key <= t_of_row + WINDOW) & (key < w + t_new)
    head_of_row = lax.broadcasted_iota(jnp.int32, (n_rows, 1), 0) // rows_per_head

    def per_batch(b, carry):
        ka_ref[0:w, :] = kbuf_ref[b]
        va_ref[0:w, :] = vbuf_ref[b]
        ka_ref[w:w + t_new, :] = kv_ref[b][:, 0:kvw]
        va_ref[w:w + t_new, :] = kv_ref[b][:, kvw:2 * kvw]
        knew_ref[b] = ka_ref[t_new:t_new + w, :]
        vnew_ref[b] = va_ref[t_new:t_new + w, :]
        for h in range(N_ATTN_HEADS):
            c0 = (h // 2) * LANES
            qp = p_ref[b][:, c0:c0 + LANES]
            qs_ref[h * rows_per_head:h * rows_per_head + t_new, :] = jnp.where(
                low4 if h % 2 == 0 else jnp.logical_not(low4), qp, 0.0)
        kall = ka_ref[...]
        vall = va_ref[...]
        kswap = pltpu.roll(kall, HEAD_DIM, 1)
        vswap = pltpu.roll(vall, HEAD_DIM, 1)
        for kvh in range(N_KV_HEADS):
            keep = lowk if kvh == 0 else jnp.logical_not(lowk)
            k2 = jnp.where(keep, kall, kswap)
            v2 = jnp.where(keep, vall, vswap)
            qs = qs_ref[kvh * n_rows:(kvh + 1) * n_rows, :]
            s = jnp.where(mask, _dot_nt(qs, k2, HIGHEST), NEG_INF)
            sink = jnp.zeros((n_rows, 1), F32)
            for hh in range(GQA_GROUP):
                sink = jnp.where(head_of_row == hh, sink_ref[kvh * GQA_GROUP + hh], sink)
            m = jnp.maximum(jnp.max(s, axis=-1, keepdims=True), sink)
            e = jnp.exp(s - m)
            den = jnp.sum(e, axis=-1, keepdims=True) + jnp.exp(sink - m)
            o = _dot(e, v2, HIGHEST) / den
            for pr in range(GQA_GROUP // 2):
                o_lo = o[(2 * pr) * rows_per_head:(2 * pr) * rows_per_head + t_new]
                o_hi = o[(2 * pr + 1) * rows_per_head:(2 * pr + 1) * rows_per_head + t_new]
                c0 = (kvh * (GQA_GROUP // 2) + pr) * LANES
                cat_ref[b, :, c0:c0 + LANES] = jnp.where(low4, o_lo, o_hi)
        for pp in range(N_RET_HEADS // 2):
            c0 = q_cols + pp * LANES
            qp = p_ref[b][:, c0:c0 + LANES]
            kp = p_ref[b][:, c0 + r_cols:c0 + r_cols + LANES]
            vp = p_ref[b][:, c0 + 2 * r_cols:c0 + 2 * r_cols + LANES]
            gp = p_ref[b][:, c0 + 3 * r_cols:c0 + 3 * r_cols + LANES]
            r0 = ret_ref[b, pp]
            q_lo = jnp.where(low4, qp, 0.0)
            q_hi = jnp.where(low4, 0.0, qp)
            sc_lo = _dot_nt(q_lo, kp, HIGHEST) * d4_ref[2 * pp]
            sc_hi = _dot_nt(q_hi, kp, HIGHEST) * d4_ref[2 * pp + 1]
            o_inner = jnp.where(low4, _dot(sc_lo, vp, HIGHEST), _dot(sc_hi, vp, HIGHEST))
            qdec = qdec_ref[:, pp * LANES:(pp + 1) * LANES]
            oc_lo = _dot(q_lo * qdec, r0, HIGHEST)
            oc_hi = _dot(q_hi * qdec, r0, HIGHEST)
            o_cross = jnp.concatenate([oc_lo, oc_hi], axis=1)
            kd = kp * kdec_ref[:, pp * LANES:(pp + 1) * LANES]
            kv_full = _dot_tn(kd, vp, HIGHEST)
            kv_pair = jnp.concatenate([kv_full[0:HEAD_DIM, 0:HEAD_DIM], kv_full[HEAD_DIM:, HEAD_DIM:]], axis=0)
            rnew_ref[b, pp] = cdec_ref[pp] * r0 + kv_pair
            y = _group_norm_pair(o_inner + o_cross, avg_ref[...])
            y = y * gnw_ref[:, pp * LANES:(pp + 1) * LANES] * _silu(gp)
            cat_ref[b, :, q_cols + pp * LANES:q_cols + (pp + 1) * LANES] = y
        return carry

    lax.fori_loop(0, kbuf_ref.shape[0], per_batch, 0)


def _sample_mixer(p_s, kv_s, k_buf, v_buf, ret0, sinks, gn_w, ret_tabs):
    nb, t_new, p_cols = p_s.shape
    kv_cols = kv_s.shape[2]
    w = k_buf.shape[1]
    bb = SAMPLE_BATCH_TILE
    n_pairs = N_RET_HEADS // 2
    d_mix = (N_ATTN_HEADS + N_RET_HEADS) * HEAD_DIM
    decay_mask, q_dec, k_dec, chunk_decay = ret_tabs
    n_keys = ((w + t_new + SUBLANES - 1) // SUBLANES) * SUBLANES
    cdec = jnp.stack([jnp.broadcast_to(chunk_decay[0, pp * LANES:(pp + 1) * LANES][:, None], (LANES, HEAD_DIM))
                      for pp in range(n_pairs)])
    const1 = lambda shape: pl.BlockSpec(shape, lambda i: (0,) * len(shape))
    body = functools.partial(_sample_mixer_body, t_new)
    return pl.pallas_call(
        body,
        grid=(nb // bb,),
        in_specs=[
            pl.BlockSpec(memory_space=pltpu.SMEM),
            pl.BlockSpec((bb, t_new, p_cols), lambda i: (i, 0, 0)),
            pl.BlockSpec((bb, t_new, kv_cols), lambda i: (i, 0, 0)),
            pl.BlockSpec((bb, w, LANES), lambda i: (i, 0, 0)),
            pl.BlockSpec((bb, w, LANES), lambda i: (i, 0, 0)),
            pl.BlockSpec((bb, n_pairs, LANES, HEAD_DIM), lambda i: (i, 0, 0, 0)),
            const1((N_RET_HEADS, t_new, t_new)),
            const1((t_new, N_RET_HEADS * HEAD_DIM)),
            const1((t_new, N_RET_HEADS * HEAD_DIM)),
            const1((n_pairs, LANES, HEAD_DIM)),
            const1((LANES, LANES)),
            const1((1, N_RET_HEADS * HEAD_DIM)),
        ],
        out_specs=[
            pl.BlockSpec((bb, t_new, d_mix), lambda i: (i, 0, 0)),
            pl.BlockSpec((bb, w, LANES), lambda i: (i, 0, 0)),
            pl.BlockSpec((bb, w, LANES), lambda i: (i, 0, 0)),
            pl.BlockSpec((bb, n_pairs, LANES, HEAD_DIM), lambda i: (i, 0, 0, 0)),
        ],
        out_shape=[
            jax.ShapeDtypeStruct((nb, t_new, d_mix), F32),
            jax.ShapeDtypeStruct((nb, w, LANES), F32),
            jax.ShapeDtypeStruct((nb, w, LANES), F32),
            jax.ShapeDtypeStruct((nb, n_pairs, LANES, HEAD_DIM), F32),
        ],
        scratch_shapes=[
            pltpu.VMEM((n_keys, LANES), F32),
            pltpu.VMEM((n_keys, LANES), F32),
            pltpu.VMEM((N_ATTN_HEADS * SUBLANES, LANES), F32),
        ],
        compiler_params=_cparams(("arbitrary",)),
        name="sample_mixer",
    )(sinks, p_s, kv_s, k_buf, v_buf, ret0, decay_mask, q_dec, k_dec, cdec, _pair_average_matrix(), gn_w.reshape(1, -1))


def _router_body(n_ptiles, tiles_per_batch, n_pbatch,
                 x_ref, catp_ref, cats_ref, wout_ref, g1p_ref, g1s_ref, nw_ref, shp_ref, scp_ref, shs_ref, scs_ref,
                 wr_ref, br_ref, x1_ref, h2_ref, idx_ref, gate_ref):
    i = pl.program_id(0)
    tv = functools.partial(_tile_vec, i, n_ptiles, tiles_per_batch, n_pbatch)
    cat = jnp.where(i >= n_ptiles, cats_ref[...], catp_ref[...])
    x1 = x_ref[...] + tv(g1p_ref, g1s_ref) * _dot(cat, wout_ref[...])
    x1_ref[...] = x1
    h2 = _rms_norm(x1, nw_ref[...]) * (1.0 + tv(scp_ref, scs_ref)) + tv(shp_ref, shs_ref)
    h2_ref[...] = h2
    logits = _dot_nt(wr_ref[...], h2, HIGHEST) + br_ref[...]
    e_iota = lax.broadcasted_iota(jnp.int32, logits.shape, 0)
    vals, sels = [], []
    for _ in range(TOP_K):
        m = jnp.max(logits, axis=0, keepdims=True)
        sel = jnp.min(jnp.where(logits == m, e_iota, N_EXPERTS), axis=0, keepdims=True)
        vals.append(m)
        sels.append(sel)
        logits = jnp.where(e_iota == sel, -jnp.inf, logits)
    es = [jnp.exp(v - vals[0]) for v in vals]
    den = es[0]
    for e in es[1:]:
        den = den + e
    for k in range(TOP_K):
        idx_ref[k:k + 1, :] = sels[k]
        gate_ref[k:k + 1, :] = es[k] / den


def _outproj_router(x_all, cat_p, cat_s, w_out_bf16, g1, norm_w, sh2, sc2, w_router_t, b_router,
                    n_ptiles, tiles_per_batch, n_pbatch):
    n, d = x_all.shape
    d_mix = cat_p.shape[1]
    body = functools.partial(_router_body, n_ptiles, tiles_per_batch, n_pbatch)
    pv = lambda a: _prompt_vec_spec(a.shape[0], d)
    sv = _sample_vec_spec(n_ptiles, d)
    tile = lambda cols: pl.BlockSpec((TOKEN_TILE, cols), lambda i: (i, 0))
    return pl.pallas_call(
        body,
        grid=(n // TOKEN_TILE,),
        in_specs=[
            tile(d),
            pl.BlockSpec((TOKEN_TILE, d_mix), lambda i: (jnp.minimum(i, n_ptiles - 1), 0)),
            pl.BlockSpec((TOKEN_TILE, d_mix), lambda i: (jnp.maximum(i - n_ptiles, 0), 0)),
            pl.BlockSpec((d_mix, d), lambda i: (0, 0)),
            pv(g1[0]), sv,
            pl.BlockSpec((1, d), lambda i: (0, 0)),
            pv(sh2[0]), pv(sc2[0]), sv, sv,
            pl.BlockSpec((N_EXPERTS, d), lambda i: (0, 0)),
            pl.BlockSpec((N_EXPERTS, 1), lambda i: (0, 0)),
        ],
        out_specs=[
            tile(d), tile(d),
            pl.BlockSpec((TOP_K, TOKEN_TILE), lambda i: (0, i)),
            pl.BlockSpec((TOP_K, TOKEN_TILE), lambda i: (0, i)),
        ],
        out_shape=[
            jax.ShapeDtypeStruct((n, d), F32),
            jax.ShapeDtypeStruct((n, d), F32),
            jax.ShapeDtypeStruct((TOP_K, n), jnp.int32),
            jax.ShapeDtypeStruct((TOP_K, n), F32),
        ],
        compiler_params=_cparams(("arbitrary",)),
        name="outproj_router",
    )(x_all, cat_p, cat_s, w_out_bf16, g1[0], g1[1], norm_w.reshape(1, d), sh2[0], sc2[0], sh2[1], sc2[1],
      w_router_t, b_router.reshape(N_EXPERTS, 1))


def _row_gather_copies(src_hbm, rows_ref, buf, sem, n_rows, start):
    def issue(r, c):
        pltpu.make_async_copy(src_hbm.at[pl.ds(rows_ref[0, 0, r], 1)], buf.at[pl.ds(r, 1)], sem).start()
        return c
    if start:
        lax.fori_loop(0, n_rows, issue, 0)
    else:
        pltpu.make_async_copy(src_hbm.at[pl.ds(0, n_rows)], buf, sem).wait()


def _expert_body(te_ref, nt_ref, rows_cur_ref, rows_nxt_ref, h_hbm, wg_ref, wu_ref, bg_ref, bu_ref, wd_ref, bd_ref,
                 y_ref, xbuf, sems):
    j = pl.program_id(0)
    n_used = nt_ref[0]
    slot = j % 2
    tm = EXPERT_TILE

    @pl.when(jnp.logical_and(j == 0, n_used > 0))
    def _():
        _row_gather_copies(h_hbm, rows_cur_ref, xbuf.at[0], sems.at[0], tm, True)

    @pl.when(j + 1 < n_used)
    def _():
        _row_gather_copies(h_hbm, rows_nxt_ref, xbuf.at[1 - slot], sems.at[1 - slot], tm, True)

    @pl.when(j < n_used)
    def _():
        _row_gather_copies(h_hbm, rows_cur_ref, xbuf.at[slot], sems.at[slot], tm, False)
        x = xbuf[slot].astype(BF16)
        glu = jnp.minimum(_dot(x, wg_ref[0]) + bg_ref[0], SWIGLU_LIMIT)
        lin = jnp.clip(_dot(x, wu_ref[0]) + bu_ref[0], -SWIGLU_LIMIT, SWIGLU_LIMIT)
        act = glu * jax.nn.sigmoid(SWIGLU_ALPHA * glu) * (lin + 1.0)
        y_ref[...] = _dot(act.astype(BF16), wd_ref[0]) + bd_ref[0]

    @pl.when(j >= n_used)
    def _():
        y_ref[...] = jnp.zeros_like(y_ref)


def _routed_experts(h2, tile_expert, n_used, slot_rows, wg, wu, bg, bu, wd, bd):
    n, d = h2.shape
    f = wg.shape[2]
    tm = EXPERT_TILE
    n_tiles = slot_rows.shape[0]
    grid_spec = pltpu.PrefetchScalarGridSpec(
        num_scalar_prefetch=2,
        grid=(n_tiles,),
        in_specs=[
            pl.BlockSpec((1, 1, tm), lambda j, te, nt: (j, 0, 0), memory_space=pltpu.SMEM),
            pl.BlockSpec((1, 1, tm), lambda j, te, nt: (jnp.minimum(j + 1, n_tiles - 1), 0, 0), memory_space=pltpu.SMEM),
            pl.BlockSpec(memory_space=pl.ANY),
            pl.BlockSpec((1, d, f), lambda j, te, nt: (te[j], 0, 0)),
            pl.BlockSpec((1, d, f), lambda j, te, nt: (te[j], 0, 0)),
            pl.BlockSpec((1, 1, f), lambda j, te, nt: (te[j], 0, 0)),
            pl.BlockSpec((1, 1, f), lambda j, te, nt: (te[j], 0, 0)),
            pl.BlockSpec((1, f, d), lambda j, te, nt: (te[j], 0, 0)),
            pl.BlockSpec((1, 1, d), lambda j, te, nt: (te[j], 0, 0)),
        ],
        out_specs=pl.BlockSpec((tm, d), lambda j, te, nt: (j, 0)),
        scratch_shapes=[pltpu.VMEM((2, tm, d), F32), pltpu.SemaphoreType.DMA((2,))],
    )
    return pl.pallas_call(
        _expert_body,
        grid_spec=grid_spec,
        out_shape=jax.ShapeDtypeStruct((n_tiles * tm, d), F32),
        compiler_params=_cparams(("arbitrary",)),
        name="routed_experts",
    )(tile_expert, n_used, slot_rows, slot_rows, h2, wg, wu, bg, bu, wd, bd)


def _routing_plan(top_idx, tm):
    k, n = top_idx.shape
    e = top_idx.reshape(-1)
    n_pairs = k * n
    n_tiles = (n_pairs + N_EXPERTS * (tm - 1)) // tm + 1
    counts = jnp.zeros((N_EXPERTS,), jnp.int32).at[e].add(1)
    tiles_per = (counts + tm - 1) // tm
    tile_end = jnp.cumsum(tiles_per)
    tile_start = tile_end - tiles_per
    order = jnp.argsort(e, stable=True).astype(jnp.int32)
    group_start = jnp.cumsum(counts) - counts
    e_sorted = e[order]
    rank = jnp.arange(n_pairs, dtype=jnp.int32) - group_start[e_sorted]
    slot_sorted = tile_start[e_sorted] * tm + rank
    slot_of_pair = jnp.zeros((n_pairs,), jnp.int32).at[order].set(slot_sorted)
    token_of_pair = (jnp.arange(n_pairs, dtype=jnp.int32) % n)
    slot_rows = jnp.zeros((n_tiles * tm,), jnp.int32).at[slot_sorted].set(token_of_pair[order])
    n_used = tile_end[-1]
    tile_ids = jnp.arange(n_tiles, dtype=jnp.int32)
    tile_expert = jnp.minimum(jnp.searchsorted(tile_end, tile_ids, side="right"), N_EXPERTS - 1).astype(jnp.int32)
    last_expert = tile_expert[jnp.maximum(n_used - 1, 0)]
    tile_expert = jnp.where(tile_ids < n_used, tile_expert, last_expert)
    return (tile_expert, n_used.reshape(1).astype(jnp.int32), slot_rows.reshape(n_tiles, 1, tm),
            slot_of_pair.reshape(k, n))


def _combine_body(n_ptiles, tiles_per_batch, n_pbatch, final_norm,
                  slots_cur_ref, slots_nxt_ref, x1_ref, gate_ref, g2p_ref, g2s_ref, nfw_ref, y_hbm, out_ref, ybuf, sems):
    i = pl.program_id(0)
    n_steps = pl.num_programs(0)
    slot = i % 2
    tm = TOKEN_TILE

    def copies(rows_ref, sl, start):
        for k in range(TOP_K):
            def issue(r, c, k=k):
                pltpu.make_async_copy(y_hbm.at[pl.ds(rows_ref[0, k, r], 1)], ybuf.at[sl, k, pl.ds(r, 1)],
                                      sems.at[sl]).start()
                return c
            if start:
                lax.fori_loop(0, tm, issue, 0)
            else:
                pltpu.make_async_copy(y_hbm.at[pl.ds(0, tm)], ybuf.at[sl, k], sems.at[sl]).wait()

    @pl.when(i == 0)
    def _():
        copies(slots_cur_ref, 0, True)

    @pl.when(i + 1 < n_steps)
    def _():
        copies(slots_nxt_ref, 1 - slot, True)

    copies(slots_cur_ref, slot, False)
    acc = ybuf[slot, 0] * gate_ref[:, 0:1]
    for k in range(1, TOP_K):
        acc = acc + ybuf[slot, k] * gate_ref[:, k:k + 1]
    g2 = _tile_vec(i, n_ptiles, tiles_per_batch, n_pbatch, g2p_ref, g2s_ref)
    x2 = x1_ref[...] + g2 * acc
    if final_norm:
        x2 = _rms_norm(x2, nfw_ref[...])
    out_ref[...] = x2


def _combine(x1, y_sorted, slot_of_pair, gates_t, g2, final_w, n_ptiles, tiles_per_batch, n_pbatch, final_norm):
    n, d = x1.shape
    tm = TOKEN_TILE
    n_steps = n // tm
    slots = slot_of_pair.reshape(TOP_K, n_steps, tm).transpose(1, 0, 2)
    body = functools.partial(_combine_body, n_ptiles, tiles_per_batch, n_pbatch, final_norm)
    return pl.pallas_call(
        body,
        grid=(n_steps,),
        in_specs=[
            pl.BlockSpec((1, TOP_K, tm), lambda i: (i, 0, 0), memory_space=pltpu.SMEM),
            pl.BlockSpec((1, TOP_K, tm), lambda i: (jnp.minimum(i + 1, n_steps - 1), 0, 0), memory_space=pltpu.SMEM),
            pl.BlockSpec((tm, d), lambda i: (i, 0)),
            pl.BlockSpec((tm, TOP_K), lambda i: (i, 0)),
            _prompt_vec_spec(g2[0].shape[0], d),
            _sample_vec_spec(n_ptiles, d),
            pl.BlockSpec((1, d), lambda i: (0, 0)),
            pl.BlockSpec(memory_space=pl.ANY),
        ],
        out_specs=pl.BlockSpec((tm, d), lambda i: (i, 0)),
        out_shape=jax.ShapeDtypeStruct((n, d), F32),
        scratch_shapes=[pltpu.VMEM((2, TOP_K, tm, d), F32), pltpu.SemaphoreType.DMA((2,))],
        compiler_params=_cparams(("arbitrary",)),
        name="moe_combine",
    )(slots, slots, x1, gates_t, g2[0], g2[1], final_w.reshape(1, d), y_sorted)


def kernel(x_prompt, x_sample, state_swa_k, state_swa_v, state_ret, c_prompt, c_sample, norm_mix_w, w_ada, b_ada, w_in, attn_sinks, ret_gn_w, w_out, norm_ffn_w, w_router, b_router, w_gate_up, b_gate_up, w_down, b_down, norm_final_w):
    n_pbatch, t_prompt, d = x_prompt.shape
    n_sbatch, t_sample, _ = x_sample.shape
    depth = w_in.shape[0]
    n_p = n_pbatch * t_prompt
    n_s = n_sbatch * t_sample
    win = state_swa_k.shape[2]
    assert d == (N_ATTN_HEADS + N_RET_HEADS) * HEAD_DIM and w_gate_up.shape[1] == N_EXPERTS
    assert t_prompt % TOKEN_TILE == 0 and n_s % TOKEN_TILE == 0 and TOKEN_TILE % t_sample == 0
    assert t_prompt % WINDOW == 0 and RET_CHUNK == WINDOW and win == WINDOW and t_sample <= SUBLANES
    assert n_sbatch % SAMPLE_BATCH_TILE == 0 and n_pbatch <= SUBLANES
    n_ptiles = n_p // TOKEN_TILE
    tiles_per_batch = t_prompt // TOKEN_TILE
    geo = (n_ptiles, tiles_per_batch, n_pbatch)

    pad = SUBLANES - n_pbatch
    c_all = jnp.concatenate([c_prompt, jnp.zeros((pad, d), F32), c_sample], axis=0)
    mod = _ada_modulation(c_all, w_ada, b_ada)

    def mod_vectors(l, j):
        m = mod[l, :, j * d:(j + 1) * d]
        return m[:SUBLANES], jnp.repeat(m[SUBLANES:], t_sample, axis=0)

    rope_tab = _rope_tables(t_prompt, t_sample)
    ret_tabs_p = _retention_tables(RET_CHUNK)
    ret_tabs_s = _retention_tables(t_sample)
    f = w_gate_up.shape[3] // 2

    x_all = jnp.concatenate([x_prompt.reshape(n_p, d), x_sample.reshape(n_s, d)], axis=0)
    kp_l, vp_l, rp_l, ks_l, vs_l, rs_l = [], [], [], [], [], []
    for l in range(depth):
        sh1, sc1, g1, sh2, sc2, g2 = [mod_vectors(l, j) for j in range(6)]
        p_all, kv_all = _input_projection(x_all, norm_mix_w[l], sh1[0], sc1[0], sh1[1], sc1[1],
                                          w_in[l].astype(BF16), rope_tab, *geo)
        cat_p, r_pairs = _prompt_mixer(p_all, kv_all, attn_sinks[l], ret_gn_w[l], n_pbatch, t_prompt, ret_tabs_p)
        p_s = p_all[n_p:].astype(F32).reshape(n_sbatch, t_sample, -1)
        kv_s = kv_all[n_p:].reshape(n_sbatch, t_sample, -1)
        cat_s, k_new, v_new, r_new = _sample_mixer(
            p_s, kv_s, state_swa_k[l].reshape(n_sbatch, win, LANES), state_swa_v[l].reshape(n_sbatch, win, LANES),
            state_ret[l].reshape(n_sbatch, N_RET_HEADS // 2, LANES, HEAD_DIM), attn_sinks[l], ret_gn_w[l], ret_tabs_s)
        x1, h2, top_idx, gates = _outproj_router(
            x_all, cat_p, cat_s.reshape(n_s, d).astype(BF16), w_out[l].astype(BF16), g1, norm_ffn_w[l], sh2, sc2,
            w_router[l].T, b_router[l], *geo)
        tile_expert, n_used, slot_rows, slot_of_pair = _routing_plan(top_idx, EXPERT_TILE)
        wgu = w_gate_up[l].astype(BF16)
        bgu = b_gate_up[l]
        y_sorted = _routed_experts(
            h2, tile_expert, n_used, slot_rows, wgu[:, :, 0::2], wgu[:, :, 1::2],
            bgu[:, None, 0::2], bgu[:, None, 1::2], w_down[l].astype(BF16), b_down[l][:, None, :])
        x_all = _combine(x1, y_sorted, slot_of_pair, gates.T, g2, norm_final_w, *geo, final_norm=(l == depth - 1))

        kv_p = kv_all[:n_p].reshape(n_pbatch, t_prompt, 2, N_KV_HEADS, HEAD_DIM)
        kp_l.append(kv_p[:, t_prompt - WINDOW:, 0])
        vp_l.append(kv_p[:, t_prompt - WINDOW:, 1])
        rp_l.append(jnp.stack([r_pairs[:, :, :HEAD_DIM, :HEAD_DIM], r_pairs[:, :, HEAD_DIM:, HEAD_DIM:]], axis=2)
                    .reshape(n_pbatch, N_RET_HEADS, HEAD_DIM, HEAD_DIM))
        ks_l.append(k_new.reshape(n_sbatch, win, N_KV_HEADS, HEAD_DIM))
        vs_l.append(v_new.reshape(n_sbatch, win, N_KV_HEADS, HEAD_DIM))
        rs_l.append(r_new.reshape(n_sbatch, N_RET_HEADS, HEAD_DIM, HEAD_DIM))
    y_prompt = x_all[:n_p].reshape(n_pbatch, t_prompt, d)
    y_sample = x_all[n_p:].reshape(n_sbatch, t_sample, d)
    return (y_prompt, y_sample, jnp.stack(kp_l), jnp.stack(vp_l), jnp.stack(rp_l),
            jnp.stack(ks_l), jnp.stack(vs_l), jnp.stack(rs_l))
```

```python
import functools

import numpy as np
import jax
import jax.numpy as jnp
from jax import lax
from jax.experimental import pallas as pl
from jax.experimental.pallas import tpu as pltpu

F32 = jnp.float32
BF16 = jnp.bfloat16
HIGHEST = lax.Precision.HIGHEST

HEAD_DIM = 64
N_ATTN_HEADS = 8
N_KV_HEADS = 2
GQA_GROUP = N_ATTN_HEADS // N_KV_HEADS
WINDOW = 128
ROPE_THETA = 500000.0
ROPE_DIMS = HEAD_DIM // 4
N_RET_HEADS = 8
RET_CHUNK = 128
RET_THETA = 10000.0
N_EXPERTS = 32
TOP_K = 4
SWIGLU_LIMIT = 7.0
SWIGLU_ALPHA = 1.702
NORM_EPS = 1e-5
PAST_LEN = 16384

LANES = 128
SUBLANES = 8
MXU_WIDTH = 256
VMEM_LIMIT_BYTES = 56 * 1024 * 1024

TOKEN_TILE = 256
EXPERT_TILE = 256
SAMPLE_BATCH_TILE = 8
DMA_UNROLL = 8

NEG_INF = -1e30


def _cparams(semantics):
    return pltpu.CompilerParams(dimension_semantics=semantics, vmem_limit_bytes=VMEM_LIMIT_BYTES)


def _lane_is_low_half(shape):
    return lax.broadcasted_iota(jnp.int32, shape, len(shape) - 1) < HEAD_DIM


def _dot(a, b, precision=None):
    return jnp.dot(a, b, preferred_element_type=F32, precision=precision)


def _dot_nt(a, b, precision=None):
    return lax.dot_general(a, b, (((1,), (1,)), ((), ())), preferred_element_type=F32, precision=precision)


def _dot_tn(a, b, precision=None):
    return lax.dot_general(a, b, (((0,), (0,)), ((), ())), preferred_element_type=F32, precision=precision)


def _rms_norm(x, w):
    return x * lax.rsqrt(jnp.mean(x * x, axis=-1, keepdims=True) + NORM_EPS) * w


def _silu(x):
    return x * jax.nn.sigmoid(x)


def _ada_body(c_ref, w_ref, b_ref, o_ref):
    a = _silu(c_ref[...])
    o_ref[0] = _dot(a, w_ref[0], HIGHEST) + b_ref[0]


def _ada_modulation(c_all, w_ada, b_ada):
    depth, d, cols = w_ada.shape
    rows = c_all.shape[0]
    tn = 1024
    return pl.pallas_call(
        _ada_body,
        grid=(depth, cols // tn),
        in_specs=[
            pl.BlockSpec((rows, d), lambda l, j: (0, 0)),
            pl.BlockSpec((1, d, tn), lambda l, j: (l, 0, j)),
            pl.BlockSpec((1, 1, tn), lambda l, j: (l, 0, j)),
        ],
        out_specs=pl.BlockSpec((1, rows, tn), lambda l, j: (l, 0, j)),
        out_shape=jax.ShapeDtypeStruct((depth, rows, cols), F32),
        compiler_params=_cparams(("arbitrary", "arbitrary")),
        name="ada_modulation",
    )(c_all, w_ada, b_ada.reshape(depth, 1, cols))


class _Geometry:
    def __init__(self, n_ptiles, tiles_per_batch, n_pbatch):
        self.n_ptiles = n_ptiles
        self.tiles_per_batch = tiles_per_batch
        self.n_pbatch = n_pbatch


def _tile_vec(i, geo, prompt_ref, sample_ref):
    b = jnp.minimum(i // geo.tiles_per_batch, geo.n_pbatch - 1)
    return jnp.where(i >= geo.n_ptiles, sample_ref[...], prompt_ref[pl.ds(b, 1), :])


def _tile_rows(i, geo, prompt_ref, sample_ref):
    return jnp.where(i >= geo.n_ptiles, sample_ref[...].astype(prompt_ref.dtype), prompt_ref[...])


def _prompt_vec_spec(layer, chunk, d):
    return pl.BlockSpec((None, SUBLANES, d), lambda i: (layer, 0, chunk))


def _sample_vec_spec(layer, chunk, geo, d):
    return pl.BlockSpec((None, TOKEN_TILE, d), lambda i: (layer, jnp.maximum(i - geo.n_ptiles, 0), chunk))


def _prompt_rows_spec(geo, cols):
    return pl.BlockSpec((TOKEN_TILE, cols), lambda i: (jnp.minimum(i, geo.n_ptiles - 1), 0))


def _sample_rows_spec(geo, cols):
    return pl.BlockSpec((TOKEN_TILE, cols), lambda i: (jnp.maximum(i - geo.n_ptiles, 0), 0))


def _rotate(xc, tab_ref, base, shift):
    return (xc * tab_ref[base] + pltpu.roll(xc, shift, 1) * tab_ref[base + 1]
            + pltpu.roll(xc, LANES - shift, 1) * tab_ref[base + 2])


def _inproj_body(geo, xp_ref, xs_ref, nw_ref, shp_ref, scp_ref, shs_ref, scs_ref, w_ref, tab_ref,
                 pp_ref, kvp_ref, ps_ref, kvs_ref):
    i = pl.program_id(0)
    x = _tile_rows(i, geo, xp_ref, xs_ref)
    h = _rms_norm(x, nw_ref[...]) * (1.0 + _tile_vec(i, geo, scp_ref, scs_ref)) + _tile_vec(i, geo, shp_ref, shs_ref)
    proj = _dot(h.astype(BF16), w_ref[...])
    q_cols = N_ATTN_HEADS * HEAD_DIM
    kv_cols = N_KV_HEADS * HEAD_DIM
    r_cols = N_RET_HEADS * HEAD_DIM
    o_ka = q_cols
    o_va = o_ka + kv_cols
    o_qr = o_va + kv_cols
    o_kr = o_qr + r_cols
    o_vr = o_kr + r_cols
    o_g = o_vr + r_cols
    attn_scale = HEAD_DIM ** -0.5
    ret_scale = HEAD_DIM ** -0.5
    half_a = ROPE_DIMS // 2
    half_r = HEAD_DIM // 2

    def emit(p_ref, kv_ref):
        dt = p_ref.dtype
        for c in range(q_cols // LANES):
            xc = proj[:, c * LANES:(c + 1) * LANES]
            p_ref[:, c * LANES:(c + 1) * LANES] = (_rotate(xc, tab_ref, 0, half_a) * attn_scale).astype(dt)
        kv_ref[:, 0:kv_cols] = _rotate(proj[:, o_ka:o_ka + kv_cols], tab_ref, 0, half_a)
        kv_ref[:, kv_cols:2 * kv_cols] = proj[:, o_va:o_va + kv_cols]
        for c in range(r_cols // LANES):
            xq = proj[:, o_qr + c * LANES:o_qr + (c + 1) * LANES]
            xk = proj[:, o_kr + c * LANES:o_kr + (c + 1) * LANES]
            p_ref[:, q_cols + c * LANES:q_cols + (c + 1) * LANES] = _rotate(xq, tab_ref, 3, half_r).astype(dt)
            p_ref[:, q_cols + r_cols + c * LANES:q_cols + r_cols + (c + 1) * LANES] = (
                _rotate(xk, tab_ref, 3, half_r) * ret_scale).astype(dt)
        p_ref[:, q_cols + 2 * r_cols:q_cols + 3 * r_cols] = proj[:, o_vr:o_vr + r_cols].astype(dt)
        p_ref[:, q_cols + 3 * r_cols:q_cols + 4 * r_cols] = proj[:, o_g:o_g + r_cols].astype(dt)

    @pl.when(i < geo.n_ptiles)
    def _():
        emit(pp_ref, kvp_ref)

    @pl.when(i >= geo.n_ptiles)
    def _():
        emit(ps_ref, kvs_ref)


def _input_projection(x_p, x_s, norm_w, mod_p, mod_s, layer, w_in_bf16, rope_tab, geo):
    n_p, d = x_p.shape
    n_s = x_s.shape[0]
    in_cols = w_in_bf16.shape[1]
    kv_cols = 2 * N_KV_HEADS * HEAD_DIM
    p_cols = in_cols - kv_cols
    tpb = geo.tiles_per_batch
    tab_index = lambda i: (0, jnp.where(i >= geo.n_ptiles, tpb, i % tpb), 0)
    return pl.pallas_call(
        functools.partial(_inproj_body, geo),
        grid=((n_p + n_s) // TOKEN_TILE,),
        in_specs=[
            _prompt_rows_spec(geo, d), _sample_rows_spec(geo, d),
            pl.BlockSpec((1, d), lambda i: (0, 0)),
            _prompt_vec_spec(layer, 0, d), _prompt_vec_spec(layer, 1, d),
            _sample_vec_spec(layer, 0, geo, d), _sample_vec_spec(layer, 1, geo, d),
            pl.BlockSpec((d, in_cols), lambda i: (0, 0)),
            pl.BlockSpec((6, TOKEN_TILE, LANES), tab_index),
        ],
        out_specs=[
            _prompt_rows_spec(geo, p_cols), _prompt_rows_spec(geo, kv_cols),
            _sample_rows_spec(geo, p_cols), _sample_rows_spec(geo, kv_cols),
        ],
        out_shape=[
            jax.ShapeDtypeStruct((n_p, p_cols), BF16), jax.ShapeDtypeStruct((n_p, kv_cols), F32),
            jax.ShapeDtypeStruct((n_s, p_cols), F32), jax.ShapeDtypeStruct((n_s, kv_cols), F32),
        ],
        compiler_params=_cparams(("arbitrary",)),
        name="input_projection",
    )(x_p, x_s, norm_w.reshape(1, d), mod_p, mod_p, mod_s, mod_s, w_in_bf16, rope_tab)


def _rope_tables(t_prompt, t_sample):
    pos = jnp.concatenate([jnp.arange(t_prompt, dtype=jnp.int32),
                           PAST_LEN + (jnp.arange(TOKEN_TILE, dtype=jnp.int32) % t_sample)])
    d = np.arange(LANES) % HEAD_DIM

    def tables(n_dims, theta):
        half = n_dims // 2
        freqs = jnp.power(jnp.float32(theta), -jnp.arange(half, dtype=jnp.float32) / half)
        ang = pos.astype(jnp.float32)[:, None] * freqs[None, :]
        cos, sin = jnp.cos(ang), jnp.sin(ang)
        fidx = np.where(d < n_dims, d % half, 0)
        cos_l = jnp.where(jnp.asarray(d < n_dims)[None, :], cos[:, fidx], 1.0)
        sin_l = sin[:, fidx]
        upper = jnp.asarray((d >= half) & (d < n_dims))[None, :]
        lower = jnp.asarray(d < half)[None, :]
        return [cos_l, jnp.where(upper, sin_l, 0.0), jnp.where(lower, -sin_l, 0.0)]

    return jnp.stack(tables(ROPE_DIMS, ROPE_THETA) + tables(HEAD_DIM, RET_THETA)).astype(F32)


def _retention_tables(c):
    h = N_RET_HEADS
    log_gamma = jnp.log(1.0 - jnp.power(2.0, -5.0 - jnp.arange(h, dtype=jnp.float32)))
    idx = jnp.arange(c, dtype=jnp.float32)
    diff = idx[:, None] - idx[None, :]
    decay_mask = jnp.where(diff >= 0, jnp.exp(log_gamma[:, None, None] * jnp.maximum(diff, 0.0)), 0.0)
    k_dec = jnp.exp(log_gamma[None, :] * (c - 1 - idx)[:, None])
    q_dec = jnp.exp(log_gamma[None, :] * (idx + 1.0)[:, None])
    chunk_decay = jnp.exp(log_gamma * c)
    rep = lambda a: jnp.repeat(a, HEAD_DIM, axis=-1)
    return decay_mask.astype(F32), rep(q_dec).astype(F32), rep(k_dec).astype(F32), rep(chunk_decay[None, :]).astype(F32)


def _group_norm_pair(o, avg):
    mu = _dot(o, avg, HIGHEST)
    dlt = o - mu
    var = _dot(dlt * dlt, avg, HIGHEST)
    return dlt * lax.rsqrt(var + NORM_EPS)


def _pair_average_matrix():
    r = np.arange(LANES)
    return jnp.asarray(((r[:, None] // HEAD_DIM) == (r[None, :] // HEAD_DIM)).astype(np.float32) / HEAD_DIM)


def _block_diag_mask():
    r = np.arange(LANES)
    return jnp.asarray(((r[:, None] // HEAD_DIM) == (r[None, :] // HEAD_DIM)).astype(np.float32))


def _prompt_mixer_body(sink_ref, p_ref, kvc_ref, kvp_ref, dmask_ref, qdec_ref, kdec_ref, cdec_ref,
                       bd_ref, avg_ref, gnw_ref, cat_ref, r_ref):
    jb = pl.program_id(1)
    blk = WINDOW
    q_cols = N_ATTN_HEADS * HEAD_DIM
    r_cols = N_RET_HEADS * HEAD_DIM
    kvw = N_KV_HEADS * HEAD_DIM

    @pl.when(jb == 0)
    def _():
        r_ref[...] = jnp.zeros_like(r_ref)

    low = _lane_is_low_half((blk, LANES))
    kband = jnp.concatenate([kvp_ref[:, 0:kvw], kvc_ref[:, 0:kvw]], axis=0)
    vband = jnp.concatenate([kvp_ref[:, kvw:2 * kvw], kvc_ref[:, kvw:2 * kvw]], axis=0)
    low2 = _lane_is_low_half((2 * blk, LANES))
    kswap = pltpu.roll(kband, HEAD_DIM, 1)
    vswap = pltpu.roll(vband, HEAD_DIM, 1)
    qi = lax.broadcasted_iota(jnp.int32, (blk, 2 * blk), 0)
    kj = lax.broadcasted_iota(jnp.int32, (blk, 2 * blk), 1)
    dist = blk + qi - kj
    mask = (dist >= 0) & (dist < WINDOW) & ((kj >= blk) | (jb > 0))
    for kvh in range(N_KV_HEADS):
        keep = low2 if kvh == 0 else jnp.logical_not(low2)
        k2 = jnp.where(keep, kband, kswap).astype(BF16)
        v2 = jnp.where(keep, vband, vswap).astype(BF16)
        pieces = []
        for pr in range(GQA_GROUP // 2):
            c0 = (kvh * (GQA_GROUP // 2) + pr) * LANES
            qp = p_ref[:, c0:c0 + LANES]
            pieces.append(jnp.where(low, qp, jnp.zeros_like(qp)))
            pieces.append(jnp.where(low, jnp.zeros_like(qp), qp))
        qs = jnp.concatenate(pieces, axis=0)
        s = _dot_nt(qs, k2)
        es, inv = [], []
        for hh in range(GQA_GROUP):
            sink = sink_ref[kvh * GQA_GROUP + hh]
            sh = jnp.where(mask, s[hh * blk:(hh + 1) * blk], NEG_INF)
            m = jnp.maximum(jnp.max(sh, axis=-1, keepdims=True), sink)
            e = jnp.exp(sh - m)
            den = jnp.sum(e, axis=-1, keepdims=True) + jnp.exp(sink - m)
            es.append(e.astype(BF16))
            inv.append(1.0 / den)
        o = _dot(jnp.concatenate(es, axis=0), v2)
        for pr in range(GQA_GROUP // 2):
            o_lo = o[(2 * pr) * blk:(2 * pr + 1) * blk] * inv[2 * pr]
            o_hi = o[(2 * pr + 1) * blk:(2 * pr + 2) * blk] * inv[2 * pr + 1]
            c0 = (kvh * (GQA_GROUP // 2) + pr) * LANES
            cat_ref[:, c0:c0 + LANES] = jnp.where(low, o_lo, o_hi).astype(BF16)
    for pp in range(N_RET_HEADS // 2):
        c0 = q_cols + pp * LANES
        qp = p_ref[:, c0:c0 + LANES]
        kp = p_ref[:, c0 + r_cols:c0 + r_cols + LANES]
        vp = p_ref[:, c0 + 2 * r_cols:c0 + 2 * r_cols + LANES]
        gp = p_ref[:, c0 + 3 * r_cols:c0 + 3 * r_cols + LANES].astype(F32)
        zero = jnp.zeros_like(qp)
        sc_lo = (_dot_nt(jnp.where(low, qp, zero), kp) * dmask_ref[2 * pp]).astype(BF16)
        sc_hi = (_dot_nt(jnp.where(low, zero, qp), kp) * dmask_ref[2 * pp + 1]).astype(BF16)
        o_inner = jnp.where(low, _dot(sc_lo, vp), _dot(sc_hi, vp))
        r_prev = r_ref[0, pp]
        qd = (qp.astype(F32) * qdec_ref[:, pp * LANES:(pp + 1) * LANES]).astype(BF16)
        o_cross = _dot(qd, r_prev.astype(BF16))
        kd = (kp.astype(F32) * kdec_ref[:, pp * LANES:(pp + 1) * LANES]).astype(BF16)
        kv_new = _dot_tn(kd, vp) * bd_ref[...]
        r_ref[0, pp] = cdec_ref[pp] * r_prev + kv_new
        y = _group_norm_pair(o_inner + o_cross, avg_ref[...])
        y = y * gnw_ref[:, pp * LANES:(pp + 1) * LANES] * _silu(gp)
        cat_ref[:, q_cols + pp * LANES:q_cols + (pp + 1) * LANES] = y.astype(BF16)


def _prompt_mixer(p_p, kv_p, sinks, gn_w, n_pbatch, t_prompt, ret_tabs):
    blk = WINDOW
    nb = t_prompt // blk
    p_cols = p_p.shape[1]
    kv_cols = kv_p.shape[1]
    d_mix = (N_ATTN_HEADS + N_RET_HEADS) * HEAD_DIM
    decay_mask, q_dec, k_dec, chunk_decay = ret_tabs
    n_pairs = N_RET_HEADS // 2
    bd = _block_diag_mask()
    cdec = jnp.stack([bd * chunk_decay[0, pp * LANES:(pp + 1) * LANES][:, None] for pp in range(n_pairs)])
    const2 = lambda shape: pl.BlockSpec(shape, lambda b, j: (0,) * len(shape))
    return pl.pallas_call(
        _prompt_mixer_body,
        grid=(n_pbatch, nb),
        in_specs=[
            pl.BlockSpec(memory_space=pltpu.SMEM),
            pl.BlockSpec((blk, p_cols), lambda b, j: (b * nb + j, 0)),
            pl.BlockSpec((blk, kv_cols), lambda b, j: (b * nb + j, 0)),
            pl.BlockSpec((blk, kv_cols), lambda b, j: (b * nb + jnp.maximum(j - 1, 0), 0)),
            const2((N_RET_HEADS, blk, blk)),
            const2((blk, N_RET_HEADS * HEAD_DIM)),
            const2((blk, N_RET_HEADS * HEAD_DIM)),
            const2((n_pairs, LANES, LANES)),
            const2((LANES, LANES)),
            const2((LANES, LANES)),
            const2((1, N_RET_HEADS * HEAD_DIM)),
        ],
        out_specs=[
            pl.BlockSpec((blk, d_mix), lambda b, j: (b * nb + j, 0)),
            pl.BlockSpec((1, n_pairs, LANES, LANES), lambda b, j: (b, 0, 0, 0)),
        ],
        out_shape=[
            jax.ShapeDtypeStruct((n_pbatch * t_prompt, d_mix), BF16),
            jax.ShapeDtypeStruct((n_pbatch, n_pairs, LANES, LANES), F32),
        ],
        compiler_params=_cparams(("arbitrary", "arbitrary")),
        name="prompt_mixer",
    )(sinks, p_p, kv_p, kv_p, decay_mask, q_dec, k_dec, cdec, bd, _pair_average_matrix(), gn_w.reshape(1, -1))


def _sample_mixer_body(t_new, sink_ref, p_ref, kv_ref, kbuf_ref, vbuf_ref, ret_ref, d4_ref, qdec_ref, kdec_ref,
                       cdec_ref, avg_ref, gnw_ref, cat_ref, knew_ref, vnew_ref, rnew_ref, ka_ref, va_ref, qs_ref):
    w = kbuf_ref.shape[1]
    q_cols = N_ATTN_HEADS * HEAD_DIM
    r_cols = N_RET_HEADS * HEAD_DIM
    kvw = N_KV_HEADS * HEAD_DIM
    n_keys = ka_ref.shape[0]
    rows_per_head = SUBLANES
    ka_ref[...] = jnp.zeros_like(ka_ref)
    va_ref[...] = jnp.zeros_like(va_ref)
    qs_ref[...] = jnp.zeros_like(qs_ref)
    low4 = _lane_is_low_half((t_new, LANES))
    lowk = _lane_is_low_half((n_keys, LANES))
    n_rows = GQA_GROUP * rows_per_head
    row = lax.broadcasted_iota(jnp.int32, (n_rows, n_keys), 0)
    key = lax.broadcasted_iota(jnp.int32, (n_rows, n_keys), 1)
    t_of_row = row % rows_per_head
    mask = (t_of_row < t_new) & (key > t_of_row) & (key <= t_of_row + WINDOW) & (key < w + t_new)
    head_of_row = lax.broadcasted_iota(jnp.int32, (n_rows, 1), 0) // rows_per_head

    def per_batch(b, carry):
        ka_ref[0:w, :] = kbuf_ref[b]
        va_ref[0:w, :] = vbuf_ref[b]
        ka_ref[w:w + t_new, :] = kv_ref[b][:, 0:kvw]
        va_ref[w:w + t_new, :] = kv_ref[b][:, kvw:2 * kvw]
        knew_ref[b] = ka_ref[t_new:t_new + w, :]
        vnew_ref[b] = va_ref[t_new:t_new + w, :]
        for h in range(N_ATTN_HEADS):
            c0 = (h // 2) * LANES
            qp = p_ref[b][:, c0:c0 + LANES]
            qs_ref[h * rows_per_head:h * rows_per_head + t_new, :] = jnp.where(
                low4 if h % 2 == 0 else jnp.logical_not(low4), qp, 0.0)
        kall = ka_ref[...]
        vall = va_ref[...]
        kswap = pltpu.roll(kall, HEAD_DIM, 1)
        vswap = pltpu.roll(vall, HEAD_DIM, 1)
        for kvh in range(N_KV_HEADS):
            keep = lowk if kvh == 0 else jnp.logical_not(lowk)
            k2 = jnp.where(keep, kall, kswap)
            v2 = jnp.where(keep, vall, vswap)
            qs = qs_ref[kvh * n_rows:(kvh + 1) * n_rows, :]
            s = jnp.where(mask, _dot_nt(qs, k2, HIGHEST), NEG_INF)
            sink = jnp.zeros((n_rows, 1), F32)
            for hh in range(GQA_GROUP):
                sink = jnp.where(head_of_row == hh, sink_ref[kvh * GQA_GROUP + hh], sink)
            m = jnp.maximum(jnp.max(s, axis=-1, keepdims=True), sink)
            e = jnp.exp(s - m)
            den = jnp.sum(e, axis=-1, keepdims=True) + jnp.exp(sink - m)
            o = _dot(e, v2, HIGHEST) / den
            for pr in range(GQA_GROUP // 2):
                o_lo = o[(2 * pr) * rows_per_head:(2 * pr) * rows_per_head + t_new]
                o_hi = o[(2 * pr + 1) * rows_per_head:(2 * pr + 1) * rows_per_head + t_new]
                c0 = (kvh * (GQA_GROUP // 2) + pr) * LANES
                cat_ref[b, :, c0:c0 + LANES] = jnp.where(low4, o_lo, o_hi)
        for pp in range(N_RET_HEADS // 2):
            c0 = q_cols + pp * LANES
            qp = p_ref[b][:, c0:c0 + LANES]
            kp = p_ref[b][:, c0 + r_cols:c0 + r_cols + LANES]
            vp = p_ref[b][:, c0 + 2 * r_cols:c0 + 2 * r_cols + LANES]
            gp = p_ref[b][:, c0 + 3 * r_cols:c0 + 3 * r_cols + LANES]
            r0 = ret_ref[b, pp]
            q_lo = jnp.where(low4, qp, 0.0)
            q_hi = jnp.where(low4, 0.0, qp)
            sc_lo = _dot_nt(q_lo, kp, HIGHEST) * d4_ref[2 * pp]
            sc_hi = _dot_nt(q_hi, kp, HIGHEST) * d4_ref[2 * pp + 1]
            o_inner = jnp.where(low4, _dot(sc_lo, vp, HIGHEST), _dot(sc_hi, vp, HIGHEST))
            qdec = qdec_ref[:, pp * LANES:(pp + 1) * LANES]
            oc_lo = _dot(q_lo * qdec, r0, HIGHEST)
            oc_hi = _dot(q_hi * qdec, r0, HIGHEST)
            o_cross = jnp.concatenate([oc_lo, oc_hi], axis=1)
            kd = kp * kdec_ref[:, pp * LANES:(pp + 1) * LANES]
            kv_full = _dot_tn(kd, vp, HIGHEST)
            kv_pair = jnp.concatenate([kv_full[0:HEAD_DIM, 0:HEAD_DIM], kv_full[HEAD_DIM:, HEAD_DIM:]], axis=0)
            rnew_ref[b, pp] = cdec_ref[pp] * r0 + kv_pair
            y = _group_norm_pair(o_inner + o_cross, avg_ref[...])
            y = y * gnw_ref[:, pp * LANES:(pp + 1) * LANES] * _silu(gp)
            cat_ref[b, :, q_cols + pp * LANES:q_cols + (pp + 1) * LANES] = y
        return carry

    lax.fori_loop(0, kbuf_ref.shape[0], per_batch, 0)


def _sample_mixer(p_s, kv_s, k_buf, v_buf, ret0, sinks, gn_w, ret_tabs):
    nb, t_new, p_cols = p_s.shape
    kv_cols = kv_s.shape[2]
    w = k_buf.shape[1]
    bb = SAMPLE_BATCH_TILE
    n_pairs = N_RET_HEADS // 2
    d_mix = (N_ATTN_HEADS + N_RET_HEADS) * HEAD_DIM
    decay_mask, q_dec, k_dec, chunk_decay = ret_tabs
    n_keys = ((w + t_new + SUBLANES - 1) // SUBLANES) * SUBLANES
    cdec = jnp.stack([jnp.broadcast_to(chunk_decay[0, pp * LANES:(pp + 1) * LANES][:, None], (LANES, HEAD_DIM))
                      for pp in range(n_pairs)])
    const1 = lambda shape: pl.BlockSpec(shape, lambda i: (0,) * len(shape))
    body = functools.partial(_sample_mixer_body, t_new)
    return pl.pallas_call(
        body,
        grid=(nb // bb,),
        in_specs=[
            pl.BlockSpec(memory_space=pltpu.SMEM),
            pl.BlockSpec((bb, t_new, p_cols), lambda i: (i, 0, 0)),
            pl.BlockSpec((bb, t_new, kv_cols), lambda i: (i, 0, 0)),
            pl.BlockSpec((bb, w, LANES), lambda i: (i, 0, 0)),
            pl.BlockSpec((bb, w, LANES), lambda i: (i, 0, 0)),
            pl.BlockSpec((bb, n_pairs, LANES, HEAD_DIM), lambda i: (i, 0, 0, 0)),
            const1((N_RET_HEADS, t_new, t_new)),
            const1((t_new, N_RET_HEADS * HEAD_DIM)),
            const1((t_new, N_RET_HEADS * HEAD_DIM)),
            const1((n_pairs, LANES, HEAD_DIM)),
            const1((LANES, LANES)),
            const1((1, N_RET_HEADS * HEAD_DIM)),
        ],
        out_specs=[
            pl.BlockSpec((bb, t_new, d_mix), lambda i: (i, 0, 0)),
            pl.BlockSpec((bb, w, LANES), lambda i: (i, 0, 0)),
            pl.BlockSpec((bb, w, LANES), lambda i: (i, 0, 0)),
            pl.BlockSpec((bb, n_pairs, LANES, HEAD_DIM), lambda i: (i, 0, 0, 0)),
        ],
        out_shape=[
            jax.ShapeDtypeStruct((nb, t_new, d_mix), F32),
            jax.ShapeDtypeStruct((nb, w, LANES), F32),
            jax.ShapeDtypeStruct((nb, w, LANES), F32),
            jax.ShapeDtypeStruct((nb, n_pairs, LANES, HEAD_DIM), F32),
        ],
        scratch_shapes=[
            pltpu.VMEM((n_keys, LANES), F32),
            pltpu.VMEM((n_keys, LANES), F32),
            pltpu.VMEM((N_ATTN_HEADS * SUBLANES, LANES), F32),
        ],
        compiler_params=_cparams(("arbitrary",)),
        name="sample_mixer",
    )(sinks, p_s, kv_s, k_buf, v_buf, ret0, decay_mask, q_dec, k_dec, cdec, _pair_average_matrix(), gn_w.reshape(1, -1))


def _router_body(geo, xp_ref, xs_ref, catp_ref, cats_ref, wout_ref, g1p_ref, g1s_ref, nw_ref, shp_ref, scp_ref,
                 shs_ref, scs_ref, wr_ref, br_ref, tri_ref, x1_ref, h2_ref, idx_ref, gate_ref, rank_ref, cnt_ref):
    i = pl.program_id(0)
    tv = functools.partial(_tile_vec, i, geo)
    cat = _tile_rows(i, geo, catp_ref, cats_ref)
    x1 = _tile_rows(i, geo, xp_ref, xs_ref) + tv(g1p_ref, g1s_ref) * _dot(cat, wout_ref[...])
    x1_ref[...] = x1
    h2 = _rms_norm(x1, nw_ref[...]) * (1.0 + tv(scp_ref, scs_ref)) + tv(shp_ref, shs_ref)
    h2_ref[...] = h2.reshape(h2_ref.shape)
    logits = _dot_nt(wr_ref[...], h2, HIGHEST) + br_ref[...]
    e_iota = lax.broadcasted_iota(jnp.int32, logits.shape, 0)
    vals, sels = [], []
    for _ in range(TOP_K):
        m = jnp.max(logits, axis=0, keepdims=True)
        sel = jnp.min(jnp.where(logits == m, e_iota, N_EXPERTS), axis=0, keepdims=True)
        vals.append(m)
        sels.append(sel)
        logits = jnp.where(e_iota == sel, -jnp.inf, logits)
    es = [jnp.exp(v - vals[0]) for v in vals]
    den = es[0]
    for e in es[1:]:
        den = den + e

    @pl.when(i == 0)
    def _():
        cnt_ref[...] = jnp.zeros_like(cnt_ref)

    base = cnt_ref[...]
    for k in range(TOP_K):
        hit = e_iota == sels[k]
        onehot = jnp.where(hit, 1.0, 0.0)
        before = _dot(onehot.astype(BF16), tri_ref[...])
        rank = jnp.sum(jnp.where(hit, base + before, 0.0), axis=0, keepdims=True)
        base = base + jnp.sum(onehot, axis=1, keepdims=True)
        idx_ref[k:k + 1, :] = sels[k]
        gate_ref[k:k + 1, :] = es[k] / den
        rank_ref[k:k + 1, :] = rank.astype(jnp.int32)
    cnt_ref[...] = base


def _outproj_router(x_p, x_s, cat_p, cat_s, w_out_bf16, norm_w, mod_p, mod_s, layer, w_router_t, b_router, geo):
    n_p, d = x_p.shape
    n = n_p + x_s.shape[0]
    d_mix = cat_p.shape[1]
    pv = lambda chunk: _prompt_vec_spec(layer, chunk, d)
    sv = lambda chunk: _sample_vec_spec(layer, chunk, geo, d)
    tile = lambda cols: pl.BlockSpec((TOKEN_TILE, cols), lambda i: (i, 0))
    choice = pl.BlockSpec((TOP_K, TOKEN_TILE), lambda i: (0, i))
    tri = jnp.asarray(np.triu(np.ones((TOKEN_TILE, TOKEN_TILE), np.float32), 1)).astype(BF16)
    return pl.pallas_call(
        functools.partial(_router_body, geo),
        grid=(n // TOKEN_TILE,),
        in_specs=[
            _prompt_rows_spec(geo, d), _sample_rows_spec(geo, d),
            _prompt_rows_spec(geo, d_mix), _sample_rows_spec(geo, d_mix),
            pl.BlockSpec((d_mix, d), lambda i: (0, 0)),
            pv(2), sv(2),
            pl.BlockSpec((1, d), lambda i: (0, 0)),
            pv(3), pv(4), sv(3), sv(4),
            pl.BlockSpec((N_EXPERTS, d), lambda i: (0, 0)),
            pl.BlockSpec((N_EXPERTS, 1), lambda i: (0, 0)),
            pl.BlockSpec((TOKEN_TILE, TOKEN_TILE), lambda i: (0, 0)),
        ],
        out_specs=[
            tile(d),
            pl.BlockSpec((TOKEN_TILE, 1, d), lambda i: (i, 0, 0)),
            choice, choice, choice,
            pl.BlockSpec((N_EXPERTS, 1), lambda i: (0, 0)),
        ],
        out_shape=[
            jax.ShapeDtypeStruct((n, d), F32),
            jax.ShapeDtypeStruct((n, 1, d), F32),
            jax.ShapeDtypeStruct((TOP_K, n), jnp.int32),
            jax.ShapeDtypeStruct((TOP_K, n), F32),
            jax.ShapeDtypeStruct((TOP_K, n), jnp.int32),
            jax.ShapeDtypeStruct((N_EXPERTS, 1), F32),
        ],
        compiler_params=_cparams(("arbitrary",)),
        name="outproj_router",
    )(x_p, x_s, cat_p, cat_s, w_out_bf16, mod_p, mod_s, norm_w.reshape(1, d), mod_p, mod_p, mod_s, mod_s,
      w_router_t, b_router.reshape(N_EXPERTS, 1), tri)


def _routing_plan(top_idx, rank, counts_f, tm):
    k, n = top_idx.shape
    n_pairs = k * n
    n_tiles = (n_pairs + N_EXPERTS * (tm - 1)) // tm + 1
    counts = counts_f[:, 0].astype(jnp.int32)
    tiles_per = (counts + tm - 1) // tm
    tile_end = jnp.cumsum(tiles_per)
    tile_start = tile_end - tiles_per
    group_start = jnp.cumsum(counts) - counts
    n_used = tile_end[-1]
    slot_of_pair = jnp.take(tile_start, top_idx) * tm + rank
    token_of_pair = jnp.broadcast_to(jnp.arange(n, dtype=jnp.int32)[None, :], (k, n))
    _, token_sorted = lax.sort((slot_of_pair.reshape(-1), token_of_pair.reshape(-1)), num_keys=1)
    tile_ids = jnp.arange(n_tiles, dtype=jnp.int32)
    tile_expert = jnp.minimum(jnp.sum(tile_ids[:, None] >= tile_end[None, :], axis=1), N_EXPERTS - 1).astype(jnp.int32)
    slot = jnp.arange(n_tiles * tm, dtype=jnp.int32)
    e_slot = jnp.repeat(tile_expert, tm)
    r_slot = slot - jnp.take(tile_start, e_slot) * tm
    live = (slot < n_used * tm) & (r_slot < jnp.take(counts, e_slot))
    pos = jnp.clip(jnp.take(group_start, e_slot) + r_slot, 0, n_pairs - 1)
    slot_rows = jnp.where(live, jnp.take(token_sorted, pos), 0)
    last_expert = tile_expert[jnp.maximum(n_used - 1, 0)]
    tile_expert = jnp.where(tile_ids < n_used, tile_expert, last_expert)
    return tile_expert, n_used.reshape(1).astype(jnp.int32), slot_rows.reshape(n_tiles, 1, tm), slot_of_pair


def _issue_row_copies(src_hbm, row_of, dst_row, sem, n_rows):
    def group(g, c):
        for u in range(DMA_UNROLL):
            r = g * DMA_UNROLL + u
            pltpu.make_async_copy(src_hbm.at[row_of(r)], dst_row(r), sem).start(priority=u % 2)
        return c
    lax.fori_loop(0, n_rows // DMA_UNROLL, group, 0)


def _wait_row_copies(src_hbm, dst_buf, sem):
    pltpu.make_async_copy(src_hbm.at[pl.ds(0, dst_buf.shape[0])], dst_buf, sem).wait()


def _expert_body(te_ref, nt_ref, rows_cur_ref, rows_nxt_ref, h_hbm, wgu_ref, bg_ref, bu_ref, wd_ref, bd_ref, perm_ref,
                 y_ref, xbuf, sems, wgu_s, act_s, x_s):
    j = pl.program_id(0)
    n_used = nt_ref[0]
    slot = j % 2
    tm = EXPERT_TILE
    f2 = wgu_ref.shape[2]
    half = MXU_WIDTH // 2

    def issue(rows_ref, sl):
        _issue_row_copies(h_hbm, lambda r: rows_ref[0, 0, r], lambda r: xbuf.at[sl, r], sems.at[sl], tm)

    @pl.when(jnp.logical_and(j == 0, n_used > 0))
    def _():
        issue(rows_cur_ref, 0)

    @pl.when(j + 1 < n_used)
    def _():
        issue(rows_nxt_ref, 1 - slot)

    new_expert = jnp.logical_or(j == 0, te_ref[j] != te_ref[jnp.maximum(j - 1, 0)])

    @pl.when(jnp.logical_and(j < n_used, new_expert))
    def _():
        for c in range(f2 // MXU_WIDTH):
            cols = slice(c * MXU_WIDTH, (c + 1) * MXU_WIDTH)
            wgu_s[:, cols] = _dot(wgu_ref[0, :, cols], perm_ref[...]).astype(BF16)

    @pl.when(j < n_used)
    def _():
        _wait_row_copies(h_hbm, xbuf.at[slot], sems.at[slot])
        x_s[...] = xbuf[slot][:, 0, :]
        x = x_s[...].astype(BF16)
        for c in range(f2 // MXU_WIDTH):
            gu = _dot(x, wgu_s[:, c * MXU_WIDTH:(c + 1) * MXU_WIDTH])
            glu = jnp.minimum(gu[:, :half] + bg_ref[0, :, c * half:(c + 1) * half], SWIGLU_LIMIT)
            lin = jnp.clip(gu[:, half:] + bu_ref[0, :, c * half:(c + 1) * half], -SWIGLU_LIMIT, SWIGLU_LIMIT)
            act_s[:, c * half:(c + 1) * half] = (glu * jax.nn.sigmoid(SWIGLU_ALPHA * glu) * (lin + 1.0)).astype(BF16)
        y = _dot(act_s[...], wd_ref[0]) + bd_ref[0]
        y_ref[...] = y.reshape(y_ref.shape)

    @pl.when(j >= n_used)
    def _():
        y_ref[...] = jnp.zeros_like(y_ref)


def _deinterleave_matrix():
    m = np.zeros((MXU_WIDTH, MXU_WIDTH), np.float32)
    j = np.arange(MXU_WIDTH // 2)
    m[2 * j, j] = 1.0
    m[2 * j + 1, MXU_WIDTH // 2 + j] = 1.0
    return jnp.asarray(m).astype(BF16)


def _routed_experts(h2_rows, tile_expert, n_used, slot_rows, wgu, bg, bu, wd, bd):
    n, _, d = h2_rows.shape
    f2 = wgu.shape[2]
    f = f2 // 2
    tm = EXPERT_TILE
    n_tiles = slot_rows.shape[0]
    by_expert = lambda shape: pl.BlockSpec(shape, lambda j, te, nt: (te[j], 0, 0))
    grid_spec = pltpu.PrefetchScalarGridSpec(
        num_scalar_prefetch=2,
        grid=(n_tiles,),
        in_specs=[
            pl.BlockSpec((1, 1, tm), lambda j, te, nt: (j, 0, 0), memory_space=pltpu.SMEM),
            pl.BlockSpec((1, 1, tm), lambda j, te, nt: (jnp.minimum(j + 1, n_tiles - 1), 0, 0), memory_space=pltpu.SMEM),
            pl.BlockSpec(memory_space=pl.ANY),
            by_expert((1, d, f2)),
            by_expert((1, 1, f)), by_expert((1, 1, f)),
            by_expert((1, f, d)),
            by_expert((1, 1, d)),
            pl.BlockSpec((MXU_WIDTH, MXU_WIDTH), lambda j, te, nt: (0, 0)),
        ],
        out_specs=pl.BlockSpec((tm, 1, d), lambda j, te, nt: (j, 0, 0)),
        scratch_shapes=[
            pltpu.VMEM((2, tm, 1, d), F32),
            pltpu.SemaphoreType.DMA((2,)),
            pltpu.VMEM((d, f2), BF16),
            pltpu.VMEM((tm, f), BF16),
            pltpu.VMEM((tm, d), F32),
        ],
    )
    return pl.pallas_call(
        _expert_body,
        grid_spec=grid_spec,
        out_shape=jax.ShapeDtypeStruct((n_tiles * tm, 1, d), F32),
        compiler_params=_cparams(("arbitrary",)),
        name="routed_experts",
    )(tile_expert, n_used, slot_rows, slot_rows, h2_rows, wgu, bg, bu, wd, bd, _deinterleave_matrix())


def _combine_body(geo, final_norm, slots_cur_ref, slots_nxt_ref, x1_ref, gate_ref, g2p_ref, g2s_ref, nfw_ref, y_hbm,
                  outp_ref, outs_ref, ybuf, sems):
    i = pl.program_id(0)
    n_steps = pl.num_programs(0)
    slot = i % 2
    tm = TOKEN_TILE

    def issue(rows_ref, sl):
        for k in range(TOP_K):
            _issue_row_copies(y_hbm, lambda r, k=k: rows_ref[0, k, r], lambda r, k=k: ybuf.at[sl, k, r], sems.at[sl], tm)

    @pl.when(i == 0)
    def _():
        issue(slots_cur_ref, 0)

    @pl.when(i + 1 < n_steps)
    def _():
        issue(slots_nxt_ref, 1 - slot)

    for k in range(TOP_K):
        _wait_row_copies(y_hbm, ybuf.at[slot, k], sems.at[slot])
    acc = ybuf[slot, 0][:, 0, :] * gate_ref[:, 0:1]
    for k in range(1, TOP_K):
        acc = acc + ybuf[slot, k][:, 0, :] * gate_ref[:, k:k + 1]
    x2 = x1_ref[...] + _tile_vec(i, geo, g2p_ref, g2s_ref) * acc
    if final_norm:
        x2 = _rms_norm(x2, nfw_ref[...])

    @pl.when(i < geo.n_ptiles)
    def _():
        outp_ref[...] = x2

    @pl.when(i >= geo.n_ptiles)
    def _():
        outs_ref[...] = x2


def _combine(x1, y_rows, slot_of_pair, gates_t, mod_p, mod_s, layer, final_w, geo, final_norm):
    n, d = x1.shape
    tm = TOKEN_TILE
    n_steps = n // tm
    n_p = geo.n_ptiles * tm
    slots = slot_of_pair.reshape(TOP_K, n_steps, tm).transpose(1, 0, 2)
    return pl.pallas_call(
        functools.partial(_combine_body, geo, final_norm),
        grid=(n_steps,),
        in_specs=[
            pl.BlockSpec((1, TOP_K, tm), lambda i: (i, 0, 0), memory_space=pltpu.SMEM),
            pl.BlockSpec((1, TOP_K, tm), lambda i: (jnp.minimum(i + 1, n_steps - 1), 0, 0), memory_space=pltpu.SMEM),
            pl.BlockSpec((tm, d), lambda i: (i, 0)),
            pl.BlockSpec((tm, TOP_K), lambda i: (i, 0)),
            _prompt_vec_spec(layer, 5, d),
            _sample_vec_spec(layer, 5, geo, d),
            pl.BlockSpec((1, d), lambda i: (0, 0)),
            pl.BlockSpec(memory_space=pl.ANY),
        ],
        out_specs=[_prompt_rows_spec(geo, d), _sample_rows_spec(geo, d)],
        out_shape=[jax.ShapeDtypeStruct((n_p, d), F32), jax.ShapeDtypeStruct((n - n_p, d), F32)],
        scratch_shapes=[pltpu.VMEM((2, TOP_K, tm, 1, d), F32), pltpu.SemaphoreType.DMA((2,))],
        compiler_params=_cparams(("arbitrary",)),
        name="moe_combine",
    )(slots, slots, x1, gates_t, mod_p, mod_s, final_w.reshape(1, d), y_rows)


def kernel(x_prompt, x_sample, state_swa_k, state_swa_v, state_ret, c_prompt, c_sample, norm_mix_w, w_ada, b_ada, w_in, attn_sinks, ret_gn_w, w_out, norm_ffn_w, w_router, b_router, w_gate_up, b_gate_up, w_down, b_down, norm_final_w):
    n_pbatch, t_prompt, d = x_prompt.shape
    n_sbatch, t_sample, _ = x_sample.shape
    depth = w_in.shape[0]
    n_p = n_pbatch * t_prompt
    n_s = n_sbatch * t_sample
    win = state_swa_k.shape[2]
    assert d == (N_ATTN_HEADS + N_RET_HEADS) * HEAD_DIM and w_gate_up.shape[1] == N_EXPERTS
    assert t_prompt % TOKEN_TILE == 0 and n_s % TOKEN_TILE == 0 and TOKEN_TILE % t_sample == 0
    assert t_prompt % WINDOW == 0 and RET_CHUNK == WINDOW and win == WINDOW and t_sample <= SUBLANES
    assert n_sbatch % SAMPLE_BATCH_TILE == 0 and n_pbatch <= SUBLANES
    assert w_gate_up.shape[3] % MXU_WIDTH == 0 and EXPERT_TILE % DMA_UNROLL == 0 and TOKEN_TILE % DMA_UNROLL == 0
    geo = _Geometry(n_p // TOKEN_TILE, t_prompt // TOKEN_TILE, n_pbatch)

    c_all = jnp.concatenate([c_prompt, jnp.zeros((SUBLANES - n_pbatch, d), F32), c_sample], axis=0)
    mod = _ada_modulation(c_all, w_ada, b_ada)
    mod_p = mod[:, :SUBLANES]
    mod_s = jnp.repeat(mod[:, SUBLANES:], t_sample, axis=1)

    rope_tab = _rope_tables(t_prompt, t_sample)
    ret_tabs_p = _retention_tables(RET_CHUNK)
    ret_tabs_s = _retention_tables(t_sample)
    f = w_gate_up.shape[3] // 2

    x_p = x_prompt.reshape(n_p, d)
    x_s = x_sample.reshape(n_s, d)
    kp_l, vp_l, rp_l, ks_l, vs_l, rs_l = [], [], [], [], [], []
    for l in range(depth):
        p_p, kv_p, p_s, kv_s = _input_projection(x_p, x_s, norm_mix_w[l], mod_p, mod_s, l, w_in[l].astype(BF16),
                                                 rope_tab, geo)
        cat_p, r_pairs = _prompt_mixer(p_p, kv_p, attn_sinks[l], ret_gn_w[l], n_pbatch, t_prompt, ret_tabs_p)
        cat_s, k_new, v_new, r_new = _sample_mixer(
            p_s.reshape(n_sbatch, t_sample, -1), kv_s.reshape(n_sbatch, t_sample, -1),
            state_swa_k[l].reshape(n_sbatch, win, LANES), state_swa_v[l].reshape(n_sbatch, win, LANES),
            state_ret[l].reshape(n_sbatch, N_RET_HEADS // 2, LANES, HEAD_DIM), attn_sinks[l], ret_gn_w[l], ret_tabs_s)
        x1, h2_rows, top_idx, gates, rank, counts = _outproj_router(
            x_p, x_s, cat_p, cat_s.reshape(n_s, d), w_out[l].astype(BF16), norm_ffn_w[l], mod_p, mod_s, l,
            w_router[l].T, b_router[l], geo)
        tile_expert, n_used, slot_rows, slot_of_pair = _routing_plan(top_idx, rank, counts, EXPERT_TILE)
        b_gu = b_gate_up[l].reshape(N_EXPERTS, 1, f, 2)
        y_rows = _routed_experts(h2_rows, tile_expert, n_used, slot_rows, w_gate_up[l].astype(BF16),
                                 b_gu[..., 0], b_gu[..., 1], w_down[l].astype(BF16), b_down[l][:, None, :])
        x_p, x_s = _combine(x1, y_rows, slot_of_pair, gates.T, mod_p, mod_s, l, norm_final_w, geo,
                            final_norm=(l == depth - 1))

        kv5 = kv_p.reshape(n_pbatch, t_prompt, 2, N_KV_HEADS, HEAD_DIM)
        kp_l.append(kv5[:, t_prompt - WINDOW:, 0])
        vp_l.append(kv5[:, t_prompt - WINDOW:, 1])
        rp_l.append(jnp.stack([r_pairs[:, :, :HEAD_DIM, :HEAD_DIM], r_pairs[:, :, HEAD_DIM:, HEAD_DIM:]], axis=2)
                    .reshape(n_pbatch, N_RET_HEADS, HEAD_DIM, HEAD_DIM))
        ks_l.append(k_new.reshape(n_sbatch, win, N_KV_HEADS, HEAD_DIM))
        vs_l.append(v_new.reshape(n_sbatch, win, N_KV_HEADS, HEAD_DIM))
        rs_l.append(r_new.reshape(n_sbatch, N_RET_HEADS, HEAD_DIM, HEAD_DIM))
    return (x_p.reshape(n_pbatch, t_prompt, d), x_s.reshape(n_sbatch, t_sample, d), jnp.stack(kp_l), jnp.stack(vp_l),
            jnp.stack(rp_l), jnp.stack(ks_l), jnp.stack(vs_l), jnp.stack(rs_l))
```

```python
import functools

import numpy as np
import jax
import jax.numpy as jnp
from jax import lax
from jax.experimental import pallas as pl
from jax.experimental.pallas import tpu as pltpu

F32 = jnp.float32
BF16 = jnp.bfloat16
HIGHEST = lax.Precision.HIGHEST

HEAD_DIM = 64
N_ATTN_HEADS = 8
N_KV_HEADS = 2
GQA_GROUP = N_ATTN_HEADS // N_KV_HEADS
WINDOW = 128
ROPE_THETA = 500000.0
ROPE_DIMS = HEAD_DIM // 4
N_RET_HEADS = 8
RET_CHUNK = 128
RET_THETA = 10000.0
N_EXPERTS = 32
TOP_K = 4
SWIGLU_LIMIT = 7.0
SWIGLU_ALPHA = 1.702
NORM_EPS = 1e-5
PAST_LEN = 16384

LANES = 128
SUBLANES = 8
MXU_WIDTH = 256
VMEM_LIMIT_BYTES = 56 * 1024 * 1024

TOKEN_TILE = 256
EXPERT_TILE = 256
SAMPLE_BATCH_TILE = 8
DMA_UNROLL = 8

NEG_INF = -1e30


def _cparams(semantics):
    return pltpu.CompilerParams(dimension_semantics=semantics, vmem_limit_bytes=VMEM_LIMIT_BYTES)


def _lane_is_low_half(shape):
    return lax.broadcasted_iota(jnp.int32, shape, len(shape) - 1) < HEAD_DIM


def _dot(a, b, precision=None):
    return jnp.dot(a, b, preferred_element_type=F32, precision=precision)


def _dot_nt(a, b, precision=None):
    return lax.dot_general(a, b, (((1,), (1,)), ((), ())), preferred_element_type=F32, precision=precision)


def _dot_tn(a, b, precision=None):
    return lax.dot_general(a, b, (((0,), (0,)), ((), ())), preferred_element_type=F32, precision=precision)


def _rms_norm(x, w):
    return x * lax.rsqrt(jnp.mean(x * x, axis=-1, keepdims=True) + NORM_EPS) * w


def _silu(x):
    return x * jax.nn.sigmoid(x)


def _ada_body(c_ref, w_ref, b_ref, o_ref):
    a = _silu(c_ref[...])
    o_ref[0] = _dot(a, w_ref[0], HIGHEST) + b_ref[0]


def _ada_modulation(c_all, w_ada, b_ada):
    depth, d, cols = w_ada.shape
    rows = c_all.shape[0]
    tn = 1024
    return pl.pallas_call(
        _ada_body,
        grid=(depth, cols // tn),
        in_specs=[
            pl.BlockSpec((rows, d), lambda l, j: (0, 0)),
            pl.BlockSpec((1, d, tn), lambda l, j: (l, 0, j)),
            pl.BlockSpec((1, 1, tn), lambda l, j: (l, 0, j)),
        ],
        out_specs=pl.BlockSpec((1, rows, tn), lambda l, j: (l, 0, j)),
        out_shape=jax.ShapeDtypeStruct((depth, rows, cols), F32),
        compiler_params=_cparams(("arbitrary", "arbitrary")),
        name="ada_modulation",
    )(c_all, w_ada, b_ada.reshape(depth, 1, cols))


class _Geometry:
    def __init__(self, n_ptiles, tiles_per_batch, n_pbatch):
        self.n_ptiles = n_ptiles
        self.tiles_per_batch = tiles_per_batch
        self.n_pbatch = n_pbatch


def _tile_vec(i, geo, prompt_ref, sample_ref):
    b = jnp.minimum(i // geo.tiles_per_batch, geo.n_pbatch - 1)
    return jnp.where(i >= geo.n_ptiles, sample_ref[...], prompt_ref[pl.ds(b, 1), :])


def _tile_rows(i, geo, prompt_ref, sample_ref):
    return jnp.where(i >= geo.n_ptiles, sample_ref[...].astype(prompt_ref.dtype), prompt_ref[...])


def _prompt_vec_spec(layer, chunk, d):
    return pl.BlockSpec((None, SUBLANES, d), lambda i: (layer, 0, chunk))


def _sample_vec_spec(layer, chunk, geo, d):
    return pl.BlockSpec((None, TOKEN_TILE, d), lambda i: (layer, jnp.maximum(i - geo.n_ptiles, 0), chunk))


def _prompt_rows_spec(geo, cols):
    return pl.BlockSpec((TOKEN_TILE, cols), lambda i: (jnp.minimum(i, geo.n_ptiles - 1), 0))


def _sample_rows_spec(geo, cols):
    return pl.BlockSpec((TOKEN_TILE, cols), lambda i: (jnp.maximum(i - geo.n_ptiles, 0), 0))


def _rotate(xc, tab_ref, base, shift):
    return (xc * tab_ref[base] + pltpu.roll(xc, shift, 1) * tab_ref[base + 1]
            + pltpu.roll(xc, LANES - shift, 1) * tab_ref[base + 2])


def _inproj_body(geo, xp_ref, xs_ref, nw_ref, shp_ref, scp_ref, shs_ref, scs_ref, w_ref, tab_ref, p_ref, kv_ref):
    i = pl.program_id(0)
    x = _tile_rows(i, geo, xp_ref, xs_ref)
    h = _rms_norm(x, nw_ref[...]) * (1.0 + _tile_vec(i, geo, scp_ref, scs_ref)) + _tile_vec(i, geo, shp_ref, shs_ref)
    proj = _dot(h.astype(BF16), w_ref[...])
    q_cols = N_ATTN_HEADS * HEAD_DIM
    kv_cols = N_KV_HEADS * HEAD_DIM
    r_cols = N_RET_HEADS * HEAD_DIM
    o_ka = q_cols
    o_va = o_ka + kv_cols
    o_qr = o_va + kv_cols
    o_kr = o_qr + r_cols
    o_vr = o_kr + r_cols
    o_g = o_vr + r_cols
    attn_scale = HEAD_DIM ** -0.5
    ret_scale = HEAD_DIM ** -0.5
    half_a = ROPE_DIMS // 2
    half_r = HEAD_DIM // 2

    def emit(p_ref, kv_ref):
        dt = p_ref.dtype
        for c in range(q_cols // LANES):
            xc = proj[:, c * LANES:(c + 1) * LANES]
            p_ref[:, c * LANES:(c + 1) * LANES] = (_rotate(xc, tab_ref, 0, half_a) * attn_scale).astype(dt)
        kv_ref[:, 0:kv_cols] = _rotate(proj[:, o_ka:o_ka + kv_cols], tab_ref, 0, half_a)
        kv_ref[:, kv_cols:2 * kv_cols] = proj[:, o_va:o_va + kv_cols]
        for c in range(r_cols // LANES):
            xq = proj[:, o_qr + c * LANES:o_qr + (c + 1) * LANES]
            xk = proj[:, o_kr + c * LANES:o_kr + (c + 1) * LANES]
            p_ref[:, q_cols + c * LANES:q_cols + (c + 1) * LANES] = _rotate(xq, tab_ref, 3, half_r).astype(dt)
            p_ref[:, q_cols + r_cols + c * LANES:q_cols + r_cols + (c + 1) * LANES] = (
                _rotate(xk, tab_ref, 3, half_r) * ret_scale).astype(dt)
        p_ref[:, q_cols + 2 * r_cols:q_cols + 3 * r_cols] = proj[:, o_vr:o_vr + r_cols].astype(dt)
        p_ref[:, q_cols + 3 * r_cols:q_cols + 4 * r_cols] = proj[:, o_g:o_g + r_cols].astype(dt)

    emit(p_ref, kv_ref)


def _input_projection(x_p, x_s, norm_w, mod_p, mod_s, layer, w_in_bf16, rope_tab, geo):
    n_p, d = x_p.shape
    n_s = x_s.shape[0]
    in_cols = w_in_bf16.shape[1]
    kv_cols = 2 * N_KV_HEADS * HEAD_DIM
    p_cols = in_cols - kv_cols
    tpb = geo.tiles_per_batch
    tab_index = lambda i: (0, jnp.where(i >= geo.n_ptiles, tpb, i % tpb), 0)
    return pl.pallas_call(
        functools.partial(_inproj_body, geo),
        grid=((n_p + n_s) // TOKEN_TILE,),
        in_specs=[
            _prompt_rows_spec(geo, d), _sample_rows_spec(geo, d),
            pl.BlockSpec((1, d), lambda i: (0, 0)),
            _prompt_vec_spec(layer, 0, d), _prompt_vec_spec(layer, 1, d),
            _sample_vec_spec(layer, 0, geo, d), _sample_vec_spec(layer, 1, geo, d),
            pl.BlockSpec((d, in_cols), lambda i: (0, 0)),
            pl.BlockSpec((6, TOKEN_TILE, LANES), tab_index),
        ],
        out_specs=[
            pl.BlockSpec((TOKEN_TILE, p_cols), lambda i: (i, 0)),
            pl.BlockSpec((TOKEN_TILE, kv_cols), lambda i: (i, 0)),
        ],
        out_shape=[
            jax.ShapeDtypeStruct((n_p + n_s, p_cols), BF16),
            jax.ShapeDtypeStruct((n_p + n_s, kv_cols), F32),
        ],
        compiler_params=_cparams(("arbitrary",)),
        name="input_projection",
    )(x_p, x_s, norm_w.reshape(1, d), mod_p, mod_p, mod_s, mod_s, w_in_bf16, rope_tab)


def _rope_tables(t_prompt, t_sample):
    pos = jnp.concatenate([jnp.arange(t_prompt, dtype=jnp.int32),
                           PAST_LEN + (jnp.arange(TOKEN_TILE, dtype=jnp.int32) % t_sample)])
    d = np.arange(LANES) % HEAD_DIM

    def tables(n_dims, theta):
        half = n_dims // 2
        freqs = jnp.power(jnp.float32(theta), -jnp.arange(half, dtype=jnp.float32) / half)
        ang = pos.astype(jnp.float32)[:, None] * freqs[None, :]
        cos, sin = jnp.cos(ang), jnp.sin(ang)
        fidx = np.where(d < n_dims, d % half, 0)
        cos_l = jnp.where(jnp.asarray(d < n_dims)[None, :], cos[:, fidx], 1.0)
        sin_l = sin[:, fidx]
        upper = jnp.asarray((d >= half) & (d < n_dims))[None, :]
        lower = jnp.asarray(d < half)[None, :]
        return [cos_l, jnp.where(upper, sin_l, 0.0), jnp.where(lower, -sin_l, 0.0)]

    return jnp.stack(tables(ROPE_DIMS, ROPE_THETA) + tables(HEAD_DIM, RET_THETA)).astype(F32)


def _retention_tables(c):
    h = N_RET_HEADS
    log_gamma = jnp.log(1.0 - jnp.power(2.0, -5.0 - jnp.arange(h, dtype=jnp.float32)))
    idx = jnp.arange(c, dtype=jnp.float32)
    diff = idx[:, None] - idx[None, :]
    decay_mask = jnp.where(diff >= 0, jnp.exp(log_gamma[:, None, None] * jnp.maximum(diff, 0.0)), 0.0)
    k_dec = jnp.exp(log_gamma[None, :] * (c - 1 - idx)[:, None])
    q_dec = jnp.exp(log_gamma[None, :] * (idx + 1.0)[:, None])
    chunk_decay = jnp.exp(log_gamma * c)
    rep = lambda a: jnp.repeat(a, HEAD_DIM, axis=-1)
    return decay_mask.astype(F32), rep(q_dec).astype(F32), rep(k_dec).astype(F32), rep(chunk_decay[None, :]).astype(F32)


def _group_norm_pair(o, avg):
    mu = _dot(o, avg, HIGHEST)
    dlt = o - mu
    var = _dot(dlt * dlt, avg, HIGHEST)
    return dlt * lax.rsqrt(var + NORM_EPS)


def _pair_average_matrix():
    r = np.arange(LANES)
    return jnp.asarray(((r[:, None] // HEAD_DIM) == (r[None, :] // HEAD_DIM)).astype(np.float32) / HEAD_DIM)


def _block_diag_mask():
    r = np.arange(LANES)
    return jnp.asarray(((r[:, None] // HEAD_DIM) == (r[None, :] // HEAD_DIM)).astype(np.float32))


def _prompt_mixer_body(sink_ref, p_ref, kvc_ref, kvp_ref, dmask_ref, qdec_ref, kdec_ref, cdec_ref,
                       bd_ref, avg_ref, gnw_ref, cat_ref, r_ref):
    jb = pl.program_id(1)
    blk = WINDOW
    q_cols = N_ATTN_HEADS * HEAD_DIM
    r_cols = N_RET_HEADS * HEAD_DIM
    kvw = N_KV_HEADS * HEAD_DIM

    @pl.when(jb == 0)
    def _():
        r_ref[...] = jnp.zeros_like(r_ref)

    low = _lane_is_low_half((blk, LANES))
    kband = jnp.concatenate([kvp_ref[:, 0:kvw], kvc_ref[:, 0:kvw]], axis=0)
    vband = jnp.concatenate([kvp_ref[:, kvw:2 * kvw], kvc_ref[:, kvw:2 * kvw]], axis=0)
    low2 = _lane_is_low_half((2 * blk, LANES))
    kswap = pltpu.roll(kband, HEAD_DIM, 1)
    vswap = pltpu.roll(vband, HEAD_DIM, 1)
    qi = lax.broadcasted_iota(jnp.int32, (blk, 2 * blk), 0)
    kj = lax.broadcasted_iota(jnp.int32, (blk, 2 * blk), 1)
    dist = blk + qi - kj
    mask = (dist >= 0) & (dist < WINDOW) & ((kj >= blk) | (jb > 0))
    for kvh in range(N_KV_HEADS):
        keep = low2 if kvh == 0 else jnp.logical_not(low2)
        k2 = jnp.where(keep, kband, kswap).astype(BF16)
        v2 = jnp.where(keep, vband, vswap).astype(BF16)
        pieces = []
        for pr in range(GQA_GROUP // 2):
            c0 = (kvh * (GQA_GROUP // 2) + pr) * LANES
            qp = p_ref[:, c0:c0 + LANES]
            pieces.append(jnp.where(low, qp, jnp.zeros_like(qp)))
            pieces.append(jnp.where(low, jnp.zeros_like(qp), qp))
        qs = jnp.concatenate(pieces, axis=0)
        s = _dot_nt(qs, k2)
        es, inv = [], []
        for hh in range(GQA_GROUP):
            sink = sink_ref[kvh * GQA_GROUP + hh]
            sh = jnp.where(mask, s[hh * blk:(hh + 1) * blk], NEG_INF)
            m = jnp.maximum(jnp.max(sh, axis=-1, keepdims=True), sink)
            e = jnp.exp(sh - m)
            den = jnp.sum(e, axis=-1, keepdims=True) + jnp.exp(sink - m)
            es.append(e.astype(BF16))
            inv.append(1.0 / den)
        o = _dot(jnp.concatenate(es, axis=0), v2)
        for pr in range(GQA_GROUP // 2):
            o_lo = o[(2 * pr) * blk:(2 * pr + 1) * blk] * inv[2 * pr]
            o_hi = o[(2 * pr + 1) * blk:(2 * pr + 2) * blk] * inv[2 * pr + 1]
            c0 = (kvh * (GQA_GROUP // 2) + pr) * LANES
            cat_ref[:, c0:c0 + LANES] = jnp.where(low, o_lo, o_hi).astype(BF16)
    for pp in range(N_RET_HEADS // 2):
        c0 = q_cols + pp * LANES
        qp = p_ref[:, c0:c0 + LANES]
        kp = p_ref[:, c0 + r_cols:c0 + r_cols + LANES]
        vp = p_ref[:, c0 + 2 * r_cols:c0 + 2 * r_cols + LANES]
        gp = p_ref[:, c0 + 3 * r_cols:c0 + 3 * r_cols + LANES].astype(F32)
        zero = jnp.zeros_like(qp)
        sc_lo = (_dot_nt(jnp.where(low, qp, zero), kp) * dmask_ref[2 * pp]).astype(BF16)
        sc_hi = (_dot_nt(jnp.where(low, zero, qp), kp) * dmask_ref[2 * pp + 1]).astype(BF16)
        o_inner = jnp.where(low, _dot(sc_lo, vp), _dot(sc_hi, vp))
        r_prev = r_ref[0, pp]
        qd = (qp.astype(F32) * qdec_ref[:, pp * LANES:(pp + 1) * LANES]).astype(BF16)
        o_cross = _dot(qd, r_prev.astype(BF16))
        kd = (kp.astype(F32) * kdec_ref[:, pp * LANES:(pp + 1) * LANES]).astype(BF16)
        kv_new = _dot_tn(kd, vp) * bd_ref[...]
        r_ref[0, pp] = cdec_ref[pp] * r_prev + kv_new
        y = _group_norm_pair(o_inner + o_cross, avg_ref[...])
        y = y * gnw_ref[:, pp * LANES:(pp + 1) * LANES] * _silu(gp)
        cat_ref[:, q_cols + pp * LANES:q_cols + (pp + 1) * LANES] = y.astype(BF16)


def _prompt_mixer(p_p, kv_p, sinks, gn_w, n_pbatch, t_prompt, ret_tabs):
    blk = WINDOW
    nb = t_prompt // blk
    p_cols = p_p.shape[1]
    kv_cols = kv_p.shape[1]
    d_mix = (N_ATTN_HEADS + N_RET_HEADS) * HEAD_DIM
    decay_mask, q_dec, k_dec, chunk_decay = ret_tabs
    n_pairs = N_RET_HEADS // 2
    bd = _block_diag_mask()
    cdec = jnp.stack([bd * chunk_decay[0, pp * LANES:(pp + 1) * LANES][:, None] for pp in range(n_pairs)])
    const2 = lambda shape: pl.BlockSpec(shape, lambda b, j: (0,) * len(shape))
    return pl.pallas_call(
        _prompt_mixer_body,
        grid=(n_pbatch, nb),
        in_specs=[
            pl.BlockSpec(memory_space=pltpu.SMEM),
            pl.BlockSpec((blk, p_cols), lambda b, j: (b * nb + j, 0)),
            pl.BlockSpec((blk, kv_cols), lambda b, j: (b * nb + j, 0)),
            pl.BlockSpec((blk, kv_cols), lambda b, j: (b * nb + jnp.maximum(j - 1, 0), 0)),
            const2((N_RET_HEADS, blk, blk)),
            const2((blk, N_RET_HEADS * HEAD_DIM)),
            const2((blk, N_RET_HEADS * HEAD_DIM)),
            const2((n_pairs, LANES, LANES)),
            const2((LANES, LANES)),
            const2((LANES, LANES)),
            const2((1, N_RET_HEADS * HEAD_DIM)),
        ],
        out_specs=[
            pl.BlockSpec((blk, d_mix), lambda b, j: (b * nb + j, 0)),
            pl.BlockSpec((1, n_pairs, LANES, LANES), lambda b, j: (b, 0, 0, 0)),
        ],
        out_shape=[
            jax.ShapeDtypeStruct((n_pbatch * t_prompt, d_mix), BF16),
            jax.ShapeDtypeStruct((n_pbatch, n_pairs, LANES, LANES), F32),
        ],
        compiler_params=_cparams(("arbitrary", "arbitrary")),
        name="prompt_mixer",
    )(sinks, p_p, kv_p, kv_p, decay_mask, q_dec, k_dec, cdec, bd, _pair_average_matrix(), gn_w.reshape(1, -1))


def _sample_mixer_body(t_new, sink_ref, p_ref, kv_ref, kbuf_ref, vbuf_ref, ret_ref, d4_ref, qdec_ref, kdec_ref,
                       cdec_ref, avg_ref, gnw_ref, cat_ref, knew_ref, vnew_ref, rnew_ref, ka2_ref, va2_ref, qs2_ref):
    w = kbuf_ref.shape[1]
    q_cols = N_ATTN_HEADS * HEAD_DIM
    r_cols = N_RET_HEADS * HEAD_DIM
    kvw = N_KV_HEADS * HEAD_DIM
    n_keys = ka2_ref.shape[1]
    rows_per_head = SUBLANES
    ka2_ref[...] = jnp.zeros_like(ka2_ref)
    va2_ref[...] = jnp.zeros_like(va2_ref)
    qs2_ref[...] = jnp.zeros_like(qs2_ref)
    low4 = _lane_is_low_half((t_new, LANES))
    lowk = _lane_is_low_half((n_keys, LANES))
    n_rows = GQA_GROUP * rows_per_head
    row = lax.broadcasted_iota(jnp.int32, (n_rows, n_keys), 0)
    key = lax.broadcasted_iota(jnp.int32, (n_rows, n_keys), 1)
    t_of_row = row % rows_per_head
    mask = (t_of_row < t_new) & (key > t_of_row) & (key <= t_of_row + WINDOW) & (key < w + t_new)
    head_of_row = lax.broadcasted_iota(jnp.int32, (n_rows, 1), 0) // rows_per_head

    def per_batch(b, ka_ref, va_ref, qs_ref):
        ka_ref[0:w, :] = kbuf_ref[b]
        va_ref[0:w, :] = vbuf_ref[b]
        ka_ref[w:w + t_new, :] = kv_ref[b][:, 0:kvw]
        va_ref[w:w + t_new, :] = kv_ref[b][:, kvw:2 * kvw]
        knew_ref[b] = ka_ref[t_new:t_new + w, :]
        vnew_ref[b] = va_ref[t_new:t_new + w, :]
        for h in range(N_ATTN_HEADS):
            c0 = (h // 2) * LANES
            qp = p_ref[b][:, c0:c0 + LANES]
            qs_ref[h * rows_per_head:h * rows_per_head + t_new, :] = jnp.where(
                low4 if h % 2 == 0 else jnp.logical_not(low4), qp, 0.0)
        kall = ka_ref[...]
        vall = va_ref[...]
        kswap = pltpu.roll(kall, HEAD_DIM, 1)
        vswap = pltpu.roll(vall, HEAD_DIM, 1)
        for kvh in range(N_KV_HEADS):
            keep = lowk if kvh == 0 else jnp.logical_not(lowk)
            k2 = jnp.where(keep, kall, kswap)
            v2 = jnp.where(keep, vall, vswap)
            qs = qs_ref[kvh * n_rows:(kvh + 1) * n_rows, :]
            s = jnp.where(mask, _dot_nt(qs, k2, HIGHEST), NEG_INF)
            sink = jnp.zeros((n_rows, 1), F32)
            for hh in range(GQA_GROUP):
                sink = jnp.where(head_of_row == hh, sink_ref[kvh * GQA_GROUP + hh], sink)
            m = jnp.maximum(jnp.max(s, axis=-1, keepdims=True), sink)
            e = jnp.exp(s - m)
            den = jnp.sum(e, axis=-1, keepdims=True) + jnp.exp(sink - m)
            o = _dot(e, v2, HIGHEST) / den
            for pr in range(GQA_GROUP // 2):
                o_lo = o[(2 * pr) * rows_per_head:(2 * pr) * rows_per_head + t_new]
                o_hi = o[(2 * pr + 1) * rows_per_head:(2 * pr + 1) * rows_per_head + t_new]
                c0 = (kvh * (GQA_GROUP // 2) + pr) * LANES
                cat_ref[b, :, c0:c0 + LANES] = jnp.where(low4, o_lo, o_hi)
        for pp in range(N_RET_HEADS // 2):
            c0 = q_cols + pp * LANES
            qp = p_ref[b][:, c0:c0 + LANES]
            kp = p_ref[b][:, c0 + r_cols:c0 + r_cols + LANES]
            vp = p_ref[b][:, c0 + 2 * r_cols:c0 + 2 * r_cols + LANES]
            gp = p_ref[b][:, c0 + 3 * r_cols:c0 + 3 * r_cols + LANES]
            r0 = ret_ref[b, pp]
            q_lo = jnp.where(low4, qp, 0.0)
            q_hi = jnp.where(low4, 0.0, qp)
            sc_lo = _dot_nt(q_lo, kp, HIGHEST) * d4_ref[2 * pp]
            sc_hi = _dot_nt(q_hi, kp, HIGHEST) * d4_ref[2 * pp + 1]
            o_inner = jnp.where(low4, _dot(sc_lo, vp, HIGHEST), _dot(sc_hi, vp, HIGHEST))
            qdec = qdec_ref[:, pp * LANES:(pp + 1) * LANES]
            oc_lo = _dot(q_lo * qdec, r0, HIGHEST)
            oc_hi = _dot(q_hi * qdec, r0, HIGHEST)
            o_cross = jnp.concatenate([oc_lo, oc_hi], axis=1)
            kd = kp * kdec_ref[:, pp * LANES:(pp + 1) * LANES]
            kv_full = _dot_tn(kd, vp, HIGHEST)
            kv_pair = jnp.concatenate([kv_full[0:HEAD_DIM, 0:HEAD_DIM], kv_full[HEAD_DIM:, HEAD_DIM:]], axis=0)
            rnew_ref[b, pp] = cdec_ref[pp] * r0 + kv_pair
            y = _group_norm_pair(o_inner + o_cross, avg_ref[...])
            y = y * gnw_ref[:, pp * LANES:(pp + 1) * LANES] * _silu(gp)
            cat_ref[b, :, q_cols + pp * LANES:q_cols + (pp + 1) * LANES] = y

    def per_pair(g, carry):
        for u in range(2):
            per_batch(2 * g + u, ka2_ref.at[u], va2_ref.at[u], qs2_ref.at[u])
        return carry

    lax.fori_loop(0, kbuf_ref.shape[0] // 2, per_pair, 0)


def _sample_mixer(p_s, kv_s, k_buf, v_buf, ret0, sinks, gn_w, ret_tabs):
    nb, t_new, p_cols = p_s.shape
    kv_cols = kv_s.shape[2]
    w = k_buf.shape[1]
    bb = SAMPLE_BATCH_TILE
    n_pairs = N_RET_HEADS // 2
    d_mix = (N_ATTN_HEADS + N_RET_HEADS) * HEAD_DIM
    decay_mask, q_dec, k_dec, chunk_decay = ret_tabs
    n_keys = ((w + t_new + SUBLANES - 1) // SUBLANES) * SUBLANES
    cdec = jnp.stack([jnp.broadcast_to(chunk_decay[0, pp * LANES:(pp + 1) * LANES][:, None], (LANES, HEAD_DIM))
                      for pp in range(n_pairs)])
    const1 = lambda shape: pl.BlockSpec(shape, lambda i: (0,) * len(shape))
    body = functools.partial(_sample_mixer_body, t_new)
    return pl.pallas_call(
        body,
        grid=(nb // bb,),
        in_specs=[
            pl.BlockSpec(memory_space=pltpu.SMEM),
            pl.BlockSpec((bb, t_new, p_cols), lambda i: (i, 0, 0)),
            pl.BlockSpec((bb, t_new, kv_cols), lambda i: (i, 0, 0)),
            pl.BlockSpec((bb, w, LANES), lambda i: (i, 0, 0)),
            pl.BlockSpec((bb, w, LANES), lambda i: (i, 0, 0)),
            pl.BlockSpec((bb, n_pairs, LANES, HEAD_DIM), lambda i: (i, 0, 0, 0)),
            const1((N_RET_HEADS, t_new, t_new)),
            const1((t_new, N_RET_HEADS * HEAD_DIM)),
            const1((t_new, N_RET_HEADS * HEAD_DIM)),
            const1((n_pairs, LANES, HEAD_DIM)),
            const1((LANES, LANES)),
            const1((1, N_RET_HEADS * HEAD_DIM)),
        ],
        out_specs=[
            pl.BlockSpec((bb, t_new, d_mix), lambda i: (i, 0, 0)),
            pl.BlockSpec((bb, w, LANES), lambda i: (i, 0, 0)),
            pl.BlockSpec((bb, w, LANES), lambda i: (i, 0, 0)),
            pl.BlockSpec((bb, n_pairs, LANES, HEAD_DIM), lambda i: (i, 0, 0, 0)),
        ],
        out_shape=[
            jax.ShapeDtypeStruct((nb, t_new, d_mix), F32),
            jax.ShapeDtypeStruct((nb, w, LANES), F32),
            jax.ShapeDtypeStruct((nb, w, LANES), F32),
            jax.ShapeDtypeStruct((nb, n_pairs, LANES, HEAD_DIM), F32),
        ],
        scratch_shapes=[
            pltpu.VMEM((2, n_keys, LANES), F32),
            pltpu.VMEM((2, n_keys, LANES), F32),
            pltpu.VMEM((2, N_ATTN_HEADS * SUBLANES, LANES), F32),
        ],
        compiler_params=_cparams(("arbitrary",)),
        name="sample_mixer",
    )(sinks, p_s, kv_s, k_buf, v_buf, ret0, decay_mask, q_dec, k_dec, cdec, _pair_average_matrix(), gn_w.reshape(1, -1))


def _router_body(geo, xp_ref, xs_ref, catp_ref, cats_ref, wout_ref, g1p_ref, g1s_ref, nw_ref, shp_ref, scp_ref,
                 shs_ref, scs_ref, wr_ref, br_ref, tri_ref, x1_ref, h2_ref, idx_ref, gate_ref, rank_ref, cnt_ref):
    i = pl.program_id(0)
    tv = functools.partial(_tile_vec, i, geo)
    cat = _tile_rows(i, geo, catp_ref, cats_ref)
    x1 = _tile_rows(i, geo, xp_ref, xs_ref) + tv(g1p_ref, g1s_ref) * _dot(cat, wout_ref[...])
    x1_ref[...] = x1
    h2 = _rms_norm(x1, nw_ref[...]) * (1.0 + tv(scp_ref, scs_ref)) + tv(shp_ref, shs_ref)
    h2_ref[...] = h2.reshape(h2_ref.shape)
    logits = _dot_nt(wr_ref[...], h2, HIGHEST) + br_ref[...]
    e_iota = lax.broadcasted_iota(jnp.int32, logits.shape, 0)
    vals, sels = [], []
    for _ in range(TOP_K):
        m = jnp.max(logits, axis=0, keepdims=True)
        sel = jnp.min(jnp.where(logits == m, e_iota, N_EXPERTS), axis=0, keepdims=True)
        vals.append(m)
        sels.append(sel)
        logits = jnp.where(e_iota == sel, -jnp.inf, logits)
    es = [jnp.exp(v - vals[0]) for v in vals]
    den = es[0]
    for e in es[1:]:
        den = den + e

    @pl.when(i == 0)
    def _():
        cnt_ref[...] = jnp.zeros_like(cnt_ref)

    base = cnt_ref[...]
    for k in range(TOP_K):
        hit = e_iota == sels[k]
        onehot = jnp.where(hit, 1.0, 0.0)
        before = _dot(onehot.astype(BF16), tri_ref[...])
        rank = jnp.sum(jnp.where(hit, base + before, 0.0), axis=0, keepdims=True)
        base = base + jnp.sum(onehot, axis=1, keepdims=True)
        idx_ref[k:k + 1, :] = sels[k]
        gate_ref[k:k + 1, :] = es[k] / den
        rank_ref[k:k + 1, :] = rank.astype(jnp.int32)
    cnt_ref[...] = base


def _outproj_router(x_p, x_s, cat_p, cat_s, w_out_bf16, norm_w, mod_p, mod_s, layer, w_router_t, b_router, geo):
    n_p, d = x_p.shape
    n = n_p + x_s.shape[0]
    d_mix = cat_p.shape[1]
    pv = lambda chunk: _prompt_vec_spec(layer, chunk, d)
    sv = lambda chunk: _sample_vec_spec(layer, chunk, geo, d)
    tile = lambda cols: pl.BlockSpec((TOKEN_TILE, cols), lambda i: (i, 0))
    choice = pl.BlockSpec((TOP_K, TOKEN_TILE), lambda i: (0, i))
    tri = jnp.asarray(np.triu(np.ones((TOKEN_TILE, TOKEN_TILE), np.float32), 1)).astype(BF16)
    return pl.pallas_call(
        functools.partial(_router_body, geo),
        grid=(n // TOKEN_TILE,),
        in_specs=[
            _prompt_rows_spec(geo, d), _sample_rows_spec(geo, d),
            _prompt_rows_spec(geo, d_mix), _sample_rows_spec(geo, d_mix),
            pl.BlockSpec((d_mix, d), lambda i: (0, 0)),
            pv(2), sv(2),
            pl.BlockSpec((1, d), lambda i: (0, 0)),
            pv(3), pv(4), sv(3), sv(4),
            pl.BlockSpec((N_EXPERTS, d), lambda i: (0, 0)),
            pl.BlockSpec((N_EXPERTS, 1), lambda i: (0, 0)),
            pl.BlockSpec((TOKEN_TILE, TOKEN_TILE), lambda i: (0, 0)),
        ],
        out_specs=[
            tile(d),
            pl.BlockSpec((TOKEN_TILE, 1, d), lambda i: (i, 0, 0)),
            choice, choice, choice,
            pl.BlockSpec((N_EXPERTS, 1), lambda i: (0, 0)),
        ],
        out_shape=[
            jax.ShapeDtypeStruct((n, d), F32),
            jax.ShapeDtypeStruct((n, 1, d), F32),
            jax.ShapeDtypeStruct((TOP_K, n), jnp.int32),
            jax.ShapeDtypeStruct((TOP_K, n), F32),
            jax.ShapeDtypeStruct((TOP_K, n), jnp.int32),
            jax.ShapeDtypeStruct((N_EXPERTS, 1), F32),
        ],
        compiler_params=_cparams(("arbitrary",)),
        name="outproj_router",
    )(x_p, x_s, cat_p, cat_s, w_out_bf16, mod_p, mod_s, norm_w.reshape(1, d), mod_p, mod_p, mod_s, mod_s,
      w_router_t, b_router.reshape(N_EXPERTS, 1), tri)


def _routing_plan(top_idx, rank, counts_f, tm):
    k, n = top_idx.shape
    n_pairs = k * n
    n_tiles = (n_pairs + N_EXPERTS * (tm - 1)) // tm + 1
    counts = counts_f[:, 0].astype(jnp.int32)
    tiles_per = (counts + tm - 1) // tm
    tile_end = jnp.cumsum(tiles_per)
    tile_start = tile_end - tiles_per
    n_used = tile_end[-1]
    n_slots = n_tiles * tm

    def lookup(table, index):
        ids = jnp.arange(table.shape[0], dtype=jnp.int32).reshape((-1,) + (1,) * index.ndim)
        return jnp.sum(jnp.where(index[None] == ids, table.reshape(ids.shape), 0), axis=0)

    slot_of_pair = lookup(tile_start * tm, top_idx) + rank
    pad_counts = jnp.concatenate([tiles_per * tm - counts, (n_slots - n_used * tm).reshape(1)])
    pad_end = jnp.cumsum(pad_counts)
    first_pad_slot = jnp.concatenate([tile_start * tm + counts, (n_used * tm).reshape(1)])
    k_pad = jnp.arange(n_slots - n_pairs, dtype=jnp.int32)
    seg = jnp.sum(k_pad[:, None] >= pad_end[None, :], axis=1).astype(jnp.int32)
    pad_slot = k_pad + lookup(first_pad_slot - (pad_end - pad_counts), seg)
    token_of_pair = jnp.broadcast_to(jnp.arange(n, dtype=jnp.int32)[None, :], (k, n))
    keys = jnp.concatenate([slot_of_pair.reshape(-1), pad_slot])
    vals = jnp.concatenate([token_of_pair.reshape(-1), jnp.zeros_like(pad_slot)])
    _, slot_rows = lax.sort((keys, vals), num_keys=1)
    tile_ids = jnp.arange(n_tiles, dtype=jnp.int32)
    tile_expert = jnp.minimum(jnp.sum(tile_ids[:, None] >= tile_end[None, :], axis=1), N_EXPERTS - 1).astype(jnp.int32)
    last_expert = tile_expert[jnp.maximum(n_used - 1, 0)]
    tile_expert = jnp.where(tile_ids < n_used, tile_expert, last_expert)
    return tile_expert, n_used.reshape(1).astype(jnp.int32), slot_rows.reshape(n_tiles, 1, tm), slot_of_pair


def _issue_row_copies(src_hbm, row_of, dst_row, sem, n_rows):
    def group(g, c):
        for u in range(DMA_UNROLL):
            r = g * DMA_UNROLL + u
            pltpu.make_async_copy(src_hbm.at[row_of(r)], dst_row(r), sem).start(priority=u % 2)
        return c
    lax.fori_loop(0, n_rows // DMA_UNROLL, group, 0)


def _issue_row_copies_static(src_hbm, row_of, dst_row, sem, lo, hi):
    for r in range(lo, hi):
        pltpu.make_async_copy(src_hbm.at[row_of(r)], dst_row(r), sem).start(priority=r % 2)


def _wait_row_copies(src_hbm, dst_buf, sem):
    pltpu.make_async_copy(src_hbm.at[pl.ds(0, dst_buf.shape[0])], dst_buf, sem).wait()


def _expert_body(te_ref, nt_ref, rows_cur_ref, rows_nxt_ref, h_hbm, wgu_ref, bg_ref, bu_ref, wd_ref, bd_ref, perm_ref,
                 y_ref, xbuf, sems, wgu_s, wd_s, act_s, x_s):
    j = pl.program_id(0)
    n_tiles = pl.num_programs(0)
    n_used = nt_ref[0]
    slot = j % 2
    tm = EXPERT_TILE
    f2 = wgu_ref.shape[2]
    n_chunks = f2 // MXU_WIDTH
    half = MXU_WIDTH // 2
    next_row = lambda r: rows_nxt_ref[0, 0, r]
    next_dst = lambda r: xbuf.at[1 - slot, r]

    @pl.when(j == 0)
    def _():
        _issue_row_copies(h_hbm, lambda r: rows_cur_ref[0, 0, r], lambda r: xbuf.at[0, r], sems.at[0], tm)

    new_expert = jnp.logical_or(j == 0, te_ref[j] != te_ref[jnp.maximum(j - 1, 0)])

    @pl.when(jnp.logical_and(j < n_used, new_expert))
    def _():
        for c in range(n_chunks):
            cols = slice(c * MXU_WIDTH, (c + 1) * MXU_WIDTH)
            wgu_s[:, cols] = _dot(wgu_ref[0, :, cols].astype(BF16), perm_ref[...]).astype(BF16)
        wd_s[...] = wd_ref[0].astype(BF16)

    @pl.when(j < n_used)
    def _():
        _wait_row_copies(h_hbm, xbuf.at[slot], sems.at[slot])
        x_s[...] = xbuf[slot][:, 0, :]
        x = x_s[...].astype(BF16)
        per_chunk = tm // n_chunks
        for c in range(n_chunks):
            _issue_row_copies_static(h_hbm, next_row, next_dst, sems.at[1 - slot], c * per_chunk, (c + 1) * per_chunk)
            gu = _dot(x, wgu_s[:, c * MXU_WIDTH:(c + 1) * MXU_WIDTH])
            glu = jnp.minimum(gu[:, :half] + bg_ref[0, :, c * half:(c + 1) * half], SWIGLU_LIMIT)
            lin = jnp.clip(gu[:, half:] + bu_ref[0, :, c * half:(c + 1) * half], -SWIGLU_LIMIT, SWIGLU_LIMIT)
            act_s[:, c * half:(c + 1) * half] = (glu * jax.nn.sigmoid(SWIGLU_ALPHA * glu) * (lin + 1.0)).astype(BF16)
        y = _dot(act_s[...], wd_s[...]) + bd_ref[0]
        y_ref[...] = y.reshape(y_ref.shape)

    @pl.when(j >= n_used)
    def _():
        _wait_row_copies(h_hbm, xbuf.at[slot], sems.at[slot])
        _issue_row_copies(h_hbm, next_row, next_dst, sems.at[1 - slot], tm)
        y_ref[...] = jnp.zeros_like(y_ref)

    @pl.when(j == n_tiles - 1)
    def _():
        _wait_row_copies(h_hbm, xbuf.at[1 - slot], sems.at[1 - slot])


def _deinterleave_matrix():
    m = np.zeros((MXU_WIDTH, MXU_WIDTH), np.float32)
    j = np.arange(MXU_WIDTH // 2)
    m[2 * j, j] = 1.0
    m[2 * j + 1, MXU_WIDTH // 2 + j] = 1.0
    return jnp.asarray(m).astype(BF16)


def _routed_experts(h2_rows, tile_expert, n_used, slot_rows, wgu, bg, bu, wd, bd):
    n, _, d = h2_rows.shape
    f2 = wgu.shape[2]
    f = f2 // 2
    tm = EXPERT_TILE
    n_tiles = slot_rows.shape[0]
    by_expert = lambda shape: pl.BlockSpec(shape, lambda j, te, nt: (te[j], 0, 0))
    grid_spec = pltpu.PrefetchScalarGridSpec(
        num_scalar_prefetch=2,
        grid=(n_tiles,),
        in_specs=[
            pl.BlockSpec((1, 1, tm), lambda j, te, nt: (j, 0, 0), memory_space=pltpu.SMEM),
            pl.BlockSpec((1, 1, tm), lambda j, te, nt: (jnp.minimum(j + 1, n_tiles - 1), 0, 0), memory_space=pltpu.SMEM),
            pl.BlockSpec(memory_space=pl.ANY),
            by_expert((1, d, f2)),
            by_expert((1, 1, f)), by_expert((1, 1, f)),
            by_expert((1, f, d)),
            by_expert((1, 1, d)),
            pl.BlockSpec((MXU_WIDTH, MXU_WIDTH), lambda j, te, nt: (0, 0)),
        ],
        out_specs=pl.BlockSpec((tm, 1, d), lambda j, te, nt: (j, 0, 0)),
        scratch_shapes=[
            pltpu.VMEM((2, tm, 1, d), F32),
            pltpu.SemaphoreType.DMA((2,)),
            pltpu.VMEM((d, f2), BF16),
            pltpu.VMEM((f, d), BF16),
            pltpu.VMEM((tm, f), BF16),
            pltpu.VMEM((tm, d), F32),
        ],
    )
    return pl.pallas_call(
        _expert_body,
        grid_spec=grid_spec,
        out_shape=jax.ShapeDtypeStruct((n_tiles * tm, 1, d), F32),
        compiler_params=_cparams(("arbitrary",)),
        name="routed_experts",
    )(tile_expert, n_used, slot_rows, slot_rows, h2_rows, wgu, bg, bu, wd, bd, _deinterleave_matrix())


def _combine_body(geo, final_norm, slots_cur_ref, slots_nxt_ref, x1_ref, gate_ref, g2p_ref, g2s_ref, nfw_ref, y_hbm,
                  outp_ref, outs_ref, ybuf, sems):
    i = pl.program_id(0)
    n_steps = pl.num_programs(0)
    slot = i % 2
    tm = TOKEN_TILE

    def issue(rows_ref, sl):
        for k in range(TOP_K):
            _issue_row_copies(y_hbm, lambda r, k=k: rows_ref[0, k, r], lambda r, k=k: ybuf.at[sl, k, r], sems.at[sl], tm)

    @pl.when(i == 0)
    def _():
        issue(slots_cur_ref, 0)

    @pl.when(i + 1 < n_steps)
    def _():
        issue(slots_nxt_ref, 1 - slot)

    for k in range(TOP_K):
        _wait_row_copies(y_hbm, ybuf.at[slot, k], sems.at[slot])
    acc = ybuf[slot, 0][:, 0, :] * gate_ref[:, 0:1]
    for k in range(1, TOP_K):
        acc = acc + ybuf[slot, k][:, 0, :] * gate_ref[:, k:k + 1]
    x2 = x1_ref[...] + _tile_vec(i, geo, g2p_ref, g2s_ref) * acc
    if final_norm:
        x2 = _rms_norm(x2, nfw_ref[...])

    @pl.when(i < geo.n_ptiles)
    def _():
        outp_ref[...] = x2

    @pl.when(i >= geo.n_ptiles)
    def _():
        outs_ref[...] = x2


def _combine(x1, y_rows, slot_of_pair, gates_t, mod_p, mod_s, layer, final_w, geo, final_norm):
    n, d = x1.shape
    tm = TOKEN_TILE
    n_steps = n // tm
    n_p = geo.n_ptiles * tm
    slots = slot_of_pair.reshape(TOP_K, n_steps, tm).transpose(1, 0, 2)
    return pl.pallas_call(
        functools.partial(_combine_body, geo, final_norm),
        grid=(n_steps,),
        in_specs=[
            pl.BlockSpec((1, TOP_K, tm), lambda i: (i, 0, 0), memory_space=pltpu.SMEM),
            pl.BlockSpec((1, TOP_K, tm), lambda i: (jnp.minimum(i + 1, n_steps - 1), 0, 0), memory_space=pltpu.SMEM),
            pl.BlockSpec((tm, d), lambda i: (i, 0)),
            pl.BlockSpec((tm, TOP_K), lambda i: (i, 0)),
            _prompt_vec_spec(layer, 5, d),
            _sample_vec_spec(layer, 5, geo, d),
            pl.BlockSpec((1, d), lambda i: (0, 0)),
            pl.BlockSpec(memory_space=pl.ANY),
        ],
        out_specs=[_prompt_rows_spec(geo, d), _sample_rows_spec(geo, d)],
        out_shape=[jax.ShapeDtypeStruct((n_p, d), F32), jax.ShapeDtypeStruct((n - n_p, d), F32)],
        scratch_shapes=[pltpu.VMEM((2, TOP_K, tm, 1, d), F32), pltpu.SemaphoreType.DMA((2,))],
        compiler_params=_cparams(("arbitrary",)),
        name="moe_combine",
    )(slots, slots, x1, gates_t, mod_p, mod_s, final_w.reshape(1, d), y_rows)


def kernel(x_prompt, x_sample, state_swa_k, state_swa_v, state_ret, c_prompt, c_sample, norm_mix_w, w_ada, b_ada, w_in, attn_sinks, ret_gn_w, w_out, norm_ffn_w, w_router, b_router, w_gate_up, b_gate_up, w_down, b_down, norm_final_w):
    n_pbatch, t_prompt, d = x_prompt.shape
    n_sbatch, t_sample, _ = x_sample.shape
    depth = w_in.shape[0]
    n_p = n_pbatch * t_prompt
    n_s = n_sbatch * t_sample
    win = state_swa_k.shape[2]
    assert d == (N_ATTN_HEADS + N_RET_HEADS) * HEAD_DIM and w_gate_up.shape[1] == N_EXPERTS
    assert t_prompt % TOKEN_TILE == 0 and n_s % TOKEN_TILE == 0 and TOKEN_TILE % t_sample == 0
    assert t_prompt % WINDOW == 0 and RET_CHUNK == WINDOW and win == WINDOW and t_sample <= SUBLANES
    assert n_sbatch % SAMPLE_BATCH_TILE == 0 and n_pbatch <= SUBLANES
    assert w_gate_up.shape[3] % MXU_WIDTH == 0 and EXPERT_TILE % DMA_UNROLL == 0 and TOKEN_TILE % DMA_UNROLL == 0
    geo = _Geometry(n_p // TOKEN_TILE, t_prompt // TOKEN_TILE, n_pbatch)

    c_all = jnp.concatenate([c_prompt, jnp.zeros((SUBLANES - n_pbatch, d), F32), c_sample], axis=0)
    mod = _ada_modulation(c_all, w_ada, b_ada)
    mod_p = mod[:, :SUBLANES]
    mod_s = jnp.repeat(mod[:, SUBLANES:], t_sample, axis=1)

    rope_tab = _rope_tables(t_prompt, t_sample)
    ret_tabs_p = _retention_tables(RET_CHUNK)
    ret_tabs_s = _retention_tables(t_sample)
    f = w_gate_up.shape[3] // 2

    x_p = x_prompt.reshape(n_p, d)
    x_s = x_sample.reshape(n_s, d)
    kp_l, vp_l, rp_l, ks_l, vs_l, rs_l = [], [], [], [], [], []
    for l in range(depth):
        p_all, kv_all = _input_projection(x_p, x_s, norm_mix_w[l], mod_p, mod_s, l, w_in[l].astype(BF16), rope_tab, geo)
        cat_p, r_pairs = _prompt_mixer(p_all, kv_all, attn_sinks[l], ret_gn_w[l], n_pbatch, t_prompt, ret_tabs_p)
        p_s = p_all[n_p:].astype(F32)
        kv_s = kv_all[n_p:]
        kv_p = kv_all[:n_p]
        cat_s, k_new, v_new, r_new = _sample_mixer(
            p_s.reshape(n_sbatch, t_sample, -1), kv_s.reshape(n_sbatch, t_sample, -1),
            state_swa_k[l].reshape(n_sbatch, win, LANES), state_swa_v[l].reshape(n_sbatch, win, LANES),
            state_ret[l].reshape(n_sbatch, N_RET_HEADS // 2, LANES, HEAD_DIM), attn_sinks[l], ret_gn_w[l], ret_tabs_s)
        x1, h2_rows, top_idx, gates, rank, counts = _outproj_router(
            x_p, x_s, cat_p, cat_s.reshape(n_s, d), w_out[l].astype(BF16), norm_ffn_w[l], mod_p, mod_s, l,
            w_router[l].T, b_router[l], geo)
        tile_expert, n_used, slot_rows, slot_of_pair = _routing_plan(top_idx, rank, counts, EXPERT_TILE)
        b_gu = b_gate_up[l].reshape(N_EXPERTS, 1, f, 2)
        y_rows = _routed_experts(h2_rows, tile_expert, n_used, slot_rows, w_gate_up[l],
                                 b_gu[..., 0], b_gu[..., 1], w_down[l], b_down[l][:, None, :])
        x_p, x_s = _combine(x1, y_rows, slot_of_pair, gates.T, mod_p, mod_s, l, norm_final_w, geo,
                            final_norm=(l == depth - 1))

        kv_last = kv_p.reshape(n_pbatch, t_prompt, -1)[:, t_prompt - WINDOW:]
        kv_last = kv_last.reshape(n_pbatch, WINDOW, 2, N_KV_HEADS, HEAD_DIM)
        kp_l.append(kv_last[:, :, 0])
        vp_l.append(kv_last[:, :, 1])
        rp_l.append(jnp.stack([r_pairs[:, :, :HEAD_DIM, :HEAD_DIM], r_pairs[:, :, HEAD_DIM:, HEAD_DIM:]], axis=2)
                    .reshape(n_pbatch, N_RET_HEADS, HEAD_DIM, HEAD_DIM))
        ks_l.append(k_new.reshape(n_sbatch, win, N_KV_HEADS, HEAD_DIM))
        vs_l.append(v_new.reshape(n_sbatch, win, N_KV_HEADS, HEAD_DIM))
        rs_l.append(r_new.reshape(n_sbatch, N_RET_HEADS, HEAD_DIM, HEAD_DIM))
    return (x_p.reshape(n_pbatch, t_prompt, d), x_s.reshape(n_sbatch, t_sample, d), jnp.stack(kp_l), jnp.stack(vp_l),
            jnp.stack(rp_l), jnp.stack(ks_l), jnp.stack(vs_l), jnp.stack(rs_l))
```

```python
import functools

import numpy as np
import jax
import jax.numpy as jnp
from jax import lax
from jax.experimental import pallas as pl
from jax.experimental.pallas import tpu as pltpu

F32 = jnp.float32
BF16 = jnp.bfloat16
HIGHEST = lax.Precision.HIGHEST

HEAD_DIM = 64
N_ATTN_HEADS = 8
N_KV_HEADS = 2
GQA_GROUP = N_ATTN_HEADS // N_KV_HEADS
WINDOW = 128
ROPE_THETA = 500000.0
ROPE_DIMS = HEAD_DIM // 4
N_RET_HEADS = 8
RET_CHUNK = 128
RET_THETA = 10000.0
N_EXPERTS = 32
TOP_K = 4
SWIGLU_LIMIT = 7.0
SWIGLU_ALPHA = 1.702
NORM_EPS = 1e-5
PAST_LEN = 16384

LANES = 128
SUBLANES = 8
MXU_WIDTH = 256
VMEM_LIMIT_BYTES = 56 * 1024 * 1024

TOKEN_TILE = 256
EXPERT_TILE = 256
SAMPLE_BATCH_TILE = 8
DMA_UNROLL = 8
ROW_BUFFERS = 3

NEG_INF = -1e30


def _cparams(semantics):
    return pltpu.CompilerParams(dimension_semantics=semantics, vmem_limit_bytes=VMEM_LIMIT_BYTES)


def _lane_is_low_half(shape):
    return lax.broadcasted_iota(jnp.int32, shape, len(shape) - 1) < HEAD_DIM


def _dot(a, b, precision=None):
    return jnp.dot(a, b, preferred_element_type=F32, precision=precision)


def _dot_nt(a, b, precision=None):
    return lax.dot_general(a, b, (((1,), (1,)), ((), ())), preferred_element_type=F32, precision=precision)


def _dot_tn(a, b, precision=None):
    return lax.dot_general(a, b, (((0,), (0,)), ((), ())), preferred_element_type=F32, precision=precision)


def _rms_norm(x, w):
    return x * lax.rsqrt(jnp.mean(x * x, axis=-1, keepdims=True) + NORM_EPS) * w


def _silu(x):
    return x * jax.nn.sigmoid(x)


def _ada_body(c_ref, w_ref, b_ref, o_ref):
    a = _silu(c_ref[...])
    o_ref[0] = _dot(a, w_ref[0], HIGHEST) + b_ref[0]


def _ada_modulation(c_all, w_ada, b_ada):
    depth, d, cols = w_ada.shape
    rows = c_all.shape[0]
    tn = 1024
    return pl.pallas_call(
        _ada_body,
        grid=(depth, cols // tn),
        in_specs=[
            pl.BlockSpec((rows, d), lambda l, j: (0, 0)),
            pl.BlockSpec((1, d, tn), lambda l, j: (l, 0, j)),
            pl.BlockSpec((1, 1, tn), lambda l, j: (l, 0, j)),
        ],
        out_specs=pl.BlockSpec((1, rows, tn), lambda l, j: (l, 0, j)),
        out_shape=jax.ShapeDtypeStruct((depth, rows, cols), F32),
        compiler_params=_cparams(("arbitrary", "arbitrary")),
        name="ada_modulation",
    )(c_all, w_ada, b_ada.reshape(depth, 1, cols))


class _Geometry:
    def __init__(self, n_ptiles, tiles_per_batch, n_pbatch):
        self.n_ptiles = n_ptiles
        self.tiles_per_batch = tiles_per_batch
        self.n_pbatch = n_pbatch


def _tile_vec(i, geo, prompt_ref, sample_ref):
    b = jnp.minimum(i // geo.tiles_per_batch, geo.n_pbatch - 1)
    return jnp.where(i >= geo.n_ptiles, sample_ref[...], prompt_ref[pl.ds(b, 1), :])


def _tile_rows(i, geo, prompt_ref, sample_ref):
    return jnp.where(i >= geo.n_ptiles, sample_ref[...].astype(prompt_ref.dtype), prompt_ref[...])


def _prompt_vec_spec(layer, chunk, d):
    return pl.BlockSpec((None, SUBLANES, d), lambda i: (layer, 0, chunk))


def _sample_vec_spec(layer, chunk, geo, d):
    return pl.BlockSpec((None, TOKEN_TILE, d), lambda i: (layer, jnp.maximum(i - geo.n_ptiles, 0), chunk))


def _prompt_rows_spec(geo, cols):
    return pl.BlockSpec((TOKEN_TILE, cols), lambda i: (jnp.minimum(i, geo.n_ptiles - 1), 0))


def _sample_rows_spec(geo, cols):
    return pl.BlockSpec((TOKEN_TILE, cols), lambda i: (jnp.maximum(i - geo.n_ptiles, 0), 0))


def _rotate(xc, tab_ref, base, shift):
    return (xc * tab_ref[base] + pltpu.roll(xc, shift, 1) * tab_ref[base + 1]
            + pltpu.roll(xc, LANES - shift, 1) * tab_ref[base + 2])


def _inproj_body(geo, xp_ref, xs_ref, nw_ref, shp_ref, scp_ref, shs_ref, scs_ref, w_ref, tab_ref, p_ref, kv_ref):
    i = pl.program_id(0)
    x = _tile_rows(i, geo, xp_ref, xs_ref)
    h = _rms_norm(x, nw_ref[...]) * (1.0 + _tile_vec(i, geo, scp_ref, scs_ref)) + _tile_vec(i, geo, shp_ref, shs_ref)
    proj = _dot(h.astype(BF16), w_ref[...])
    q_cols = N_ATTN_HEADS * HEAD_DIM
    kv_cols = N_KV_HEADS * HEAD_DIM
    r_cols = N_RET_HEADS * HEAD_DIM
    o_ka = q_cols
    o_va = o_ka + kv_cols
    o_qr = o_va + kv_cols
    o_kr = o_qr + r_cols
    o_vr = o_kr + r_cols
    o_g = o_vr + r_cols
    attn_scale = HEAD_DIM ** -0.5
    ret_scale = HEAD_DIM ** -0.5
    half_a = ROPE_DIMS // 2
    half_r = HEAD_DIM // 2

    def emit(p_ref, kv_ref):
        dt = p_ref.dtype
        for c in range(q_cols // LANES):
            xc = proj[:, c * LANES:(c + 1) * LANES]
            p_ref[:, c * LANES:(c + 1) * LANES] = (_rotate(xc, tab_ref, 0, half_a) * attn_scale).astype(dt)
        kv_ref[:, 0:kv_cols] = _rotate(proj[:, o_ka:o_ka + kv_cols], tab_ref, 0, half_a)
        kv_ref[:, kv_cols:2 * kv_cols] = proj[:, o_va:o_va + kv_cols]
        for c in range(r_cols // LANES):
            xq = proj[:, o_qr + c * LANES:o_qr + (c + 1) * LANES]
            xk = proj[:, o_kr + c * LANES:o_kr + (c + 1) * LANES]
            p_ref[:, q_cols + c * LANES:q_cols + (c + 1) * LANES] = _rotate(xq, tab_ref, 3, half_r).astype(dt)
            p_ref[:, q_cols + r_cols + c * LANES:q_cols + r_cols + (c + 1) * LANES] = (
                _rotate(xk, tab_ref, 3, half_r) * ret_scale).astype(dt)
        p_ref[:, q_cols + 2 * r_cols:q_cols + 3 * r_cols] = proj[:, o_vr:o_vr + r_cols].astype(dt)
        p_ref[:, q_cols + 3 * r_cols:q_cols + 4 * r_cols] = proj[:, o_g:o_g + r_cols].astype(dt)

    emit(p_ref, kv_ref)


def _input_projection(x_p, x_s, norm_w, mod_p, mod_s, layer, w_in_bf16, rope_tab, geo):
    n_p, d = x_p.shape
    n_s = x_s.shape[0]
    in_cols = w_in_bf16.shape[1]
    kv_cols = 2 * N_KV_HEADS * HEAD_DIM
    p_cols = in_cols - kv_cols
    tpb = geo.tiles_per_batch
    tab_index = lambda i: (0, jnp.where(i >= geo.n_ptiles, tpb, i % tpb), 0)
    return pl.pallas_call(
        functools.partial(_inproj_body, geo),
        grid=((n_p + n_s) // TOKEN_TILE,),
        in_specs=[
            _prompt_rows_spec(geo, d), _sample_rows_spec(geo, d),
            pl.BlockSpec((1, d), lambda i: (0, 0)),
            _prompt_vec_spec(layer, 0, d), _prompt_vec_spec(layer, 1, d),
            _sample_vec_spec(layer, 0, geo, d), _sample_vec_spec(layer, 1, geo, d),
            pl.BlockSpec((d, in_cols), lambda i: (0, 0)),
            pl.BlockSpec((6, TOKEN_TILE, LANES), tab_index),
        ],
        out_specs=[
            pl.BlockSpec((TOKEN_TILE, p_cols), lambda i: (i, 0)),
            pl.BlockSpec((TOKEN_TILE, kv_cols), lambda i: (i, 0)),
        ],
        out_shape=[
            jax.ShapeDtypeStruct((n_p + n_s, p_cols), BF16),
            jax.ShapeDtypeStruct((n_p + n_s, kv_cols), F32),
        ],
        compiler_params=_cparams(("arbitrary",)),
        name="input_projection",
    )(x_p, x_s, norm_w.reshape(1, d), mod_p, mod_p, mod_s, mod_s, w_in_bf16, rope_tab)


def _rope_tables(t_prompt, t_sample):
    pos = jnp.concatenate([jnp.arange(t_prompt, dtype=jnp.int32),
                           PAST_LEN + (jnp.arange(TOKEN_TILE, dtype=jnp.int32) % t_sample)])
    d = np.arange(LANES) % HEAD_DIM

    def tables(n_dims, theta):
        half = n_dims // 2
        freqs = jnp.power(jnp.float32(theta), -jnp.arange(half, dtype=jnp.float32) / half)
        ang = pos.astype(jnp.float32)[:, None] * freqs[None, :]
        cos, sin = jnp.cos(ang), jnp.sin(ang)
        fidx = np.where(d < n_dims, d % half, 0)
        cos_l = jnp.where(jnp.asarray(d < n_dims)[None, :], cos[:, fidx], 1.0)
        sin_l = sin[:, fidx]
        upper = jnp.asarray((d >= half) & (d < n_dims))[None, :]
        lower = jnp.asarray(d < half)[None, :]
        return [cos_l, jnp.where(upper, sin_l, 0.0), jnp.where(lower, -sin_l, 0.0)]

    return jnp.stack(tables(ROPE_DIMS, ROPE_THETA) + tables(HEAD_DIM, RET_THETA)).astype(F32)


def _retention_tables(c):
    h = N_RET_HEADS
    log_gamma = jnp.log(1.0 - jnp.power(2.0, -5.0 - jnp.arange(h, dtype=jnp.float32)))
    idx = jnp.arange(c, dtype=jnp.float32)
    diff = idx[:, None] - idx[None, :]
    decay_mask = jnp.where(diff >= 0, jnp.exp(log_gamma[:, None, None] * jnp.maximum(diff, 0.0)), 0.0)
    k_dec = jnp.exp(log_gamma[None, :] * (c - 1 - idx)[:, None])
    q_dec = jnp.exp(log_gamma[None, :] * (idx + 1.0)[:, None])
    chunk_decay = jnp.exp(log_gamma * c)
    rep = lambda a: jnp.repeat(a, HEAD_DIM, axis=-1)
    return decay_mask.astype(F32), rep(q_dec).astype(F32), rep(k_dec).astype(F32), rep(chunk_decay[None, :]).astype(F32)


def _group_norm_pair(o, avg):
    mu = _dot(o, avg, HIGHEST)
    dlt = o - mu
    var = _dot(dlt * dlt, avg, HIGHEST)
    return dlt * lax.rsqrt(var + NORM_EPS)


def _pair_average_matrix():
    r = np.arange(LANES)
    return jnp.asarray(((r[:, None] // HEAD_DIM) == (r[None, :] // HEAD_DIM)).astype(np.float32) / HEAD_DIM)


def _block_diag_mask():
    r = np.arange(LANES)
    return jnp.asarray(((r[:, None] // HEAD_DIM) == (r[None, :] // HEAD_DIM)).astype(np.float32))


def _prompt_mixer_body(sink_ref, p_ref, kvc_ref, kvp_ref, dmask_ref, qdec_ref, kdec_ref, cdec_ref,
                       bd_ref, avg_ref, gnw_ref, cat_ref, r_ref):
    jb = pl.program_id(1)
    blk = WINDOW
    q_cols = N_ATTN_HEADS * HEAD_DIM
    r_cols = N_RET_HEADS * HEAD_DIM
    kvw = N_KV_HEADS * HEAD_DIM

    @pl.when(jb == 0)
    def _():
        r_ref[...] = jnp.zeros_like(r_ref)

    low = _lane_is_low_half((blk, LANES))
    kband = jnp.concatenate([kvp_ref[:, 0:kvw], kvc_ref[:, 0:kvw]], axis=0)
    vband = jnp.concatenate([kvp_ref[:, kvw:2 * kvw], kvc_ref[:, kvw:2 * kvw]], axis=0)
    low2 = _lane_is_low_half((2 * blk, LANES))
    kswap = pltpu.roll(kband, HEAD_DIM, 1)
    vswap = pltpu.roll(vband, HEAD_DIM, 1)
    qi = lax.broadcasted_iota(jnp.int32, (blk, 2 * blk), 0)
    kj = lax.broadcasted_iota(jnp.int32, (blk, 2 * blk), 1)
    dist = blk + qi - kj
    mask = (dist >= 0) & (dist < WINDOW) & ((kj >= blk) | (jb > 0))
    for kvh in range(N_KV_HEADS):
        keep = low2 if kvh == 0 else jnp.logical_not(low2)
        k2 = jnp.where(keep, kband, kswap).astype(BF16)
        v2 = jnp.where(keep, vband, vswap).astype(BF16)
        pieces = []
        for pr in range(GQA_GROUP // 2):
            c0 = (kvh * (GQA_GROUP // 2) + pr) * LANES
            qp = p_ref[:, c0:c0 + LANES]
            pieces.append(jnp.where(low, qp, jnp.zeros_like(qp)))
            pieces.append(jnp.where(low, jnp.zeros_like(qp), qp))
        qs = jnp.concatenate(pieces, axis=0)
        s = _dot_nt(qs, k2)
        es, inv = [], []
        for hh in range(GQA_GROUP):
            sink = sink_ref[kvh * GQA_GROUP + hh]
            sh = jnp.where(mask, s[hh * blk:(hh + 1) * blk], NEG_INF)
            m = jnp.maximum(jnp.max(sh, axis=-1, keepdims=True), sink)
            e = jnp.exp(sh - m)
            den = jnp.sum(e, axis=-1, keepdims=True) + jnp.exp(sink - m)
            es.append(e.astype(BF16))
            inv.append(1.0 / den)
        o = _dot(jnp.concatenate(es, axis=0), v2)
        for pr in range(GQA_GROUP // 2):
            o_lo = o[(2 * pr) * blk:(2 * pr + 1) * blk] * inv[2 * pr]
            o_hi = o[(2 * pr + 1) * blk:(2 * pr + 2) * blk] * inv[2 * pr + 1]
            c0 = (kvh * (GQA_GROUP // 2) + pr) * LANES
            cat_ref[:, c0:c0 + LANES] = jnp.where(low, o_lo, o_hi).astype(BF16)
    for pp in range(N_RET_HEADS // 2):
        c0 = q_cols + pp * LANES
        qp = p_ref[:, c0:c0 + LANES]
        kp = p_ref[:, c0 + r_cols:c0 + r_cols + LANES]
        vp = p_ref[:, c0 + 2 * r_cols:c0 + 2 * r_cols + LANES]
        gp = p_ref[:, c0 + 3 * r_cols:c0 + 3 * r_cols + LANES].astype(F32)
        zero = jnp.zeros_like(qp)
        sc_lo = (_dot_nt(jnp.where(low, qp, zero), kp) * dmask_ref[2 * pp]).astype(BF16)
        sc_hi = (_dot_nt(jnp.where(low, zero, qp), kp) * dmask_ref[2 * pp + 1]).astype(BF16)
        o_inner = jnp.where(low, _dot(sc_lo, vp), _dot(sc_hi, vp))
        r_prev = r_ref[0, pp]
        qd = (qp.astype(F32) * qdec_ref[:, pp * LANES:(pp + 1) * LANES]).astype(BF16)
        o_cross = _dot(qd, r_prev.astype(BF16))
        kd = (kp.astype(F32) * kdec_ref[:, pp * LANES:(pp + 1) * LANES]).astype(BF16)
        kv_new = _dot_tn(kd, vp) * bd_ref[...]
        r_ref[0, pp] = cdec_ref[pp] * r_prev + kv_new
        y = _group_norm_pair(o_inner + o_cross, avg_ref[...])
        y = y * gnw_ref[:, pp * LANES:(pp + 1) * LANES] * _silu(gp)
        cat_ref[:, q_cols + pp * LANES:q_cols + (pp + 1) * LANES] = y.astype(BF16)


def _prompt_mixer(p_p, kv_p, sinks, gn_w, n_pbatch, t_prompt, ret_tabs):
    blk = WINDOW
    nb = t_prompt // blk
    p_cols = p_p.shape[1]
    kv_cols = kv_p.shape[1]
    d_mix = (N_ATTN_HEADS + N_RET_HEADS) * HEAD_DIM
    decay_mask, q_dec, k_dec, chunk_decay = ret_tabs
    n_pairs = N_RET_HEADS // 2
    bd = _block_diag_mask()
    cdec = jnp.stack([bd * chunk_decay[0, pp * LANES:(pp + 1) * LANES][:, None] for pp in range(n_pairs)])
    const2 = lambda shape: pl.BlockSpec(shape, lambda b, j: (0,) * len(shape))
    return pl.pallas_call(
        _prompt_mixer_body,
        grid=(n_pbatch, nb),
        in_specs=[
            pl.BlockSpec(memory_space=pltpu.SMEM),
            pl.BlockSpec((blk, p_cols), lambda b, j: (b * nb + j, 0)),
            pl.BlockSpec((blk, kv_cols), lambda b, j: (b * nb + j, 0)),
            pl.BlockSpec((blk, kv_cols), lambda b, j: (b * nb + jnp.maximum(j - 1, 0), 0)),
            const2((N_RET_HEADS, blk, blk)),
            const2((blk, N_RET_HEADS * HEAD_DIM)),
            const2((blk, N_RET_HEADS * HEAD_DIM)),
            const2((n_pairs, LANES, LANES)),
            const2((LANES, LANES)),
            const2((LANES, LANES)),
            const2((1, N_RET_HEADS * HEAD_DIM)),
        ],
        out_specs=[
            pl.BlockSpec((blk, d_mix), lambda b, j: (b * nb + j, 0)),
            pl.BlockSpec((1, n_pairs, LANES, LANES), lambda b, j: (b, 0, 0, 0)),
        ],
        out_shape=[
            jax.ShapeDtypeStruct((n_pbatch * t_prompt, d_mix), BF16),
            jax.ShapeDtypeStruct((n_pbatch, n_pairs, LANES, LANES), F32),
        ],
        compiler_params=_cparams(("arbitrary", "arbitrary")),
        name="prompt_mixer",
    )(sinks, p_p, kv_p, kv_p, decay_mask, q_dec, k_dec, cdec, bd, _pair_average_matrix(), gn_w.reshape(1, -1))


def _sample_mixer_body(t_new, sink_ref, p_ref, kv_ref, kbuf_ref, vbuf_ref, ret_ref, d4_ref, qdec_ref, kdec_ref,
                       cdec_ref, avg_ref, gnw_ref, cat_ref, knew_ref, vnew_ref, rnew_ref, ka2_ref, va2_ref, qs2_ref):
    w = kbuf_ref.shape[1]
    q_cols = N_ATTN_HEADS * HEAD_DIM
    r_cols = N_RET_HEADS * HEAD_DIM
    kvw = N_KV_HEADS * HEAD_DIM
    n_keys = ka2_ref.shape[1]
    rows_per_head = SUBLANES
    ka2_ref[...] = jnp.zeros_like(ka2_ref)
    va2_ref[...] = jnp.zeros_like(va2_ref)
    qs2_ref[...] = jnp.zeros_like(qs2_ref)
    low4 = _lane_is_low_half((t_new, LANES))
    lowk = _lane_is_low_half((n_keys, LANES))
    n_rows = GQA_GROUP * rows_per_head
    row = lax.broadcasted_iota(jnp.int32, (n_rows, n_keys), 0)
    key = lax.broadcasted_iota(jnp.int32, (n_rows, n_keys), 1)
    t_of_row = row % rows_per_head
    mask = (t_of_row < t_new) & (key > t_of_row) & (key <= t_of_row + WINDOW) & (key < w + t_new)
    head_of_row = lax.broadcasted_iota(jnp.int32, (n_rows, 1), 0) // rows_per_head

    def per_batch(b, ka_ref, va_ref, qs_ref):
        ka_ref[0:w, :] = kbuf_ref[b]
        va_ref[0:w, :] = vbuf_ref[b]
        ka_ref[w:w + t_new, :] = kv_ref[b][:, 0:kvw]
        va_ref[w:w + t_new, :] = kv_ref[b][:, kvw:2 * kvw]
        knew_ref[b] = ka_ref[t_new:t_new + w, :]
        vnew_ref[b] = va_ref[t_new:t_new + w, :]
        for h in range(N_ATTN_HEADS):
            c0 = (h // 2) * LANES
            qp = p_ref[b][:, c0:c0 + LANES]
            qs_ref[h * rows_per_head:h * rows_per_head + t_new, :] = jnp.where(
                low4 if h % 2 == 0 else jnp.logical_not(low4), qp, 0.0)
        kall = ka_ref[...]
        vall = va_ref[...]
        kswap = pltpu.roll(kall, HEAD_DIM, 1)
        vswap = pltpu.roll(vall, HEAD_DIM, 1)
        for kvh in range(N_KV_HEADS):
            keep = lowk if kvh == 0 else jnp.logical_not(lowk)
            k2 = jnp.where(keep, kall, kswap)
            v2 = jnp.where(keep, vall, vswap)
            qs = qs_ref[kvh * n_rows:(kvh + 1) * n_rows, :]
            s = jnp.where(mask, _dot_nt(qs, k2, HIGHEST), NEG_INF)
            sink = jnp.zeros((n_rows, 1), F32)
            for hh in range(GQA_GROUP):
                sink = jnp.where(head_of_row == hh, sink_ref[kvh * GQA_GROUP + hh], sink)
            m = jnp.maximum(jnp.max(s, axis=-1, keepdims=True), sink)
            e = jnp.exp(s - m)
            den = jnp.sum(e, axis=-1, keepdims=True) + jnp.exp(sink - m)
            o = _dot(e, v2, HIGHEST) / den
            for pr in range(GQA_GROUP // 2):
                o_lo = o[(2 * pr) * rows_per_head:(2 * pr) * rows_per_head + t_new]
                o_hi = o[(2 * pr + 1) * rows_per_head:(2 * pr + 1) * rows_per_head + t_new]
                c0 = (kvh * (GQA_GROUP // 2) + pr) * LANES
                cat_ref[b, :, c0:c0 + LANES] = jnp.where(low4, o_lo, o_hi)
        for pp in range(N_RET_HEADS // 2):
            c0 = q_cols + pp * LANES
            qp = p_ref[b][:, c0:c0 + LANES]
            kp = p_ref[b][:, c0 + r_cols:c0 + r_cols + LANES]
            vp = p_ref[b][:, c0 + 2 * r_cols:c0 + 2 * r_cols + LANES]
            gp = p_ref[b][:, c0 + 3 * r_cols:c0 + 3 * r_cols + LANES]
            r0 = ret_ref[b, pp]
            q_lo = jnp.where(low4, qp, 0.0)
            q_hi = jnp.where(low4, 0.0, qp)
            sc_lo = _dot_nt(q_lo, kp, HIGHEST) * d4_ref[2 * pp]
            sc_hi = _dot_nt(q_hi, kp, HIGHEST) * d4_ref[2 * pp + 1]
            o_inner = jnp.where(low4, _dot(sc_lo, vp, HIGHEST), _dot(sc_hi, vp, HIGHEST))
            qdec = qdec_ref[:, pp * LANES:(pp + 1) * LANES]
            oc_lo = _dot(q_lo * qdec, r0, HIGHEST)
            oc_hi = _dot(q_hi * qdec, r0, HIGHEST)
            o_cross = jnp.concatenate([oc_lo, oc_hi], axis=1)
            kd = kp * kdec_ref[:, pp * LANES:(pp + 1) * LANES]
            kv_full = _dot_tn(kd, vp, HIGHEST)
            kv_pair = jnp.concatenate([kv_full[0:HEAD_DIM, 0:HEAD_DIM], kv_full[HEAD_DIM:, HEAD_DIM:]], axis=0)
            rnew_ref[b, pp] = cdec_ref[pp] * r0 + kv_pair
            y = _group_norm_pair(o_inner + o_cross, avg_ref[...])
            y = y * gnw_ref[:, pp * LANES:(pp + 1) * LANES] * _silu(gp)
            cat_ref[b, :, q_cols + pp * LANES:q_cols + (pp + 1) * LANES] = y

    def per_pair(g, carry):
        for u in range(2):
            per_batch(2 * g + u, ka2_ref.at[u], va2_ref.at[u], qs2_ref.at[u])
        return carry

    lax.fori_loop(0, kbuf_ref.shape[0] // 2, per_pair, 0)


def _sample_mixer(p_s, kv_s, k_buf, v_buf, ret0, sinks, gn_w, ret_tabs):
    nb, t_new, p_cols = p_s.shape
    kv_cols = kv_s.shape[2]
    w = k_buf.shape[1]
    bb = SAMPLE_BATCH_TILE
    n_pairs = N_RET_HEADS // 2
    d_mix = (N_ATTN_HEADS + N_RET_HEADS) * HEAD_DIM
    decay_mask, q_dec, k_dec, chunk_decay = ret_tabs
    n_keys = ((w + t_new + SUBLANES - 1) // SUBLANES) * SUBLANES
    cdec = jnp.stack([jnp.broadcast_to(chunk_decay[0, pp * LANES:(pp + 1) * LANES][:, None], (LANES, HEAD_DIM))
                      for pp in range(n_pairs)])
    const1 = lambda shape: pl.BlockSpec(shape, lambda i: (0,) * len(shape))
    body = functools.partial(_sample_mixer_body, t_new)
    return pl.pallas_call(
        body,
        grid=(nb // bb,),
        in_specs=[
            pl.BlockSpec(memory_space=pltpu.SMEM),
            pl.BlockSpec((bb, t_new, p_cols), lambda i: (i, 0, 0)),
            pl.BlockSpec((bb, t_new, kv_cols), lambda i: (i, 0, 0)),
            pl.BlockSpec((bb, w, LANES), lambda i: (i, 0, 0)),
            pl.BlockSpec((bb, w, LANES), lambda i: (i, 0, 0)),
            pl.BlockSpec((bb, n_pairs, LANES, HEAD_DIM), lambda i: (i, 0, 0, 0)),
            const1((N_RET_HEADS, t_new, t_new)),
            const1((t_new, N_RET_HEADS * HEAD_DIM)),
            const1((t_new, N_RET_HEADS * HEAD_DIM)),
            const1((n_pairs, LANES, HEAD_DIM)),
            const1((LANES, LANES)),
            const1((1, N_RET_HEADS * HEAD_DIM)),
        ],
        out_specs=[
            pl.BlockSpec((bb, t_new, d_mix), lambda i: (i, 0, 0)),
            pl.BlockSpec((bb, w, LANES), lambda i: (i, 0, 0)),
            pl.BlockSpec((bb, w, LANES), lambda i: (i, 0, 0)),
            pl.BlockSpec((bb, n_pairs, LANES, HEAD_DIM), lambda i: (i, 0, 0, 0)),
        ],
        out_shape=[
            jax.ShapeDtypeStruct((nb, t_new, d_mix), F32),
            jax.ShapeDtypeStruct((nb, w, LANES), F32),
            jax.ShapeDtypeStruct((nb, w, LANES), F32),
            jax.ShapeDtypeStruct((nb, n_pairs, LANES, HEAD_DIM), F32),
        ],
        scratch_shapes=[
            pltpu.VMEM((2, n_keys, LANES), F32),
            pltpu.VMEM((2, n_keys, LANES), F32),
            pltpu.VMEM((2, N_ATTN_HEADS * SUBLANES, LANES), F32),
        ],
        compiler_params=_cparams(("arbitrary",)),
        name="sample_mixer",
    )(sinks, p_s, kv_s, k_buf, v_buf, ret0, decay_mask, q_dec, k_dec, cdec, _pair_average_matrix(), gn_w.reshape(1, -1))


def _router_body(geo, xp_ref, xs_ref, catp_ref, cats_ref, wout_ref, g1p_ref, g1s_ref, nw_ref, shp_ref, scp_ref,
                 shs_ref, scs_ref, wr_ref, br_ref, tri_ref, x1_ref, h2_ref, idx_ref, gate_ref, rank_ref, cnt_ref):
    i = pl.program_id(0)
    tv = functools.partial(_tile_vec, i, geo)
    cat = _tile_rows(i, geo, catp_ref, cats_ref)
    x1 = _tile_rows(i, geo, xp_ref, xs_ref) + tv(g1p_ref, g1s_ref) * _dot(cat, wout_ref[...])
    x1_ref[...] = x1
    h2 = _rms_norm(x1, nw_ref[...]) * (1.0 + tv(scp_ref, scs_ref)) + tv(shp_ref, shs_ref)
    h2_ref[...] = h2.reshape(h2_ref.shape)
    logits = _dot_nt(wr_ref[...], h2, HIGHEST) + br_ref[...]
    e_iota = lax.broadcasted_iota(jnp.int32, logits.shape, 0)
    vals, sels = [], []
    for _ in range(TOP_K):
        m = jnp.max(logits, axis=0, keepdims=True)
        sel = jnp.min(jnp.where(logits == m, e_iota, N_EXPERTS), axis=0, keepdims=True)
        vals.append(m)
        sels.append(sel)
        logits = jnp.where(e_iota == sel, -jnp.inf, logits)
    es = [jnp.exp(v - vals[0]) for v in vals]
    den = es[0]
    for e in es[1:]:
        den = den + e

    @pl.when(i == 0)
    def _():
        cnt_ref[...] = jnp.zeros_like(cnt_ref)

    base = cnt_ref[...]
    for k in range(TOP_K):
        hit = e_iota == sels[k]
        onehot = jnp.where(hit, 1.0, 0.0)
        before = _dot(onehot.astype(BF16), tri_ref[...])
        rank = jnp.sum(jnp.where(hit, base + before, 0.0), axis=0, keepdims=True)
        base = base + jnp.sum(onehot, axis=1, keepdims=True)
        idx_ref[k:k + 1, :] = sels[k]
        gate_ref[k:k + 1, :] = es[k] / den
        rank_ref[k:k + 1, :] = rank.astype(jnp.int32)
    cnt_ref[...] = base


def _outproj_router(x_p, x_s, cat_p, cat_s, w_out_bf16, norm_w, mod_p, mod_s, layer, w_router_t, b_router, geo):
    n_p, d = x_p.shape
    n = n_p + x_s.shape[0]
    d_mix = cat_p.shape[1]
    pv = lambda chunk: _prompt_vec_spec(layer, chunk, d)
    sv = lambda chunk: _sample_vec_spec(layer, chunk, geo, d)
    tile = lambda cols: pl.BlockSpec((TOKEN_TILE, cols), lambda i: (i, 0))
    choice = pl.BlockSpec((TOP_K, TOKEN_TILE), lambda i: (0, i))
    tri = jnp.asarray(np.triu(np.ones((TOKEN_TILE, TOKEN_TILE), np.float32), 1)).astype(BF16)
    return pl.pallas_call(
        functools.partial(_router_body, geo),
        grid=(n // TOKEN_TILE,),
        in_specs=[
            _prompt_rows_spec(geo, d), _sample_rows_spec(geo, d),
            _prompt_rows_spec(geo, d_mix), _sample_rows_spec(geo, d_mix),
            pl.BlockSpec((d_mix, d), lambda i: (0, 0)),
            pv(2), sv(2),
            pl.BlockSpec((1, d), lambda i: (0, 0)),
            pv(3), pv(4), sv(3), sv(4),
            pl.BlockSpec((N_EXPERTS, d), lambda i: (0, 0)),
            pl.BlockSpec((N_EXPERTS, 1), lambda i: (0, 0)),
            pl.BlockSpec((TOKEN_TILE, TOKEN_TILE), lambda i: (0, 0)),
        ],
        out_specs=[
            tile(d),
            pl.BlockSpec((TOKEN_TILE, 1, d), lambda i: (i, 0, 0)),
            choice, choice, choice,
            pl.BlockSpec((N_EXPERTS, 1), lambda i: (0, 0)),
        ],
        out_shape=[
            jax.ShapeDtypeStruct((n, d), F32),
            jax.ShapeDtypeStruct((n, 1, d), F32),
            jax.ShapeDtypeStruct((TOP_K, n), jnp.int32),
            jax.ShapeDtypeStruct((TOP_K, n), F32),
            jax.ShapeDtypeStruct((TOP_K, n), jnp.int32),
            jax.ShapeDtypeStruct((N_EXPERTS, 1), F32),
        ],
        compiler_params=_cparams(("arbitrary",)),
        name="outproj_router",
    )(x_p, x_s, cat_p, cat_s, w_out_bf16, mod_p, mod_s, norm_w.reshape(1, d), mod_p, mod_p, mod_s, mod_s,
      w_router_t, b_router.reshape(N_EXPERTS, 1), tri)


def _routing_plan(top_idx, rank, counts_f, tm):
    k, n = top_idx.shape
    n_pairs = k * n
    n_tiles = (n_pairs + N_EXPERTS * (tm - 1)) // tm + 1
    counts = counts_f[:, 0].astype(jnp.int32)
    tiles_per = (counts + tm - 1) // tm
    tile_end = jnp.cumsum(tiles_per)
    tile_start = tile_end - tiles_per
    n_used = tile_end[-1]
    n_slots = n_tiles * tm

    def lookup(table, index):
        ids = jnp.arange(table.shape[0], dtype=jnp.int32).reshape((-1,) + (1,) * index.ndim)
        return jnp.sum(jnp.where(index[None] == ids, table.reshape(ids.shape), 0), axis=0)

    slot_of_pair = lookup(tile_start * tm, top_idx) + rank
    pad_counts = jnp.concatenate([tiles_per * tm - counts, (n_slots - n_used * tm).reshape(1)])
    pad_end = jnp.cumsum(pad_counts)
    first_pad_slot = jnp.concatenate([tile_start * tm + counts, (n_used * tm).reshape(1)])
    k_pad = jnp.arange(n_slots - n_pairs, dtype=jnp.int32)
    seg = jnp.sum(k_pad[:, None] >= pad_end[None, :], axis=1).astype(jnp.int32)
    pad_slot = k_pad + lookup(first_pad_slot - (pad_end - pad_counts), seg)
    token_of_pair = jnp.broadcast_to(jnp.arange(n, dtype=jnp.int32)[None, :], (k, n))
    keys = jnp.concatenate([slot_of_pair.reshape(-1), pad_slot])
    vals = jnp.concatenate([token_of_pair.reshape(-1), jnp.zeros_like(pad_slot)])
    _, slot_rows = lax.sort((keys, vals), num_keys=1)
    tile_ids = jnp.arange(n_tiles, dtype=jnp.int32)
    tile_expert = jnp.minimum(jnp.sum(tile_ids[:, None] >= tile_end[None, :], axis=1), N_EXPERTS - 1).astype(jnp.int32)
    last_expert = tile_expert[jnp.maximum(n_used - 1, 0)]
    tile_expert = jnp.where(tile_ids < n_used, tile_expert, last_expert)
    return tile_expert, n_used.reshape(1).astype(jnp.int32), slot_rows.reshape(n_tiles, 1, tm), slot_of_pair


def _issue_row_copies(src_hbm, row_of, dst_row, sem, n_rows):
    def group(g, c):
        for u in range(DMA_UNROLL):
            r = g * DMA_UNROLL + u
            pltpu.make_async_copy(src_hbm.at[row_of(r)], dst_row(r), sem).start(priority=u % 2)
        return c
    lax.fori_loop(0, n_rows // DMA_UNROLL, group, 0)


def _issue_row_copies_static(src_hbm, row_of, dst_row, sem, lo, hi):
    for r in range(lo, hi):
        pltpu.make_async_copy(src_hbm.at[row_of(r)], dst_row(r), sem).start(priority=r % 2)


def _wait_row_copies(src_hbm, dst_buf, sem):
    pltpu.make_async_copy(src_hbm.at[pl.ds(0, dst_buf.shape[0])], dst_buf, sem).wait()


def _expert_body(te_ref, nt_ref, rows0_ref, rows1_ref, rows2_ref, h_hbm, wgu_ref, bg_ref, bu_ref, wd_ref, bd_ref,
                 perm_ref, y_ref, xbuf, sems, wgu_s, wd_s, act_s, x_s):
    j = pl.program_id(0)
    n_tiles = pl.num_programs(0)
    n_used = nt_ref[0]
    slot = j % ROW_BUFFERS
    far = (j + ROW_BUFFERS - 1) % ROW_BUFFERS
    tm = EXPERT_TILE
    f2 = wgu_ref.shape[2]
    n_chunks = f2 // MXU_WIDTH
    half = MXU_WIDTH // 2
    next_row = lambda r: rows2_ref[0, 0, r]
    next_dst = lambda r: xbuf.at[far, r]

    @pl.when(j == 0)
    def _():
        _issue_row_copies(h_hbm, lambda r: rows0_ref[0, 0, r], lambda r: xbuf.at[0, r], sems.at[0], tm)
        _issue_row_copies(h_hbm, lambda r: rows1_ref[0, 0, r], lambda r: xbuf.at[1, r], sems.at[1], tm)

    new_expert = jnp.logical_or(j == 0, te_ref[j] != te_ref[jnp.maximum(j - 1, 0)])

    @pl.when(jnp.logical_and(j < n_used, new_expert))
    def _():
        for c in range(n_chunks):
            cols = slice(c * MXU_WIDTH, (c + 1) * MXU_WIDTH)
            wgu_s[:, cols] = _dot(wgu_ref[0, :, cols].astype(BF16), perm_ref[...]).astype(BF16)
        wd_s[...] = wd_ref[0].astype(BF16)

    @pl.when(j < n_used)
    def _():
        _wait_row_copies(h_hbm, xbuf.at[slot], sems.at[slot])
        x_s[...] = xbuf[slot][:, 0, :]
        x = x_s[...].astype(BF16)
        issue_chunks = n_chunks // 2
        per_chunk = tm // issue_chunks
        for c in range(n_chunks):
            if c < issue_chunks:
                _issue_row_copies_static(h_hbm, next_row, next_dst, sems.at[far], c * per_chunk, (c + 1) * per_chunk)
            gu = _dot(x, wgu_s[:, c * MXU_WIDTH:(c + 1) * MXU_WIDTH])
            glu = jnp.minimum(gu[:, :half] + bg_ref[0, :, c * half:(c + 1) * half], SWIGLU_LIMIT)
            lin = jnp.clip(gu[:, half:] + bu_ref[0, :, c * half:(c + 1) * half], -SWIGLU_LIMIT, SWIGLU_LIMIT)
            act_s[:, c * half:(c + 1) * half] = (glu * jax.nn.sigmoid(SWIGLU_ALPHA * glu) * (lin + 1.0)).astype(BF16)
        y = _dot(act_s[...], wd_s[...]) + bd_ref[0]
        y_ref[...] = y.reshape(y_ref.shape)

    @pl.when(j >= n_used)
    def _():
        _wait_row_copies(h_hbm, xbuf.at[slot], sems.at[slot])
        _issue_row_copies(h_hbm, next_row, next_dst, sems.at[far], tm)
        y_ref[...] = jnp.zeros_like(y_ref)

    @pl.when(j == n_tiles - 1)
    def _():
        for ahead in range(1, ROW_BUFFERS):
            late = (j + ahead) % ROW_BUFFERS
            _wait_row_copies(h_hbm, xbuf.at[late], sems.at[late])


def _deinterleave_matrix():
    m = np.zeros((MXU_WIDTH, MXU_WIDTH), np.float32)
    j = np.arange(MXU_WIDTH // 2)
    m[2 * j, j] = 1.0
    m[2 * j + 1, MXU_WIDTH // 2 + j] = 1.0
    return jnp.asarray(m).astype(BF16)


def _routed_experts(h2_rows, tile_expert, n_used, slot_rows, w_gate_up, bg, bu, w_down, bd, layer):
    n, _, d = h2_rows.shape
    f2 = w_gate_up.shape[3]
    f = f2 // 2
    tm = EXPERT_TILE
    n_tiles = slot_rows.shape[0]
    assert ROW_BUFFERS == 3 and n_tiles >= ROW_BUFFERS
    by_expert = lambda shape: pl.BlockSpec(shape, lambda j, te, nt: (te[j], 0, 0))
    layer_expert = lambda shape: pl.BlockSpec((None,) + shape, lambda j, te, nt: (layer, te[j], 0, 0))
    rows_ahead = lambda k: pl.BlockSpec((1, 1, tm), lambda j, te, nt: (jnp.minimum(j + k, n_tiles - 1), 0, 0),
                                        memory_space=pltpu.SMEM)
    grid_spec = pltpu.PrefetchScalarGridSpec(
        num_scalar_prefetch=2,
        grid=(n_tiles,),
        in_specs=[
            rows_ahead(0), rows_ahead(1), rows_ahead(2),
            pl.BlockSpec(memory_space=pl.ANY),
            layer_expert((1, d, f2)),
            by_expert((1, 1, f)), by_expert((1, 1, f)),
            layer_expert((1, f, d)),
            by_expert((1, 1, d)),
            pl.BlockSpec((MXU_WIDTH, MXU_WIDTH), lambda j, te, nt: (0, 0)),
        ],
        out_specs=pl.BlockSpec((tm, 1, d), lambda j, te, nt: (j, 0, 0)),
        scratch_shapes=[
            pltpu.VMEM((ROW_BUFFERS, tm, 1, d), F32),
            pltpu.SemaphoreType.DMA((ROW_BUFFERS,)),
            pltpu.VMEM((d, f2), BF16),
            pltpu.VMEM((f, d), BF16),
            pltpu.VMEM((tm, f), BF16),
            pltpu.VMEM((tm, d), F32),
        ],
    )
    return pl.pallas_call(
        _expert_body,
        grid_spec=grid_spec,
        out_shape=jax.ShapeDtypeStruct((n_tiles * tm, 1, d), F32),
        compiler_params=_cparams(("arbitrary",)),
        name="routed_experts",
    )(tile_expert, n_used, slot_rows, slot_rows, slot_rows, h2_rows, w_gate_up, bg, bu, w_down, bd,
      _deinterleave_matrix())


def _combine_body(geo, final_norm, slots_cur_ref, slots_nxt_ref, x1_ref, gate_ref, g2p_ref, g2s_ref, nfw_ref, y_hbm,
                  outp_ref, outs_ref, ybuf, sems):
    i = pl.program_id(0)
    n_steps = pl.num_programs(0)
    slot = i % 2
    tm = TOKEN_TILE

    def issue(rows_ref, sl):
        for k in range(TOP_K):
            _issue_row_copies(y_hbm, lambda r, k=k: rows_ref[0, k, r], lambda r, k=k: ybuf.at[sl, k, r], sems.at[sl], tm)

    @pl.when(i == 0)
    def _():
        issue(slots_cur_ref, 0)

    @pl.when(i + 1 < n_steps)
    def _():
        issue(slots_nxt_ref, 1 - slot)

    for k in range(TOP_K):
        _wait_row_copies(y_hbm, ybuf.at[slot, k], sems.at[slot])
    acc = ybuf[slot, 0][:, 0, :] * gate_ref[:, 0:1]
    for k in range(1, TOP_K):
        acc = acc + ybuf[slot, k][:, 0, :] * gate_ref[:, k:k + 1]
    x2 = x1_ref[...] + _tile_vec(i, geo, g2p_ref, g2s_ref) * acc
    if final_norm:
        x2 = _rms_norm(x2, nfw_ref[...])

    @pl.when(i < geo.n_ptiles)
    def _():
        outp_ref[...] = x2

    @pl.when(i >= geo.n_ptiles)
    def _():
        outs_ref[...] = x2


def _combine(x1, y_rows, slot_of_pair, gates_t, mod_p, mod_s, layer, final_w, geo, final_norm):
    n, d = x1.shape
    tm = TOKEN_TILE
    n_steps = n // tm
    n_p = geo.n_ptiles * tm
    slots = slot_of_pair.reshape(TOP_K, n_steps, tm).transpose(1, 0, 2)
    return pl.pallas_call(
        functools.partial(_combine_body, geo, final_norm),
        grid=(n_steps,),
        in_specs=[
            pl.BlockSpec((1, TOP_K, tm), lambda i: (i, 0, 0), memory_space=pltpu.SMEM),
            pl.BlockSpec((1, TOP_K, tm), lambda i: (jnp.minimum(i + 1, n_steps - 1), 0, 0), memory_space=pltpu.SMEM),
            pl.BlockSpec((tm, d), lambda i: (i, 0)),
            pl.BlockSpec((tm, TOP_K), lambda i: (i, 0)),
            _prompt_vec_spec(layer, 5, d),
            _sample_vec_spec(layer, 5, geo, d),
            pl.BlockSpec((1, d), lambda i: (0, 0)),
            pl.BlockSpec(memory_space=pl.ANY),
        ],
        out_specs=[_prompt_rows_spec(geo, d), _sample_rows_spec(geo, d)],
        out_shape=[jax.ShapeDtypeStruct((n_p, d), F32), jax.ShapeDtypeStruct((n - n_p, d), F32)],
        scratch_shapes=[pltpu.VMEM((2, TOP_K, tm, 1, d), F32), pltpu.SemaphoreType.DMA((2,))],
        compiler_params=_cparams(("arbitrary",)),
        name="moe_combine",
    )(slots, slots, x1, gates_t, mod_p, mod_s, final_w.reshape(1, d), y_rows)


def kernel(x_prompt, x_sample, state_swa_k, state_swa_v, state_ret, c_prompt, c_sample, norm_mix_w, w_ada, b_ada, w_in, attn_sinks, ret_gn_w, w_out, norm_ffn_w, w_router, b_router, w_gate_up, b_gate_up, w_down, b_down, norm_final_w):
    n_pbatch, t_prompt, d = x_prompt.shape
    n_sbatch, t_sample, _ = x_sample.shape
    depth = w_in.shape[0]
    n_p = n_pbatch * t_prompt
    n_s = n_sbatch * t_sample
    win = state_swa_k.shape[2]
    assert d == (N_ATTN_HEADS + N_RET_HEADS) * HEAD_DIM and w_gate_up.shape[1] == N_EXPERTS
    assert t_prompt % TOKEN_TILE == 0 and n_s % TOKEN_TILE == 0 and TOKEN_TILE % t_sample == 0
    assert t_prompt % WINDOW == 0 and RET_CHUNK == WINDOW and win == WINDOW and t_sample <= SUBLANES
    assert n_sbatch % SAMPLE_BATCH_TILE == 0 and n_pbatch <= SUBLANES
    assert w_gate_up.shape[3] % MXU_WIDTH == 0 and EXPERT_TILE % DMA_UNROLL == 0 and TOKEN_TILE % DMA_UNROLL == 0
    geo = _Geometry(n_p // TOKEN_TILE, t_prompt // TOKEN_TILE, n_pbatch)

    c_all = jnp.concatenate([c_prompt, jnp.zeros((SUBLANES - n_pbatch, d), F32), c_sample], axis=0)
    mod = _ada_modulation(c_all, w_ada, b_ada)
    mod_p = mod[:, :SUBLANES]
    mod_s = jnp.repeat(mod[:, SUBLANES:], t_sample, axis=1)

    rope_tab = _rope_tables(t_prompt, t_sample)
    ret_tabs_p = _retention_tables(RET_CHUNK)
    ret_tabs_s = _retention_tables(t_sample)
    f = w_gate_up.shape[3] // 2

    x_p = x_prompt.reshape(n_p, d)
    x_s = x_sample.reshape(n_s, d)
    kp_l, vp_l, rp_l, ks_l, vs_l, rs_l = [], [], [], [], [], []
    for l in range(depth):
        p_all, kv_all = _input_projection(x_p, x_s, norm_mix_w[l], mod_p, mod_s, l, w_in[l].astype(BF16), rope_tab, geo)
        cat_p, r_pairs = _prompt_mixer(p_all, kv_all, attn_sinks[l], ret_gn_w[l], n_pbatch, t_prompt, ret_tabs_p)
        p_s = p_all[n_p:].astype(F32)
        kv_s = kv_all[n_p:]
        kv_p = kv_all[:n_p]
        cat_s, k_new, v_new, r_new = _sample_mixer(
            p_s.reshape(n_sbatch, t_sample, -1), kv_s.reshape(n_sbatch, t_sample, -1),
            state_swa_k[l].reshape(n_sbatch, win, LANES), state_swa_v[l].reshape(n_sbatch, win, LANES),
            state_ret[l].reshape(n_sbatch, N_RET_HEADS // 2, LANES, HEAD_DIM), attn_sinks[l], ret_gn_w[l], ret_tabs_s)
        x1, h2_rows, top_idx, gates, rank, counts = _outproj_router(
            x_p, x_s, cat_p, cat_s.reshape(n_s, d), w_out[l].astype(BF16), norm_ffn_w[l], mod_p, mod_s, l,
            w_router[l].T, b_router[l], geo)
        tile_expert, n_used, slot_rows, slot_of_pair = _routing_plan(top_idx, rank, counts, EXPERT_TILE)
        b_gu = b_gate_up[l].reshape(N_EXPERTS, 1, f, 2)
        y_rows = _routed_experts(h2_rows, tile_expert, n_used, slot_rows, w_gate_up,
                                 b_gu[..., 0], b_gu[..., 1], w_down, b_down[l][:, None, :], l)
        x_p, x_s = _combine(x1, y_rows, slot_of_pair, gates.T, mod_p, mod_s, l, norm_final_w, geo,
                            final_norm=(l == depth - 1))

        kv_last = kv_p.reshape(n_pbatch, t_prompt, -1)[:, t_prompt - WINDOW:]
        kv_last = kv_last.reshape(n_pbatch, WINDOW, 2, N_KV_HEADS, HEAD_DIM)
        kp_l.append(kv_last[:, :, 0])
        vp_l.append(kv_last[:, :, 1])
        rp_l.append(jnp.stack([r_pairs[:, :, :HEAD_DIM, :HEAD_DIM], r_pairs[:, :, HEAD_DIM:, HEAD_DIM:]], axis=2)
                    .reshape(n_pbatch, N_RET_HEADS, HEAD_DIM, HEAD_DIM))
        ks_l.append(k_new.reshape(n_sbatch, win, N_KV_HEADS, HEAD_DIM))
        vs_l.append(v_new.reshape(n_sbatch, win, N_KV_HEADS, HEAD_DIM))
        rs_l.append(r_new.reshape(n_sbatch, N_RET_HEADS, HEAD_DIM, HEAD_DIM))
    return (x_p.reshape(n_pbatch, t_prompt, d), x_s.reshape(n_sbatch, t_sample, d), jnp.stack(kp_l), jnp.stack(vp_l),
            jnp.stack(rp_l), jnp.stack(ks_l), jnp.stack(vs_l), jnp.stack(rs_l))
```

```python
import functools

import numpy as np
import jax
import jax.numpy as jnp
from jax import lax
from jax.experimental import pallas as pl
from jax.experimental.pallas import tpu as pltpu

F32 = jnp.float32
BF16 = jnp.bfloat16
HIGHEST = lax.Precision.HIGHEST

HEAD_DIM = 64
N_ATTN_HEADS = 8
N_KV_HEADS = 2
GQA_GROUP = N_ATTN_HEADS // N_KV_HEADS
WINDOW = 128
ROPE_THETA = 500000.0
ROPE_DIMS = HEAD_DIM // 4
N_RET_HEADS = 8
RET_CHUNK = 128
RET_THETA = 10000.0
N_EXPERTS = 32
TOP_K = 4
SWIGLU_LIMIT = 7.0
SWIGLU_ALPHA = 1.702
NORM_EPS = 1e-5
PAST_LEN = 16384

LANES = 128
SUBLANES = 8
MXU_WIDTH = 256
VMEM_LIMIT_BYTES = 56 * 1024 * 1024

TOKEN_TILE = 256
EXPERT_TILE = 256
SAMPLE_BATCH_TILE = 8
SAMPLE_GROUP = 4
DMA_UNROLL = 8
ROW_BUFFERS = 3

NEG_INF = -1e30


def _cparams(semantics):
    return pltpu.CompilerParams(dimension_semantics=semantics, vmem_limit_bytes=VMEM_LIMIT_BYTES)


def _lane_is_low_half(shape):
    return lax.broadcasted_iota(jnp.int32, shape, len(shape) - 1) < HEAD_DIM


def _dot(a, b, precision=None):
    return jnp.dot(a, b, preferred_element_type=F32, precision=precision)


def _dot_nt(a, b, precision=None):
    return lax.dot_general(a, b, (((1,), (1,)), ((), ())), preferred_element_type=F32, precision=precision)


def _dot_tn(a, b, precision=None):
    return lax.dot_general(a, b, (((0,), (0,)), ((), ())), preferred_element_type=F32, precision=precision)


def _rms_norm(x, w):
    return x * lax.rsqrt(jnp.mean(x * x, axis=-1, keepdims=True) + NORM_EPS) * w


def _silu(x):
    return x * jax.nn.sigmoid(x)


def _ada_body(c_ref, w_ref, b_ref, o_ref):
    a = _silu(c_ref[...])
    o_ref[0] = _dot(a, w_ref[0], HIGHEST) + b_ref[0]


def _ada_modulation(c_all, w_ada, b_ada):
    depth, d, cols = w_ada.shape
    rows = c_all.shape[0]
    tn = 1024
    return pl.pallas_call(
        _ada_body,
        grid=(depth, cols // tn),
        in_specs=[
            pl.BlockSpec((rows, d), lambda l, j: (0, 0)),
            pl.BlockSpec((1, d, tn), lambda l, j: (l, 0, j)),
            pl.BlockSpec((1, 1, tn), lambda l, j: (l, 0, j)),
        ],
        out_specs=pl.BlockSpec((1, rows, tn), lambda l, j: (l, 0, j)),
        out_shape=jax.ShapeDtypeStruct((depth, rows, cols), F32),
        compiler_params=_cparams(("arbitrary", "arbitrary")),
        name="ada_modulation",
    )(c_all, w_ada, b_ada.reshape(depth, 1, cols))


class _Geometry:
    def __init__(self, n_ptiles, tiles_per_batch, n_pbatch):
        self.n_ptiles = n_ptiles
        self.tiles_per_batch = tiles_per_batch
        self.n_pbatch = n_pbatch


def _tile_vec(i, geo, prompt_ref, sample_ref):
    b = jnp.minimum(i // geo.tiles_per_batch, geo.n_pbatch - 1)
    return jnp.where(i >= geo.n_ptiles, sample_ref[...], prompt_ref[pl.ds(b, 1), :])


def _tile_rows(i, geo, prompt_ref, sample_ref):
    return jnp.where(i >= geo.n_ptiles, sample_ref[...].astype(prompt_ref.dtype), prompt_ref[...])


def _prompt_vec_spec(layer, chunk, d):
    return pl.BlockSpec((None, SUBLANES, d), lambda i: (layer, 0, chunk))


def _sample_vec_spec(layer, chunk, geo, d):
    return pl.BlockSpec((None, TOKEN_TILE, d), lambda i: (layer, jnp.maximum(i - geo.n_ptiles, 0), chunk))


def _prompt_rows_spec(geo, cols):
    return pl.BlockSpec((TOKEN_TILE, cols), lambda i: (jnp.minimum(i, geo.n_ptiles - 1), 0))


def _sample_rows_spec(geo, cols):
    return pl.BlockSpec((TOKEN_TILE, cols), lambda i: (jnp.maximum(i - geo.n_ptiles, 0), 0))


def _rotate(xc, tab_ref, base, shift):
    return (xc * tab_ref[base] + pltpu.roll(xc, shift, 1) * tab_ref[base + 1]
            + pltpu.roll(xc, LANES - shift, 1) * tab_ref[base + 2])


def _inproj_body(geo, xp_ref, xs_ref, nw_ref, shp_ref, scp_ref, shs_ref, scs_ref, w_ref, tab_ref, p_ref, kv_ref):
    i = pl.program_id(0)
    x = _tile_rows(i, geo, xp_ref, xs_ref)
    h = _rms_norm(x, nw_ref[...]) * (1.0 + _tile_vec(i, geo, scp_ref, scs_ref)) + _tile_vec(i, geo, shp_ref, shs_ref)
    proj = _dot(h.astype(BF16), w_ref[...])
    q_cols = N_ATTN_HEADS * HEAD_DIM
    kv_cols = N_KV_HEADS * HEAD_DIM
    r_cols = N_RET_HEADS * HEAD_DIM
    o_ka = q_cols
    o_va = o_ka + kv_cols
    o_qr = o_va + kv_cols
    o_kr = o_qr + r_cols
    o_vr = o_kr + r_cols
    o_g = o_vr + r_cols
    attn_scale = HEAD_DIM ** -0.5
    ret_scale = HEAD_DIM ** -0.5
    half_a = ROPE_DIMS // 2
    half_r = HEAD_DIM // 2

    def emit(p_ref, kv_ref):
        dt = p_ref.dtype
        for c in range(q_cols // LANES):
            xc = proj[:, c * LANES:(c + 1) * LANES]
            p_ref[:, c * LANES:(c + 1) * LANES] = (_rotate(xc, tab_ref, 0, half_a) * attn_scale).astype(dt)
        kv_ref[:, 0:kv_cols] = _rotate(proj[:, o_ka:o_ka + kv_cols], tab_ref, 0, half_a)
        kv_ref[:, kv_cols:2 * kv_cols] = proj[:, o_va:o_va + kv_cols]
        for c in range(r_cols // LANES):
            xq = proj[:, o_qr + c * LANES:o_qr + (c + 1) * LANES]
            xk = proj[:, o_kr + c * LANES:o_kr + (c + 1) * LANES]
            p_ref[:, q_cols + c * LANES:q_cols + (c + 1) * LANES] = _rotate(xq, tab_ref, 3, half_r).astype(dt)
            p_ref[:, q_cols + r_cols + c * LANES:q_cols + r_cols + (c + 1) * LANES] = (
                _rotate(xk, tab_ref, 3, half_r) * ret_scale).astype(dt)
        p_ref[:, q_cols + 2 * r_cols:q_cols + 3 * r_cols] = proj[:, o_vr:o_vr + r_cols].astype(dt)
        p_ref[:, q_cols + 3 * r_cols:q_cols + 4 * r_cols] = proj[:, o_g:o_g + r_cols].astype(dt)

    emit(p_ref, kv_ref)


def _input_projection(x_p, x_s, norm_w, mod_p, mod_s, layer, w_in_bf16, rope_tab, geo):
    n_p, d = x_p.shape
    n_s = x_s.shape[0]
    in_cols = w_in_bf16.shape[1]
    kv_cols = 2 * N_KV_HEADS * HEAD_DIM
    p_cols = in_cols - kv_cols
    tpb = geo.tiles_per_batch
    tab_index = lambda i: (0, jnp.where(i >= geo.n_ptiles, tpb, i % tpb), 0)
    return pl.pallas_call(
        functools.partial(_inproj_body, geo),
        grid=((n_p + n_s) // TOKEN_TILE,),
        in_specs=[
            _prompt_rows_spec(geo, d), _sample_rows_spec(geo, d),
            pl.BlockSpec((1, d), lambda i: (0, 0)),
            _prompt_vec_spec(layer, 0, d), _prompt_vec_spec(layer, 1, d),
            _sample_vec_spec(layer, 0, geo, d), _sample_vec_spec(layer, 1, geo, d),
            pl.BlockSpec((d, in_cols), lambda i: (0, 0)),
            pl.BlockSpec((6, TOKEN_TILE, LANES), tab_index),
        ],
        out_specs=[
            pl.BlockSpec((TOKEN_TILE, p_cols), lambda i: (i, 0)),
            pl.BlockSpec((TOKEN_TILE, kv_cols), lambda i: (i, 0)),
        ],
        out_shape=[
            jax.ShapeDtypeStruct((n_p + n_s, p_cols), BF16),
            jax.ShapeDtypeStruct((n_p + n_s, kv_cols), F32),
        ],
        compiler_params=_cparams(("arbitrary",)),
        name="input_projection",
    )(x_p, x_s, norm_w.reshape(1, d), mod_p, mod_p, mod_s, mod_s, w_in_bf16, rope_tab)


def _rope_tables(t_prompt, t_sample):
    pos = jnp.concatenate([jnp.arange(t_prompt, dtype=jnp.int32),
                           PAST_LEN + (jnp.arange(TOKEN_TILE, dtype=jnp.int32) % t_sample)])
    d = np.arange(LANES) % HEAD_DIM

    def tables(n_dims, theta):
        half = n_dims // 2
        freqs = jnp.power(jnp.float32(theta), -jnp.arange(half, dtype=jnp.float32) / half)
        ang = pos.astype(jnp.float32)[:, None] * freqs[None, :]
        cos, sin = jnp.cos(ang), jnp.sin(ang)
        fidx = np.where(d < n_dims, d % half, 0)
        cos_l = jnp.where(jnp.asarray(d < n_dims)[None, :], cos[:, fidx], 1.0)
        sin_l = sin[:, fidx]
        upper = jnp.asarray((d >= half) & (d < n_dims))[None, :]
        lower = jnp.asarray(d < half)[None, :]
        return [cos_l, jnp.where(upper, sin_l, 0.0), jnp.where(lower, -sin_l, 0.0)]

    return jnp.stack(tables(ROPE_DIMS, ROPE_THETA) + tables(HEAD_DIM, RET_THETA)).astype(F32)


def _retention_tables(c):
    h = N_RET_HEADS
    log_gamma = jnp.log(1.0 - jnp.power(2.0, -5.0 - jnp.arange(h, dtype=jnp.float32)))
    idx = jnp.arange(c, dtype=jnp.float32)
    diff = idx[:, None] - idx[None, :]
    decay_mask = jnp.where(diff >= 0, jnp.exp(log_gamma[:, None, None] * jnp.maximum(diff, 0.0)), 0.0)
    k_dec = jnp.exp(log_gamma[None, :] * (c - 1 - idx)[:, None])
    q_dec = jnp.exp(log_gamma[None, :] * (idx + 1.0)[:, None])
    chunk_decay = jnp.exp(log_gamma * c)
    rep = lambda a: jnp.repeat(a, HEAD_DIM, axis=-1)
    return decay_mask.astype(F32), rep(q_dec).astype(F32), rep(k_dec).astype(F32), rep(chunk_decay[None, :]).astype(F32)


def _group_norm_pairs(items, avg):
    avg_b = avg.astype(BF16)

    def block_means(xs):
        his = [x.astype(BF16) for x in xs]
        los = [(x - hi.astype(F32)).astype(BF16) for x, hi in zip(xs, his)]
        return [_dot(hi, avg_b) + _dot(lo, avg_b) for hi, lo in zip(his, los)]

    dlts = [o - mu for o, mu in zip(items, block_means(items))]
    varis = block_means([d * d for d in dlts])
    return [d * lax.rsqrt(v + NORM_EPS) for d, v in zip(dlts, varis)]


def _group_norm_pair(o, avg):
    return _group_norm_pairs([o], avg)[0]


def _pair_average_matrix():
    r = np.arange(LANES)
    return jnp.asarray(((r[:, None] // HEAD_DIM) == (r[None, :] // HEAD_DIM)).astype(np.float32) / HEAD_DIM)


def _block_diag_mask():
    r = np.arange(LANES)
    return jnp.asarray(((r[:, None] // HEAD_DIM) == (r[None, :] // HEAD_DIM)).astype(np.float32))


def _prompt_mixer_body(sink_ref, p_ref, kvc_ref, kvp_ref, dmask_ref, qdec_ref, kdec_ref, cdec_ref,
                       bd_ref, avg_ref, gnw_ref, cat_ref, r_ref):
    jb = pl.program_id(1)
    blk = WINDOW
    q_cols = N_ATTN_HEADS * HEAD_DIM
    r_cols = N_RET_HEADS * HEAD_DIM
    kvw = N_KV_HEADS * HEAD_DIM

    @pl.when(jb == 0)
    def _():
        r_ref[...] = jnp.zeros_like(r_ref)

    low = _lane_is_low_half((blk, LANES))
    kband = jnp.concatenate([kvp_ref[:, 0:kvw], kvc_ref[:, 0:kvw]], axis=0)
    vband = jnp.concatenate([kvp_ref[:, kvw:2 * kvw], kvc_ref[:, kvw:2 * kvw]], axis=0)
    low2 = _lane_is_low_half((2 * blk, LANES))
    kswap = pltpu.roll(kband, HEAD_DIM, 1)
    vswap = pltpu.roll(vband, HEAD_DIM, 1)
    qi = lax.broadcasted_iota(jnp.int32, (blk, 2 * blk), 0)
    kj = lax.broadcasted_iota(jnp.int32, (blk, 2 * blk), 1)
    dist = blk + qi - kj
    mask = (dist >= 0) & (dist < WINDOW) & ((kj >= blk) | (jb > 0))
    for kvh in range(N_KV_HEADS):
        keep = low2 if kvh == 0 else jnp.logical_not(low2)
        k2 = jnp.where(keep, kband, kswap).astype(BF16)
        v2 = jnp.where(keep, vband, vswap).astype(BF16)
        pieces = []
        for pr in range(GQA_GROUP // 2):
            c0 = (kvh * (GQA_GROUP // 2) + pr) * LANES
            qp = p_ref[:, c0:c0 + LANES]
            pieces.append(jnp.where(low, qp, jnp.zeros_like(qp)))
            pieces.append(jnp.where(low, jnp.zeros_like(qp), qp))
        qs = jnp.concatenate(pieces, axis=0)
        s = _dot_nt(qs, k2)
        es, inv = [], []
        for hh in range(GQA_GROUP):
            sink = sink_ref[kvh * GQA_GROUP + hh]
            sh = jnp.where(mask, s[hh * blk:(hh + 1) * blk], NEG_INF)
            m = jnp.maximum(jnp.max(sh, axis=-1, keepdims=True), sink)
            e = jnp.exp(sh - m)
            den = jnp.sum(e, axis=-1, keepdims=True) + jnp.exp(sink - m)
            es.append(e.astype(BF16))
            inv.append(1.0 / den)
        o = _dot(jnp.concatenate(es, axis=0), v2)
        for pr in range(GQA_GROUP // 2):
            o_lo = o[(2 * pr) * blk:(2 * pr + 1) * blk] * inv[2 * pr]
            o_hi = o[(2 * pr + 1) * blk:(2 * pr + 2) * blk] * inv[2 * pr + 1]
            c0 = (kvh * (GQA_GROUP // 2) + pr) * LANES
            cat_ref[:, c0:c0 + LANES] = jnp.where(low, o_lo, o_hi).astype(BF16)
    n_pairs = N_RET_HEADS // 2
    pairs = range(n_pairs)
    col = lambda pp, part: slice(q_cols + part * r_cols + pp * LANES, q_cols + part * r_cols + (pp + 1) * LANES)
    pcol = lambda pp: slice(pp * LANES, (pp + 1) * LANES)
    qs = [p_ref[:, col(pp, 0)] for pp in pairs]
    ks = [p_ref[:, col(pp, 1)] for pp in pairs]
    vs = [p_ref[:, col(pp, 2)] for pp in pairs]
    r_prev = [r_ref[0, pp] for pp in pairs]
    zero = jnp.zeros_like(qs[0])
    sc_lo = [_dot_nt(jnp.where(low, qs[pp], zero), ks[pp]) for pp in pairs]
    sc_hi = [_dot_nt(jnp.where(low, zero, qs[pp]), ks[pp]) for pp in pairs]
    qd = [(qs[pp].astype(F32) * qdec_ref[:, pcol(pp)]).astype(BF16) for pp in pairs]
    kd = [(ks[pp].astype(F32) * kdec_ref[:, pcol(pp)]).astype(BF16) for pp in pairs]
    o_cross = [_dot(qd[pp], r_prev[pp].astype(BF16)) for pp in pairs]
    kv_new = [_dot_tn(kd[pp], vs[pp]) for pp in pairs]
    o_lo = [_dot((sc_lo[pp] * dmask_ref[2 * pp]).astype(BF16), vs[pp]) for pp in pairs]
    o_hi = [_dot((sc_hi[pp] * dmask_ref[2 * pp + 1]).astype(BF16), vs[pp]) for pp in pairs]
    for pp in pairs:
        r_ref[0, pp] = cdec_ref[pp] * r_prev[pp] + kv_new[pp] * bd_ref[...]
    o_all = jnp.concatenate([jnp.where(low, o_lo[pp], o_hi[pp]) + o_cross[pp] for pp in pairs], axis=0)
    y_all = _group_norm_pair(o_all, avg_ref[...])
    for pp in pairs:
        gp = p_ref[:, col(pp, 3)].astype(F32)
        y = y_all[pp * blk:(pp + 1) * blk] * gnw_ref[:, pcol(pp)] * _silu(gp)
        cat_ref[:, q_cols + pp * LANES:q_cols + (pp + 1) * LANES] = y.astype(BF16)


def _prompt_mixer(p_p, kv_p, sinks, gn_w, n_pbatch, t_prompt, ret_tabs):
    blk = WINDOW
    nb = t_prompt // blk
    p_cols = p_p.shape[1]
    kv_cols = kv_p.shape[1]
    d_mix = (N_ATTN_HEADS + N_RET_HEADS) * HEAD_DIM
    decay_mask, q_dec, k_dec, chunk_decay = ret_tabs
    n_pairs = N_RET_HEADS // 2
    bd = _block_diag_mask()
    cdec = jnp.stack([bd * chunk_decay[0, pp * LANES:(pp + 1) * LANES][:, None] for pp in range(n_pairs)])
    const2 = lambda shape: pl.BlockSpec(shape, lambda b, j: (0,) * len(shape))
    return pl.pallas_call(
        _prompt_mixer_body,
        grid=(n_pbatch, nb),
        in_specs=[
            pl.BlockSpec(memory_space=pltpu.SMEM),
            pl.BlockSpec((blk, p_cols), lambda b, j: (b * nb + j, 0)),
            pl.BlockSpec((blk, kv_cols), lambda b, j: (b * nb + j, 0)),
            pl.BlockSpec((blk, kv_cols), lambda b, j: (b * nb + jnp.maximum(j - 1, 0), 0)),
            const2((N_RET_HEADS, blk, blk)),
            const2((blk, N_RET_HEADS * HEAD_DIM)),
            const2((blk, N_RET_HEADS * HEAD_DIM)),
            const2((n_pairs, LANES, LANES)),
            const2((LANES, LANES)),
            const2((LANES, LANES)),
            const2((1, N_RET_HEADS * HEAD_DIM)),
        ],
        out_specs=[
            pl.BlockSpec((blk, d_mix), lambda b, j: (b * nb + j, 0)),
            pl.BlockSpec((1, n_pairs, LANES, LANES), lambda b, j: (b, 0, 0, 0)),
        ],
        out_shape=[
            jax.ShapeDtypeStruct((n_pbatch * t_prompt, d_mix), BF16),
            jax.ShapeDtypeStruct((n_pbatch, n_pairs, LANES, LANES), F32),
        ],
        compiler_params=_cparams(("arbitrary", "arbitrary")),
        name="prompt_mixer",
    )(sinks, p_p, kv_p, kv_p, decay_mask, q_dec, k_dec, cdec, bd, _pair_average_matrix(), gn_w.reshape(1, -1))


def _sample_mixer_body(t_new, sink_ref, p_ref, kv_ref, kbuf_ref, vbuf_ref, ret_ref, d4_ref, qdec_ref, kdec_ref,
                       cdec_ref, avg_ref, gnw_ref, cat_ref, knew_ref, vnew_ref, rnew_ref, ka2_ref, va2_ref, qs2_ref):
    w = kbuf_ref.shape[1]
    q_cols = N_ATTN_HEADS * HEAD_DIM
    r_cols = N_RET_HEADS * HEAD_DIM
    kvw = N_KV_HEADS * HEAD_DIM
    n_keys = ka2_ref.shape[1]
    rows_per_head = SUBLANES
    ka2_ref[...] = jnp.zeros_like(ka2_ref)
    va2_ref[...] = jnp.zeros_like(va2_ref)
    qs2_ref[...] = jnp.zeros_like(qs2_ref)
    low4 = _lane_is_low_half((t_new, LANES))
    lowk = _lane_is_low_half((n_keys, LANES))
    n_rows = GQA_GROUP * rows_per_head
    row = lax.broadcasted_iota(jnp.int32, (n_rows, n_keys), 0)
    key = lax.broadcasted_iota(jnp.int32, (n_rows, n_keys), 1)
    t_of_row = row % rows_per_head
    mask = (t_of_row < t_new) & (key > t_of_row) & (key <= t_of_row + WINDOW) & (key < w + t_new)
    head_of_row = lax.broadcasted_iota(jnp.int32, (n_rows, 1), 0) // rows_per_head

    def per_group(g, carry):
        units = range(SAMPLE_GROUP)
        bs = [SAMPLE_GROUP * g + u for u in units]
        items_a = [(u, kvh) for u in units for kvh in range(N_KV_HEADS)]
        items_r = [(u, pp) for u in units for pp in range(N_RET_HEADS // 2)]
        for u in units:
            b, ka_ref, va_ref, qs_ref = bs[u], ka2_ref.at[u], va2_ref.at[u], qs2_ref.at[u]
            ka_ref[0:w, :] = kbuf_ref[b]
            va_ref[0:w, :] = vbuf_ref[b]
            ka_ref[w:w + t_new, :] = kv_ref[b][:, 0:kvw]
            va_ref[w:w + t_new, :] = kv_ref[b][:, kvw:2 * kvw]
            knew_ref[b] = ka_ref[t_new:t_new + w, :]
            vnew_ref[b] = va_ref[t_new:t_new + w, :]
            for h in range(N_ATTN_HEADS):
                c0 = (h // 2) * LANES
                qp = p_ref[b][:, c0:c0 + LANES]
                qs_ref[h * rows_per_head:h * rows_per_head + t_new, :] = jnp.where(
                    low4 if h % 2 == 0 else jnp.logical_not(low4), qp, 0.0)
        kall = [ka2_ref[u] for u in units]
        vall = [va2_ref[u] for u in units]
        kswap = [pltpu.roll(kall[u], HEAD_DIM, 1) for u in units]
        vswap = [pltpu.roll(vall[u], HEAD_DIM, 1) for u in units]
        keep = [lowk, jnp.logical_not(lowk)]
        k2 = {(u, kvh): jnp.where(keep[kvh], kall[u], kswap[u]) for u, kvh in items_a}
        v2 = {(u, kvh): jnp.where(keep[kvh], vall[u], vswap[u]) for u, kvh in items_a}
        sinks = []
        for kvh in range(N_KV_HEADS):
            sink = jnp.zeros((n_rows, 1), F32)
            for hh in range(GQA_GROUP):
                sink = jnp.where(head_of_row == hh, sink_ref[kvh * GQA_GROUP + hh], sink)
            sinks.append(sink)
        s = {it: jnp.where(mask, _dot_nt(qs2_ref[it[0], it[1] * n_rows:(it[1] + 1) * n_rows, :], k2[it], HIGHEST),
                           NEG_INF) for it in items_a}
        m = {it: jnp.maximum(jnp.max(s[it], axis=-1, keepdims=True), sinks[it[1]]) for it in items_a}
        e = {it: jnp.exp(s[it] - m[it]) for it in items_a}
        den = {it: jnp.sum(e[it], axis=-1, keepdims=True) + jnp.exp(sinks[it[1]] - m[it]) for it in items_a}
        o = {it: _dot(e[it], v2[it], HIGHEST) / den[it] for it in items_a}
        for u, kvh in items_a:
            for pr in range(GQA_GROUP // 2):
                o_lo = o[u, kvh][(2 * pr) * rows_per_head:(2 * pr) * rows_per_head + t_new]
                o_hi = o[u, kvh][(2 * pr + 1) * rows_per_head:(2 * pr + 1) * rows_per_head + t_new]
                c0 = (kvh * (GQA_GROUP // 2) + pr) * LANES
                cat_ref[bs[u], :, c0:c0 + LANES] = jnp.where(low4, o_lo, o_hi)
        col = lambda pp, part: slice(q_cols + part * r_cols + pp * LANES, q_cols + part * r_cols + (pp + 1) * LANES)
        pcol = lambda pp: slice(pp * LANES, (pp + 1) * LANES)
        qp = {it: p_ref[bs[it[0]]][:, col(it[1], 0)] for it in items_r}
        kp = {it: p_ref[bs[it[0]]][:, col(it[1], 1)] for it in items_r}
        vp = {it: p_ref[bs[it[0]]][:, col(it[1], 2)] for it in items_r}
        r0 = {it: ret_ref[bs[it[0]], it[1]] for it in items_r}
        q_lo = {it: jnp.where(low4, qp[it], 0.0) for it in items_r}
        q_hi = {it: jnp.where(low4, 0.0, qp[it]) for it in items_r}
        sc_lo = {it: _dot_nt(q_lo[it], kp[it], HIGHEST) * d4_ref[2 * it[1]] for it in items_r}
        sc_hi = {it: _dot_nt(q_hi[it], kp[it], HIGHEST) * d4_ref[2 * it[1] + 1] for it in items_r}
        oc_lo = {it: _dot(q_lo[it] * qdec_ref[:, pcol(it[1])], r0[it], HIGHEST) for it in items_r}
        oc_hi = {it: _dot(q_hi[it] * qdec_ref[:, pcol(it[1])], r0[it], HIGHEST) for it in items_r}
        kv_full = {it: _dot_tn(kp[it] * kdec_ref[:, pcol(it[1])], vp[it], HIGHEST) for it in items_r}
        oi_lo = {it: _dot(sc_lo[it], vp[it], HIGHEST) for it in items_r}
        oi_hi = {it: _dot(sc_hi[it], vp[it], HIGHEST) for it in items_r}
        for it in items_r:
            kv_pair = jnp.concatenate([kv_full[it][0:HEAD_DIM, 0:HEAD_DIM], kv_full[it][HEAD_DIM:, HEAD_DIM:]], axis=0)
            rnew_ref[bs[it[0]], it[1]] = cdec_ref[it[1]] * r0[it] + kv_pair
        o_ret = [jnp.where(low4, oi_lo[it], oi_hi[it]) + jnp.concatenate([oc_lo[it], oc_hi[it]], axis=1)
                 for it in items_r]
        y_ret = _group_norm_pairs(o_ret, avg_ref[...])
        for it, y in zip(items_r, y_ret):
            gp = p_ref[bs[it[0]]][:, col(it[1], 3)]
            cat_ref[bs[it[0]], :, q_cols + it[1] * LANES:q_cols + (it[1] + 1) * LANES] = (
                y * gnw_ref[:, pcol(it[1])] * _silu(gp))
        return carry

    lax.fori_loop(0, kbuf_ref.shape[0] // SAMPLE_GROUP, per_group, 0)


def _sample_mixer(p_s, kv_s, k_buf, v_buf, ret0, sinks, gn_w, ret_tabs):
    nb, t_new, p_cols = p_s.shape
    kv_cols = kv_s.shape[2]
    w = k_buf.shape[1]
    bb = SAMPLE_BATCH_TILE
    n_pairs = N_RET_HEADS // 2
    d_mix = (N_ATTN_HEADS + N_RET_HEADS) * HEAD_DIM
    decay_mask, q_dec, k_dec, chunk_decay = ret_tabs
    n_keys = ((w + t_new + SUBLANES - 1) // SUBLANES) * SUBLANES
    cdec = jnp.stack([jnp.broadcast_to(chunk_decay[0, pp * LANES:(pp + 1) * LANES][:, None], (LANES, HEAD_DIM))
                      for pp in range(n_pairs)])
    const1 = lambda shape: pl.BlockSpec(shape, lambda i: (0,) * len(shape))
    body = functools.partial(_sample_mixer_body, t_new)
    return pl.pallas_call(
        body,
        grid=(nb // bb,),
        in_specs=[
            pl.BlockSpec(memory_space=pltpu.SMEM),
            pl.BlockSpec((bb, t_new, p_cols), lambda i: (i, 0, 0)),
            pl.BlockSpec((bb, t_new, kv_cols), lambda i: (i, 0, 0)),
            pl.BlockSpec((bb, w, LANES), lambda i: (i, 0, 0)),
            pl.BlockSpec((bb, w, LANES), lambda i: (i, 0, 0)),
            pl.BlockSpec((bb, n_pairs, LANES, HEAD_DIM), lambda i: (i, 0, 0, 0)),
            const1((N_RET_HEADS, t_new, t_new)),
            const1((t_new, N_RET_HEADS * HEAD_DIM)),
            const1((t_new, N_RET_HEADS * HEAD_DIM)),
            const1((n_pairs, LANES, HEAD_DIM)),
            const1((LANES, LANES)),
            const1((1, N_RET_HEADS * HEAD_DIM)),
        ],
        out_specs=[
            pl.BlockSpec((bb, t_new, d_mix), lambda i: (i, 0, 0)),
            pl.BlockSpec((bb, w, LANES), lambda i: (i, 0, 0)),
            pl.BlockSpec((bb, w, LANES), lambda i: (i, 0, 0)),
            pl.BlockSpec((bb, n_pairs, LANES, HEAD_DIM), lambda i: (i, 0, 0, 0)),
        ],
        out_shape=[
            jax.ShapeDtypeStruct((nb, t_new, d_mix), F32),
            jax.ShapeDtypeStruct((nb, w, LANES), F32),
            jax.ShapeDtypeStruct((nb, w, LANES), F32),
            jax.ShapeDtypeStruct((nb, n_pairs, LANES, HEAD_DIM), F32),
        ],
        scratch_shapes=[
            pltpu.VMEM((SAMPLE_GROUP, n_keys, LANES), F32),
            pltpu.VMEM((SAMPLE_GROUP, n_keys, LANES), F32),
            pltpu.VMEM((SAMPLE_GROUP, N_ATTN_HEADS * SUBLANES, LANES), F32),
        ],
        compiler_params=_cparams(("arbitrary",)),
        name="sample_mixer",
    )(sinks, p_s, kv_s, k_buf, v_buf, ret0, decay_mask, q_dec, k_dec, cdec, _pair_average_matrix(), gn_w.reshape(1, -1))


def _router_body(geo, xp_ref, xs_ref, catp_ref, cats_ref, wout_ref, g1p_ref, g1s_ref, nw_ref, shp_ref, scp_ref,
                 shs_ref, scs_ref, wr_ref, br_ref, tri_ref, x1_ref, h2_ref, idx_ref, gate_ref, rank_ref, cnt_ref):
    i = pl.program_id(0)
    tv = functools.partial(_tile_vec, i, geo)
    cat = _tile_rows(i, geo, catp_ref, cats_ref)
    x1 = _tile_rows(i, geo, xp_ref, xs_ref) + tv(g1p_ref, g1s_ref) * _dot(cat, wout_ref[...])
    x1_ref[...] = x1
    h2 = _rms_norm(x1, nw_ref[...]) * (1.0 + tv(scp_ref, scs_ref)) + tv(shp_ref, shs_ref)
    h2_ref[...] = h2.reshape(h2_ref.shape)
    logits = _dot_nt(wr_ref[...], h2, HIGHEST) + br_ref[...]
    e_iota = lax.broadcasted_iota(jnp.int32, logits.shape, 0)
    vals, sels = [], []
    for _ in range(TOP_K):
        m = jnp.max(logits, axis=0, keepdims=True)
        sel = jnp.min(jnp.where(logits == m, e_iota, N_EXPERTS), axis=0, keepdims=True)
        vals.append(m)
        sels.append(sel)
        logits = jnp.where(e_iota == sel, -jnp.inf, logits)
    es = [jnp.exp(v - vals[0]) for v in vals]
    den = es[0]
    for e in es[1:]:
        den = den + e

    @pl.when(i == 0)
    def _():
        cnt_ref[...] = jnp.zeros_like(cnt_ref)

    base = cnt_ref[...]
    for k in range(TOP_K):
        hit = e_iota == sels[k]
        onehot = jnp.where(hit, 1.0, 0.0)
        before = _dot(onehot.astype(BF16), tri_ref[...])
        rank = jnp.sum(jnp.where(hit, base + before, 0.0), axis=0, keepdims=True)
        base = base + jnp.sum(onehot, axis=1, keepdims=True)
        idx_ref[k:k + 1, :] = sels[k]
        gate_ref[k:k + 1, :] = es[k] / den
        rank_ref[k:k + 1, :] = rank.astype(jnp.int32)
    cnt_ref[...] = base


def _outproj_router(x_p, x_s, cat_p, cat_s, w_out_bf16, norm_w, mod_p, mod_s, layer, w_router_t, b_router, geo):
    n_p, d = x_p.shape
    n = n_p + x_s.shape[0]
    d_mix = cat_p.shape[1]
    pv = lambda chunk: _prompt_vec_spec(layer, chunk, d)
    sv = lambda chunk: _sample_vec_spec(layer, chunk, geo, d)
    tile = lambda cols: pl.BlockSpec((TOKEN_TILE, cols), lambda i: (i, 0))
    choice = pl.BlockSpec((TOP_K, TOKEN_TILE), lambda i: (0, i))
    tri = jnp.asarray(np.triu(np.ones((TOKEN_TILE, TOKEN_TILE), np.float32), 1)).astype(BF16)
    return pl.pallas_call(
        functools.partial(_router_body, geo),
        grid=(n // TOKEN_TILE,),
        in_specs=[
            _prompt_rows_spec(geo, d), _sample_rows_spec(geo, d),
            _prompt_rows_spec(geo, d_mix), _sample_rows_spec(geo, d_mix),
            pl.BlockSpec((d_mix, d), lambda i: (0, 0)),
            pv(2), sv(2),
            pl.BlockSpec((1, d), lambda i: (0, 0)),
            pv(3), pv(4), sv(3), sv(4),
            pl.BlockSpec((N_EXPERTS, d), lambda i: (0, 0)),
            pl.BlockSpec((N_EXPERTS, 1), lambda i: (0, 0)),
            pl.BlockSpec((TOKEN_TILE, TOKEN_TILE), lambda i: (0, 0)),
        ],
        out_specs=[
            tile(d),
            pl.BlockSpec((TOKEN_TILE, 1, d), lambda i: (i, 0, 0)),
            choice, choice, choice,
            pl.BlockSpec((N_EXPERTS, 1), lambda i: (0, 0)),
        ],
        out_shape=[
            jax.ShapeDtypeStruct((n, d), F32),
            jax.ShapeDtypeStruct((n, 1, d), F32),
            jax.ShapeDtypeStruct((TOP_K, n), jnp.int32),
            jax.ShapeDtypeStruct((TOP_K, n), F32),
            jax.ShapeDtypeStruct((TOP_K, n), jnp.int32),
            jax.ShapeDtypeStruct((N_EXPERTS, 1), F32),
        ],
        compiler_params=_cparams(("arbitrary",)),
        name="outproj_router",
    )(x_p, x_s, cat_p, cat_s, w_out_bf16, mod_p, mod_s, norm_w.reshape(1, d), mod_p, mod_p, mod_s, mod_s,
      w_router_t, b_router.reshape(N_EXPERTS, 1), tri)


def _routing_plan(top_idx, rank, counts_f, tm):
    k, n = top_idx.shape
    n_pairs = k * n
    n_tiles = (n_pairs + N_EXPERTS * (tm - 1)) // tm + 1
    counts = counts_f[:, 0].astype(jnp.int32)
    tiles_per = (counts + tm - 1) // tm
    tile_end = jnp.cumsum(tiles_per)
    tile_start = tile_end - tiles_per
    n_used = tile_end[-1]
    n_slots = n_tiles * tm

    def lookup(table, index):
        ids = jnp.arange(table.shape[0], dtype=jnp.int32).reshape((-1,) + (1,) * index.ndim)
        return jnp.sum(jnp.where(index[None] == ids, table.reshape(ids.shape), 0), axis=0)

    slot_of_pair = lookup(tile_start * tm, top_idx) + rank
    pad_counts = jnp.concatenate([tiles_per * tm - counts, (n_slots - n_used * tm).reshape(1)])
    pad_end = jnp.cumsum(pad_counts)
    first_pad_slot = jnp.concatenate([tile_start * tm + counts, (n_used * tm).reshape(1)])
    k_pad = jnp.arange(n_slots - n_pairs, dtype=jnp.int32)
    seg = jnp.sum(k_pad[:, None] >= pad_end[None, :], axis=1).astype(jnp.int32)
    pad_slot = k_pad + lookup(first_pad_slot - (pad_end - pad_counts), seg)
    token_of_pair = jnp.broadcast_to(jnp.arange(n, dtype=jnp.int32)[None, :], (k, n))
    keys = jnp.concatenate([slot_of_pair.reshape(-1), pad_slot])
    vals = jnp.concatenate([token_of_pair.reshape(-1), jnp.zeros_like(pad_slot)])
    _, slot_rows = lax.sort((keys, vals), num_keys=1)
    tile_ids = jnp.arange(n_tiles, dtype=jnp.int32)
    tile_expert = jnp.minimum(jnp.sum(tile_ids[:, None] >= tile_end[None, :], axis=1), N_EXPERTS - 1).astype(jnp.int32)
    last_expert = tile_expert[jnp.maximum(n_used - 1, 0)]
    tile_expert = jnp.where(tile_ids < n_used, tile_expert, last_expert)
    return tile_expert, n_used.reshape(1).astype(jnp.int32), slot_rows.reshape(n_tiles, 1, tm), slot_of_pair


def _issue_row_copies(src_hbm, row_of, dst_row, sem, n_rows):
    def group(g, c):
        for u in range(DMA_UNROLL):
            r = g * DMA_UNROLL + u
            pltpu.make_async_copy(src_hbm.at[row_of(r)], dst_row(r), sem).start(priority=u % 2)
        return c
    lax.fori_loop(0, n_rows // DMA_UNROLL, group, 0)


def _issue_row_copies_static(src_hbm, row_of, dst_row, sem, lo, hi):
    for r in range(lo, hi):
        pltpu.make_async_copy(src_hbm.at[row_of(r)], dst_row(r), sem).start(priority=r % 2)


def _wait_row_copies(src_hbm, dst_buf, sem):
    pltpu.make_async_copy(src_hbm.at[pl.ds(0, dst_buf.shape[0])], dst_buf, sem).wait()


def _expert_body(te_ref, nt_ref, rows0_ref, rows1_ref, rows2_ref, h_hbm, wgu_ref, bg_ref, bu_ref, wd_ref, bd_ref,
                 perm_ref, y_ref, xbuf, sems, wgu_s, wd_s, act_s, x_s):
    j = pl.program_id(0)
    n_tiles = pl.num_programs(0)
    n_used = nt_ref[0]
    slot = j % ROW_BUFFERS
    far = (j + ROW_BUFFERS - 1) % ROW_BUFFERS
    tm = EXPERT_TILE
    f2 = wgu_ref.shape[2]
    n_chunks = f2 // MXU_WIDTH
    half = MXU_WIDTH // 2
    next_row = lambda r: rows2_ref[0, 0, r]
    next_dst = lambda r: xbuf.at[far, r]

    @pl.when(j == 0)
    def _():
        _issue_row_copies(h_hbm, lambda r: rows0_ref[0, 0, r], lambda r: xbuf.at[0, r], sems.at[0], tm)
        _issue_row_copies(h_hbm, lambda r: rows1_ref[0, 0, r], lambda r: xbuf.at[1, r], sems.at[1], tm)

    new_expert = jnp.logical_or(j == 0, te_ref[j] != te_ref[jnp.maximum(j - 1, 0)])

    @pl.when(jnp.logical_and(j < n_used, new_expert))
    def _():
        for c in range(n_chunks):
            cols = slice(c * MXU_WIDTH, (c + 1) * MXU_WIDTH)
            wgu_s[:, cols] = _dot(wgu_ref[0, :, cols].astype(BF16), perm_ref[...]).astype(BF16)
        wd_s[...] = wd_ref[0].astype(BF16)

    @pl.when(j < n_used)
    def _():
        _wait_row_copies(h_hbm, xbuf.at[slot], sems.at[slot])
        x_s[...] = xbuf[slot][:, 0, :]
        x = x_s[...].astype(BF16)
        issue_chunks = n_chunks // 2
        per_chunk = tm // issue_chunks
        for c in range(n_chunks):
            if c < issue_chunks:
                _issue_row_copies_static(h_hbm, next_row, next_dst, sems.at[far], c * per_chunk, (c + 1) * per_chunk)
            gu = _dot(x, wgu_s[:, c * MXU_WIDTH:(c + 1) * MXU_WIDTH])
            glu = jnp.minimum(gu[:, :half] + bg_ref[0, :, c * half:(c + 1) * half], SWIGLU_LIMIT)
            lin = jnp.clip(gu[:, half:] + bu_ref[0, :, c * half:(c + 1) * half], -SWIGLU_LIMIT, SWIGLU_LIMIT)
            act_s[:, c * half:(c + 1) * half] = (glu * jax.nn.sigmoid(SWIGLU_ALPHA * glu) * (lin + 1.0)).astype(BF16)
        y = _dot(act_s[...], wd_s[...]) + bd_ref[0]
        y_ref[...] = y.reshape(y_ref.shape)

    @pl.when(j >= n_used)
    def _():
        _wait_row_copies(h_hbm, xbuf.at[slot], sems.at[slot])
        _issue_row_copies(h_hbm, next_row, next_dst, sems.at[far], tm)
        y_ref[...] = jnp.zeros_like(y_ref)

    @pl.when(j == n_tiles - 1)
    def _():
        for ahead in range(1, ROW_BUFFERS):
            late = (j + ahead) % ROW_BUFFERS
            _wait_row_copies(h_hbm, xbuf.at[late], sems.at[late])


def _deinterleave_matrix():
    m = np.zeros((MXU_WIDTH, MXU_WIDTH), np.float32)
    j = np.arange(MXU_WIDTH // 2)
    m[2 * j, j] = 1.0
    m[2 * j + 1, MXU_WIDTH // 2 + j] = 1.0
    return jnp.asarray(m).astype(BF16)


def _routed_experts(h2_rows, tile_expert, n_used, slot_rows, w_gate_up, bg, bu, w_down, bd, layer):
    n, _, d = h2_rows.shape
    f2 = w_gate_up.shape[3]
    f = f2 // 2
    tm = EXPERT_TILE
    n_tiles = slot_rows.shape[0]
    assert ROW_BUFFERS == 3 and n_tiles >= ROW_BUFFERS
    by_expert = lambda shape: pl.BlockSpec(shape, lambda j, te, nt: (te[j], 0, 0))
    layer_expert = lambda shape: pl.BlockSpec((None,) + shape, lambda j, te, nt: (layer, te[j], 0, 0))
    rows_ahead = lambda k: pl.BlockSpec((1, 1, tm), lambda j, te, nt: (jnp.minimum(j + k, n_tiles - 1), 0, 0),
                                        memory_space=pltpu.SMEM)
    grid_spec = pltpu.PrefetchScalarGridSpec(
        num_scalar_prefetch=2,
        grid=(n_tiles,),
        in_specs=[
            rows_ahead(0), rows_ahead(1), rows_ahead(2),
            pl.BlockSpec(memory_space=pl.ANY),
            layer_expert((1, d, f2)),
            by_expert((1, 1, f)), by_expert((1, 1, f)),
            layer_expert((1, f, d)),
            by_expert((1, 1, d)),
            pl.BlockSpec((MXU_WIDTH, MXU_WIDTH), lambda j, te, nt: (0, 0)),
        ],
        out_specs=pl.BlockSpec((tm, 1, d), lambda j, te, nt: (j, 0, 0)),
        scratch_shapes=[
            pltpu.VMEM((ROW_BUFFERS, tm, 1, d), F32),
            pltpu.SemaphoreType.DMA((ROW_BUFFERS,)),
            pltpu.VMEM((d, f2), BF16),
            pltpu.VMEM((f, d), BF16),
            pltpu.VMEM((tm, f), BF16),
            pltpu.VMEM((tm, d), F32),
        ],
    )
    return pl.pallas_call(
        _expert_body,
        grid_spec=grid_spec,
        out_shape=jax.ShapeDtypeStruct((n_tiles * tm, 1, d), F32),
        compiler_params=_cparams(("arbitrary",)),
        name="routed_experts",
    )(tile_expert, n_used, slot_rows, slot_rows, slot_rows, h2_rows, w_gate_up, bg, bu, w_down, bd,
      _deinterleave_matrix())


def _combine_body(geo, final_norm, slots_cur_ref, slots_nxt_ref, x1_ref, gate_ref, g2p_ref, g2s_ref, nfw_ref, y_hbm,
                  outp_ref, outs_ref, ybuf_even, ybuf_odd, sems):
    i = pl.program_id(0)
    n_steps = pl.num_programs(0)
    tm = TOKEN_TILE

    @pl.when(i == 0)
    def _():
        for k in range(TOP_K):
            _issue_row_copies(y_hbm, lambda r, k=k: slots_cur_ref[0, k, r], lambda r, k=k: ybuf_even.at[k, r],
                              sems.at[0], tm)

    def step(cur, nxt, sem_cur, sem_nxt):
        for k in range(TOP_K):
            _wait_row_copies(y_hbm, cur.at[k], sem_cur)
        for k in range(TOP_K):
            _issue_row_copies_static(y_hbm, lambda r, k=k: slots_nxt_ref[0, k, r], lambda r, k=k: nxt.at[k, r],
                                     sem_nxt, 0, tm)
        acc = cur[0][:, 0, :] * gate_ref[:, 0:1]
        for k in range(1, TOP_K):
            acc = acc + cur[k][:, 0, :] * gate_ref[:, k:k + 1]
        x2 = x1_ref[...] + _tile_vec(i, geo, g2p_ref, g2s_ref) * acc
        if final_norm:
            x2 = _rms_norm(x2, nfw_ref[...])

        @pl.when(i < geo.n_ptiles)
        def _():
            outp_ref[...] = x2

        @pl.when(i >= geo.n_ptiles)
        def _():
            outs_ref[...] = x2

        @pl.when(i == n_steps - 1)
        def _():
            for k in range(TOP_K):
                _wait_row_copies(y_hbm, nxt.at[k], sem_nxt)

    @pl.when(i % 2 == 0)
    def _():
        step(ybuf_even, ybuf_odd, sems.at[0], sems.at[1])

    @pl.when(i % 2 == 1)
    def _():
        step(ybuf_odd, ybuf_even, sems.at[1], sems.at[0])


def _combine(x1, y_rows, slot_of_pair, gates_t, mod_p, mod_s, layer, final_w, geo, final_norm):
    n, d = x1.shape
    tm = TOKEN_TILE
    n_steps = n // tm
    n_p = geo.n_ptiles * tm
    slots = slot_of_pair.reshape(TOP_K, n_steps, tm).transpose(1, 0, 2)
    return pl.pallas_call(
        functools.partial(_combine_body, geo, final_norm),
        grid=(n_steps,),
        in_specs=[
            pl.BlockSpec((1, TOP_K, tm), lambda i: (i, 0, 0), memory_space=pltpu.SMEM),
            pl.BlockSpec((1, TOP_K, tm), lambda i: (jnp.minimum(i + 1, n_steps - 1), 0, 0), memory_space=pltpu.SMEM),
            pl.BlockSpec((tm, d), lambda i: (i, 0)),
            pl.BlockSpec((tm, TOP_K), lambda i: (i, 0)),
            _prompt_vec_spec(layer, 5, d),
            _sample_vec_spec(layer, 5, geo, d),
            pl.BlockSpec((1, d), lambda i: (0, 0)),
            pl.BlockSpec(memory_space=pl.ANY),
        ],
        out_specs=[_prompt_rows_spec(geo, d), _sample_rows_spec(geo, d)],
        out_shape=[jax.ShapeDtypeStruct((n_p, d), F32), jax.ShapeDtypeStruct((n - n_p, d), F32)],
        scratch_shapes=[pltpu.VMEM((TOP_K, tm, 1, d), F32), pltpu.VMEM((TOP_K, tm, 1, d), F32),
                        pltpu.SemaphoreType.DMA((2,))],
        compiler_params=_cparams(("arbitrary",)),
        name="moe_combine",
    )(slots, slots, x1, gates_t, mod_p, mod_s, final_w.reshape(1, d), y_rows)


def kernel(x_prompt, x_sample, state_swa_k, state_swa_v, state_ret, c_prompt, c_sample, norm_mix_w, w_ada, b_ada, w_in, attn_sinks, ret_gn_w, w_out, norm_ffn_w, w_router, b_router, w_gate_up, b_gate_up, w_down, b_down, norm_final_w):
    n_pbatch, t_prompt, d = x_prompt.shape
    n_sbatch, t_sample, _ = x_sample.shape
    depth = w_in.shape[0]
    n_p = n_pbatch * t_prompt
    n_s = n_sbatch * t_sample
    win = state_swa_k.shape[2]
    assert d == (N_ATTN_HEADS + N_RET_HEADS) * HEAD_DIM and w_gate_up.shape[1] == N_EXPERTS
    assert t_prompt % TOKEN_TILE == 0 and n_s % TOKEN_TILE == 0 and TOKEN_TILE % t_sample == 0
    assert t_prompt % WINDOW == 0 and RET_CHUNK == WINDOW and win == WINDOW and t_sample <= SUBLANES
    assert n_sbatch % SAMPLE_BATCH_TILE == 0 and n_pbatch <= SUBLANES
    assert w_gate_up.shape[3] % MXU_WIDTH == 0 and EXPERT_TILE % DMA_UNROLL == 0 and TOKEN_TILE % DMA_UNROLL == 0
    geo = _Geometry(n_p // TOKEN_TILE, t_prompt // TOKEN_TILE, n_pbatch)

    c_all = jnp.concatenate([c_prompt, jnp.zeros((SUBLANES - n_pbatch, d), F32), c_sample], axis=0)
    mod = _ada_modulation(c_all, w_ada, b_ada)
    mod_p = mod[:, :SUBLANES]
    mod_s = jnp.repeat(mod[:, SUBLANES:], t_sample, axis=1)

    rope_tab = _rope_tables(t_prompt, t_sample)
    ret_tabs_p = _retention_tables(RET_CHUNK)
    ret_tabs_s = _retention_tables(t_sample)
    f = w_gate_up.shape[3] // 2

    x_p = x_prompt.reshape(n_p, d)
    x_s = x_sample.reshape(n_s, d)
    kp_l, vp_l, rp_l, ks_l, vs_l, rs_l = [], [], [], [], [], []
    for l in range(depth):
        p_all, kv_all = _input_projection(x_p, x_s, norm_mix_w[l], mod_p, mod_s, l, w_in[l].astype(BF16), rope_tab, geo)
        cat_p, r_pairs = _prompt_mixer(p_all, kv_all, attn_sinks[l], ret_gn_w[l], n_pbatch, t_prompt, ret_tabs_p)
        p_s = p_all[n_p:].astype(F32)
        kv_s = kv_all[n_p:]
        kv_p = kv_all[:n_p]
        cat_s, k_new, v_new, r_new = _sample_mixer(
            p_s.reshape(n_sbatch, t_sample, -1), kv_s.reshape(n_sbatch, t_sample, -1),
            state_swa_k[l].reshape(n_sbatch, win, LANES), state_swa_v[l].reshape(n_sbatch, win, LANES),
            state_ret[l].reshape(n_sbatch, N_RET_HEADS // 2, LANES, HEAD_DIM), attn_sinks[l], ret_gn_w[l], ret_tabs_s)
        x1, h2_rows, top_idx, gates, rank, counts = _outproj_router(
            x_p, x_s, cat_p, cat_s.reshape(n_s, d), w_out[l].astype(BF16), norm_ffn_w[l], mod_p, mod_s, l,
            w_router[l].T, b_router[l], geo)
        tile_expert, n_used, slot_rows, slot_of_pair = _routing_plan(top_idx, rank, counts, EXPERT_TILE)
        b_gu = b_gate_up[l].reshape(N_EXPERTS, 1, f, 2)
        y_rows = _routed_experts(h2_rows, tile_expert, n_used, slot_rows, w_gate_up,
                                 b_gu[..., 0], b_gu[..., 1], w_down, b_down[l][:, None, :], l)
        x_p, x_s = _combine(x1, y_rows, slot_of_pair, gates.T, mod_p, mod_s, l, norm_final_w, geo,
                            final_norm=(l == depth - 1))

        kv_last = kv_p.reshape(n_pbatch, t_prompt, -1)[:, t_prompt - WINDOW:]
        kv_last = kv_last.reshape(n_pbatch, WINDOW, 2, N_KV_HEADS, HEAD_DIM)
        kp_l.append(kv_last[:, :, 0])
        vp_l.append(kv_last[:, :, 1])
        rp_l.append(jnp.stack([r_pairs[:, :, :HEAD_DIM, :HEAD_DIM], r_pairs[:, :, HEAD_DIM:, HEAD_DIM:]], axis=2)
                    .reshape(n_pbatch, N_RET_HEADS, HEAD_DIM, HEAD_DIM))
        ks_l.append(k_new.reshape(n_sbatch, win, N_KV_HEADS, HEAD_DIM))
        vs_l.append(v_new.reshape(n_sbatch, win, N_KV_HEADS, HEAD_DIM))
        rs_l.append(r_new.reshape(n_sbatch, N_RET_HEADS, HEAD_DIM, HEAD_DIM))
    return (x_p.reshape(n_pbatch, t_prompt, d), x_s.reshape(n_sbatch, t_sample, d), jnp.stack(kp_l), jnp.stack(vp_l),
            jnp.stack(rp_l), jnp.stack(ks_l), jnp.stack(vs_l), jnp.stack(rs_l))
```

```python
import functools

import numpy as np
import jax
import jax.numpy as jnp
from jax import lax
from jax.experimental import pallas as pl
from jax.experimental.pallas import tpu as pltpu

F32 = jnp.float32
BF16 = jnp.bfloat16
HIGHEST = lax.Precision.HIGHEST

HEAD_DIM = 64
N_ATTN_HEADS = 8
N_KV_HEADS = 2
GQA_GROUP = N_ATTN_HEADS // N_KV_HEADS
WINDOW = 128
ROPE_THETA = 500000.0
ROPE_DIMS = HEAD_DIM // 4
N_RET_HEADS = 8
RET_CHUNK = 128
RET_THETA = 10000.0
N_EXPERTS = 32
TOP_K = 4
SWIGLU_LIMIT = 7.0
SWIGLU_ALPHA = 1.702
NORM_EPS = 1e-5
PAST_LEN = 16384

LANES = 128
SUBLANES = 8
MXU_WIDTH = 256
VMEM_LIMIT_BYTES = 56 * 1024 * 1024

TOKEN_TILE = 256
EXPERT_TILE = 256
SAMPLE_BATCH_TILE = 8
SAMPLE_GROUP = 4
DMA_UNROLL = 8
ROW_BUFFERS = 3

NEG_INF = -1e30


def _cparams(semantics):
    return pltpu.CompilerParams(dimension_semantics=semantics, vmem_limit_bytes=VMEM_LIMIT_BYTES)


def _lane_is_low_half(shape):
    return lax.broadcasted_iota(jnp.int32, shape, len(shape) - 1) < HEAD_DIM


def _dot(a, b, precision=None):
    return jnp.dot(a, b, preferred_element_type=F32, precision=precision)


def _dot_nt(a, b, precision=None):
    return lax.dot_general(a, b, (((1,), (1,)), ((), ())), preferred_element_type=F32, precision=precision)


def _dot_tn(a, b, precision=None):
    return lax.dot_general(a, b, (((0,), (0,)), ((), ())), preferred_element_type=F32, precision=precision)


def _rms_norm(x, w):
    return x * lax.rsqrt(jnp.mean(x * x, axis=-1, keepdims=True) + NORM_EPS) * w


def _silu(x):
    return x * jax.nn.sigmoid(x)


def _ada_body(c_ref, w_ref, b_ref, o_ref):
    a = _silu(c_ref[...])
    o_ref[0] = _dot(a, w_ref[0], HIGHEST) + b_ref[0]


def _ada_modulation(c_all, w_ada, b_ada):
    depth, d, cols = w_ada.shape
    rows = c_all.shape[0]
    tn = 1024
    return pl.pallas_call(
        _ada_body,
        grid=(depth, cols // tn),
        in_specs=[
            pl.BlockSpec((rows, d), lambda l, j: (0, 0)),
            pl.BlockSpec((1, d, tn), lambda l, j: (l, 0, j)),
            pl.BlockSpec((1, 1, tn), lambda l, j: (l, 0, j)),
        ],
        out_specs=pl.BlockSpec((1, rows, tn), lambda l, j: (l, 0, j)),
        out_shape=jax.ShapeDtypeStruct((depth, rows, cols), F32),
        compiler_params=_cparams(("arbitrary", "arbitrary")),
        name="ada_modulation",
    )(c_all, w_ada, b_ada.reshape(depth, 1, cols))


class _Geometry:
    def __init__(self, n_ptiles, tiles_per_batch, n_pbatch):
        self.n_ptiles = n_ptiles
        self.tiles_per_batch = tiles_per_batch
        self.n_pbatch = n_pbatch


def _tile_vec(i, geo, prompt_ref, sample_ref):
    b = jnp.minimum(i // geo.tiles_per_batch, geo.n_pbatch - 1)
    return jnp.where(i >= geo.n_ptiles, sample_ref[...], prompt_ref[pl.ds(b, 1), :])


def _tile_rows(i, geo, prompt_ref, sample_ref):
    return jnp.where(i >= geo.n_ptiles, sample_ref[...].astype(prompt_ref.dtype), prompt_ref[...])


def _prompt_vec_spec(layer, chunk, d):
    return pl.BlockSpec((None, SUBLANES, d), lambda i: (layer, 0, chunk))


def _sample_vec_spec(layer, chunk, geo, d):
    return pl.BlockSpec((None, TOKEN_TILE, d), lambda i: (layer, jnp.maximum(i - geo.n_ptiles, 0), chunk))


def _prompt_rows_spec(geo, cols):
    return pl.BlockSpec((TOKEN_TILE, cols), lambda i: (jnp.minimum(i, geo.n_ptiles - 1), 0))


def _sample_rows_spec(geo, cols):
    return pl.BlockSpec((TOKEN_TILE, cols), lambda i: (jnp.maximum(i - geo.n_ptiles, 0), 0))


def _rotate(xc, tab_ref, base, shift):
    return (xc * tab_ref[base] + pltpu.roll(xc, shift, 1) * tab_ref[base + 1]
            + pltpu.roll(xc, LANES - shift, 1) * tab_ref[base + 2])


def _inproj_body(geo, xp_ref, xs_ref, nw_ref, shp_ref, scp_ref, shs_ref, scs_ref, w_ref, tab_ref, p_ref, kv_ref):
    i = pl.program_id(0)
    x = _tile_rows(i, geo, xp_ref, xs_ref)
    h = _rms_norm(x, nw_ref[...]) * (1.0 + _tile_vec(i, geo, scp_ref, scs_ref)) + _tile_vec(i, geo, shp_ref, shs_ref)
    proj = _dot(h.astype(BF16), w_ref[...])
    q_cols = N_ATTN_HEADS * HEAD_DIM
    kv_cols = N_KV_HEADS * HEAD_DIM
    r_cols = N_RET_HEADS * HEAD_DIM
    o_ka = q_cols
    o_va = o_ka + kv_cols
    o_qr = o_va + kv_cols
    o_kr = o_qr + r_cols
    o_vr = o_kr + r_cols
    o_g = o_vr + r_cols
    attn_scale = HEAD_DIM ** -0.5
    ret_scale = HEAD_DIM ** -0.5
    half_a = ROPE_DIMS // 2
    half_r = HEAD_DIM // 2

    def emit(p_ref, kv_ref):
        dt = p_ref.dtype
        for c in range(q_cols // LANES):
            xc = proj[:, c * LANES:(c + 1) * LANES]
            p_ref[:, c * LANES:(c + 1) * LANES] = (_rotate(xc, tab_ref, 0, half_a) * attn_scale).astype(dt)
        kv_ref[:, 0:kv_cols] = _rotate(proj[:, o_ka:o_ka + kv_cols], tab_ref, 0, half_a)
        kv_ref[:, kv_cols:2 * kv_cols] = proj[:, o_va:o_va + kv_cols]
        for c in range(r_cols // LANES):
            xq = proj[:, o_qr + c * LANES:o_qr + (c + 1) * LANES]
            xk = proj[:, o_kr + c * LANES:o_kr + (c + 1) * LANES]
            p_ref[:, q_cols + c * LANES:q_cols + (c + 1) * LANES] = _rotate(xq, tab_ref, 3, half_r).astype(dt)
            p_ref[:, q_cols + r_cols + c * LANES:q_cols + r_cols + (c + 1) * LANES] = (
                _rotate(xk, tab_ref, 3, half_r) * ret_scale).astype(dt)
        p_ref[:, q_cols + 2 * r_cols:q_cols + 3 * r_cols] = proj[:, o_vr:o_vr + r_cols].astype(dt)
        p_ref[:, q_cols + 3 * r_cols:q_cols + 4 * r_cols] = proj[:, o_g:o_g + r_cols].astype(dt)

    emit(p_ref, kv_ref)


def _input_projection(x_p, x_s, norm_w, mod_p, mod_s, layer, w_in_bf16, rope_tab, geo):
    n_p, d = x_p.shape
    n_s = x_s.shape[0]
    in_cols = w_in_bf16.shape[1]
    kv_cols = 2 * N_KV_HEADS * HEAD_DIM
    p_cols = in_cols - kv_cols
    tpb = geo.tiles_per_batch
    tab_index = lambda i: (0, jnp.where(i >= geo.n_ptiles, tpb, i % tpb), 0)
    return pl.pallas_call(
        functools.partial(_inproj_body, geo),
        grid=((n_p + n_s) // TOKEN_TILE,),
        in_specs=[
            _prompt_rows_spec(geo, d), _sample_rows_spec(geo, d),
            pl.BlockSpec((1, d), lambda i: (0, 0)),
            _prompt_vec_spec(layer, 0, d), _prompt_vec_spec(layer, 1, d),
            _sample_vec_spec(layer, 0, geo, d), _sample_vec_spec(layer, 1, geo, d),
            pl.BlockSpec((d, in_cols), lambda i: (0, 0)),
            pl.BlockSpec((6, TOKEN_TILE, LANES), tab_index),
        ],
        out_specs=[
            pl.BlockSpec((TOKEN_TILE, p_cols), lambda i: (i, 0)),
            pl.BlockSpec((TOKEN_TILE, kv_cols), lambda i: (i, 0)),
        ],
        out_shape=[
            jax.ShapeDtypeStruct((n_p + n_s, p_cols), BF16),
            jax.ShapeDtypeStruct((n_p + n_s, kv_cols), F32),
        ],
        compiler_params=_cparams(("arbitrary",)),
        name="input_projection",
    )(x_p, x_s, norm_w.reshape(1, d), mod_p, mod_p, mod_s, mod_s, w_in_bf16, rope_tab)


def _rope_tables(t_prompt, t_sample):
    pos = jnp.concatenate([jnp.arange(t_prompt, dtype=jnp.int32),
                           PAST_LEN + (jnp.arange(TOKEN_TILE, dtype=jnp.int32) % t_sample)])
    d = np.arange(LANES) % HEAD_DIM

    def tables(n_dims, theta):
        half = n_dims // 2
        freqs = jnp.power(jnp.float32(theta), -jnp.arange(half, dtype=jnp.float32) / half)
        ang = pos.astype(jnp.float32)[:, None] * freqs[None, :]
        cos, sin = jnp.cos(ang), jnp.sin(ang)
        fidx = np.where(d < n_dims, d % half, 0)
        cos_l = jnp.where(jnp.asarray(d < n_dims)[None, :], cos[:, fidx], 1.0)
        sin_l = sin[:, fidx]
        upper = jnp.asarray((d >= half) & (d < n_dims))[None, :]
        lower = jnp.asarray(d < half)[None, :]
        return [cos_l, jnp.where(upper, sin_l, 0.0), jnp.where(lower, -sin_l, 0.0)]

    return jnp.stack(tables(ROPE_DIMS, ROPE_THETA) + tables(HEAD_DIM, RET_THETA)).astype(F32)


def _retention_tables(c):
    h = N_RET_HEADS
    log_gamma = jnp.log(1.0 - jnp.power(2.0, -5.0 - jnp.arange(h, dtype=jnp.float32)))
    idx = jnp.arange(c, dtype=jnp.float32)
    diff = idx[:, None] - idx[None, :]
    decay_mask = jnp.where(diff >= 0, jnp.exp(log_gamma[:, None, None] * jnp.maximum(diff, 0.0)), 0.0)
    k_dec = jnp.exp(log_gamma[None, :] * (c - 1 - idx)[:, None])
    q_dec = jnp.exp(log_gamma[None, :] * (idx + 1.0)[:, None])
    chunk_decay = jnp.exp(log_gamma * c)
    rep = lambda a: jnp.repeat(a, HEAD_DIM, axis=-1)
    return decay_mask.astype(F32), rep(q_dec).astype(F32), rep(k_dec).astype(F32), rep(chunk_decay[None, :]).astype(F32)


def _group_norm_pairs(items, avg):
    avg_b = avg.astype(BF16)

    def block_means(xs):
        his = [x.astype(BF16) for x in xs]
        los = [(x - hi.astype(F32)).astype(BF16) for x, hi in zip(xs, his)]
        return [_dot(hi, avg_b) + _dot(lo, avg_b) for hi, lo in zip(his, los)]

    dlts = [o - mu for o, mu in zip(items, block_means(items))]
    varis = block_means([d * d for d in dlts])
    return [d * lax.rsqrt(v + NORM_EPS) for d, v in zip(dlts, varis)]


def _group_norm_pair(o, avg):
    return _group_norm_pairs([o], avg)[0]


def _pair_average_matrix():
    r = np.arange(LANES)
    return jnp.asarray(((r[:, None] // HEAD_DIM) == (r[None, :] // HEAD_DIM)).astype(np.float32) / HEAD_DIM)


def _block_diag_mask():
    r = np.arange(LANES)
    return jnp.asarray(((r[:, None] // HEAD_DIM) == (r[None, :] // HEAD_DIM)).astype(np.float32))


def _prompt_mixer_body(sink_ref, p_ref, kvc_ref, kvp_ref, dmask_ref, qdec_ref, kdec_ref, cdec_ref,
                       bd_ref, avg_ref, gnw_ref, cat_ref, r_ref):
    jb = pl.program_id(1)
    blk = WINDOW
    q_cols = N_ATTN_HEADS * HEAD_DIM
    r_cols = N_RET_HEADS * HEAD_DIM
    kvw = N_KV_HEADS * HEAD_DIM

    @pl.when(jb == 0)
    def _():
        r_ref[...] = jnp.zeros_like(r_ref)

    low = _lane_is_low_half((blk, LANES))
    kband = jnp.concatenate([kvp_ref[:, 0:kvw], kvc_ref[:, 0:kvw]], axis=0)
    vband = jnp.concatenate([kvp_ref[:, kvw:2 * kvw], kvc_ref[:, kvw:2 * kvw]], axis=0)
    low2 = _lane_is_low_half((2 * blk, LANES))
    kswap = pltpu.roll(kband, HEAD_DIM, 1)
    vswap = pltpu.roll(vband, HEAD_DIM, 1)
    qi = lax.broadcasted_iota(jnp.int32, (blk, 2 * blk), 0)
    kj = lax.broadcasted_iota(jnp.int32, (blk, 2 * blk), 1)
    dist = blk + qi - kj
    mask = (dist >= 0) & (dist < WINDOW) & ((kj >= blk) | (jb > 0))
    for kvh in range(N_KV_HEADS):
        keep = low2 if kvh == 0 else jnp.logical_not(low2)
        k2 = jnp.where(keep, kband, kswap).astype(BF16)
        v2 = jnp.where(keep, vband, vswap).astype(BF16)
        pieces = []
        for pr in range(GQA_GROUP // 2):
            c0 = (kvh * (GQA_GROUP // 2) + pr) * LANES
            qp = p_ref[:, c0:c0 + LANES]
            pieces.append(jnp.where(low, qp, jnp.zeros_like(qp)))
            pieces.append(jnp.where(low, jnp.zeros_like(qp), qp))
        qs = jnp.concatenate(pieces, axis=0)
        s = _dot_nt(qs, k2)
        es, inv = [], []
        for hh in range(GQA_GROUP):
            sink = sink_ref[kvh * GQA_GROUP + hh]
            sh = jnp.where(mask, s[hh * blk:(hh + 1) * blk], NEG_INF)
            m = jnp.maximum(jnp.max(sh, axis=-1, keepdims=True), sink)
            e = jnp.exp(sh - m)
            den = jnp.sum(e, axis=-1, keepdims=True) + jnp.exp(sink - m)
            es.append(e.astype(BF16))
            inv.append(1.0 / den)
        o = _dot(jnp.concatenate(es, axis=0), v2)
        for pr in range(GQA_GROUP // 2):
            o_lo = o[(2 * pr) * blk:(2 * pr + 1) * blk] * inv[2 * pr]
            o_hi = o[(2 * pr + 1) * blk:(2 * pr + 2) * blk] * inv[2 * pr + 1]
            c0 = (kvh * (GQA_GROUP // 2) + pr) * LANES
            cat_ref[:, c0:c0 + LANES] = jnp.where(low, o_lo, o_hi).astype(BF16)
    n_pairs = N_RET_HEADS // 2
    pairs = range(n_pairs)
    col = lambda pp, part: slice(q_cols + part * r_cols + pp * LANES, q_cols + part * r_cols + (pp + 1) * LANES)
    pcol = lambda pp: slice(pp * LANES, (pp + 1) * LANES)
    qs = [p_ref[:, col(pp, 0)] for pp in pairs]
    ks = [p_ref[:, col(pp, 1)] for pp in pairs]
    vs = [p_ref[:, col(pp, 2)] for pp in pairs]
    r_prev = [r_ref[0, pp] for pp in pairs]
    zero = jnp.zeros_like(qs[0])
    sc_lo = [_dot_nt(jnp.where(low, qs[pp], zero), ks[pp]) for pp in pairs]
    sc_hi = [_dot_nt(jnp.where(low, zero, qs[pp]), ks[pp]) for pp in pairs]
    qd = [(qs[pp].astype(F32) * qdec_ref[:, pcol(pp)]).astype(BF16) for pp in pairs]
    kd = [(ks[pp].astype(F32) * kdec_ref[:, pcol(pp)]).astype(BF16) for pp in pairs]
    o_cross = [_dot(qd[pp], r_prev[pp].astype(BF16)) for pp in pairs]
    kv_new = [_dot_tn(kd[pp], vs[pp]) for pp in pairs]
    o_lo = [_dot((sc_lo[pp] * dmask_ref[2 * pp]).astype(BF16), vs[pp]) for pp in pairs]
    o_hi = [_dot((sc_hi[pp] * dmask_ref[2 * pp + 1]).astype(BF16), vs[pp]) for pp in pairs]
    for pp in pairs:
        r_ref[0, pp] = cdec_ref[pp] * r_prev[pp] + kv_new[pp] * bd_ref[...]
    o_all = jnp.concatenate([jnp.where(low, o_lo[pp], o_hi[pp]) + o_cross[pp] for pp in pairs], axis=0)
    y_all = _group_norm_pair(o_all, avg_ref[...])
    for pp in pairs:
        gp = p_ref[:, col(pp, 3)].astype(F32)
        y = y_all[pp * blk:(pp + 1) * blk] * gnw_ref[:, pcol(pp)] * _silu(gp)
        cat_ref[:, q_cols + pp * LANES:q_cols + (pp + 1) * LANES] = y.astype(BF16)


def _prompt_mixer(p_p, kv_p, sinks, gn_w, n_pbatch, t_prompt, ret_tabs):
    blk = WINDOW
    nb = t_prompt // blk
    p_cols = p_p.shape[1]
    kv_cols = kv_p.shape[1]
    d_mix = (N_ATTN_HEADS + N_RET_HEADS) * HEAD_DIM
    decay_mask, q_dec, k_dec, chunk_decay = ret_tabs
    n_pairs = N_RET_HEADS // 2
    bd = _block_diag_mask()
    cdec = jnp.stack([bd * chunk_decay[0, pp * LANES:(pp + 1) * LANES][:, None] for pp in range(n_pairs)])
    const2 = lambda shape: pl.BlockSpec(shape, lambda b, j: (0,) * len(shape))
    return pl.pallas_call(
        _prompt_mixer_body,
        grid=(n_pbatch, nb),
        in_specs=[
            pl.BlockSpec(memory_space=pltpu.SMEM),
            pl.BlockSpec((blk, p_cols), lambda b, j: (b * nb + j, 0)),
            pl.BlockSpec((blk, kv_cols), lambda b, j: (b * nb + j, 0)),
            pl.BlockSpec((blk, kv_cols), lambda b, j: (b * nb + jnp.maximum(j - 1, 0), 0)),
            const2((N_RET_HEADS, blk, blk)),
            const2((blk, N_RET_HEADS * HEAD_DIM)),
            const2((blk, N_RET_HEADS * HEAD_DIM)),
            const2((n_pairs, LANES, LANES)),
            const2((LANES, LANES)),
            const2((LANES, LANES)),
            const2((1, N_RET_HEADS * HEAD_DIM)),
        ],
        out_specs=[
            pl.BlockSpec((blk, d_mix), lambda b, j: (b * nb + j, 0)),
            pl.BlockSpec((1, n_pairs, LANES, LANES), lambda b, j: (b, 0, 0, 0)),
        ],
        out_shape=[
            jax.ShapeDtypeStruct((n_pbatch * t_prompt, d_mix), BF16),
            jax.ShapeDtypeStruct((n_pbatch, n_pairs, LANES, LANES), F32),
        ],
        compiler_params=_cparams(("arbitrary", "arbitrary")),
        name="prompt_mixer",
    )(sinks, p_p, kv_p, kv_p, decay_mask, q_dec, k_dec, cdec, bd, _pair_average_matrix(), gn_w.reshape(1, -1))


def _sample_mixer_body(t_new, sink_ref, p_ref, kv_ref, kbuf_ref, vbuf_ref, ret_ref, d4_ref, qdec_ref, kdec_ref,
                       cdec_ref, avg_ref, gnw_ref, cat_ref, knew_ref, vnew_ref, rnew_ref, ka2_ref, va2_ref, qs2_ref):
    w = kbuf_ref.shape[1]
    q_cols = N_ATTN_HEADS * HEAD_DIM
    r_cols = N_RET_HEADS * HEAD_DIM
    kvw = N_KV_HEADS * HEAD_DIM
    n_keys = ka2_ref.shape[1]
    rows_per_head = SUBLANES
    ka2_ref[...] = jnp.zeros_like(ka2_ref)
    va2_ref[...] = jnp.zeros_like(va2_ref)
    qs2_ref[...] = jnp.zeros_like(qs2_ref)
    low4 = _lane_is_low_half((t_new, LANES))
    lowk = _lane_is_low_half((n_keys, LANES))
    n_rows = GQA_GROUP * rows_per_head
    row = lax.broadcasted_iota(jnp.int32, (n_rows, n_keys), 0)
    key = lax.broadcasted_iota(jnp.int32, (n_rows, n_keys), 1)
    t_of_row = row % rows_per_head
    mask = (t_of_row < t_new) & (key > t_of_row) & (key <= t_of_row + WINDOW) & (key < w + t_new)
    head_of_row = lax.broadcasted_iota(jnp.int32, (n_rows, 1), 0) // rows_per_head

    def per_group(g, carry):
        units = range(SAMPLE_GROUP)
        bs = [SAMPLE_GROUP * g + u for u in units]
        items_a = [(u, kvh) for u in units for kvh in range(N_KV_HEADS)]
        items_r = [(u, pp) for u in units for pp in range(N_RET_HEADS // 2)]
        for u in units:
            b, ka_ref, va_ref, qs_ref = bs[u], ka2_ref.at[u], va2_ref.at[u], qs2_ref.at[u]
            ka_ref[0:w, :] = kbuf_ref[b]
            va_ref[0:w, :] = vbuf_ref[b]
            ka_ref[w:w + t_new, :] = kv_ref[b][:, 0:kvw]
            va_ref[w:w + t_new, :] = kv_ref[b][:, kvw:2 * kvw]
            knew_ref[b] = ka_ref[t_new:t_new + w, :]
            vnew_ref[b] = va_ref[t_new:t_new + w, :]
            for h in range(N_ATTN_HEADS):
                c0 = (h // 2) * LANES
                qp = p_ref[b][:, c0:c0 + LANES]
                qs_ref[h * rows_per_head:h * rows_per_head + t_new, :] = jnp.where(
                    low4 if h % 2 == 0 else jnp.logical_not(low4), qp, 0.0)
        kall = [ka2_ref[u] for u in units]
        vall = [va2_ref[u] for u in units]
        kswap = [pltpu.roll(kall[u], HEAD_DIM, 1) for u in units]
        vswap = [pltpu.roll(vall[u], HEAD_DIM, 1) for u in units]
        keep = [lowk, jnp.logical_not(lowk)]
        k2 = {(u, kvh): jnp.where(keep[kvh], kall[u], kswap[u]) for u, kvh in items_a}
        v2 = {(u, kvh): jnp.where(keep[kvh], vall[u], vswap[u]) for u, kvh in items_a}
        sinks = []
        for kvh in range(N_KV_HEADS):
            sink = jnp.zeros((n_rows, 1), F32)
            for hh in range(GQA_GROUP):
                sink = jnp.where(head_of_row == hh, sink_ref[kvh * GQA_GROUP + hh], sink)
            sinks.append(sink)
        s = {it: jnp.where(mask, _dot_nt(qs2_ref[it[0], it[1] * n_rows:(it[1] + 1) * n_rows, :], k2[it], HIGHEST),
                           NEG_INF) for it in items_a}
        m = {it: jnp.maximum(jnp.max(s[it], axis=-1, keepdims=True), sinks[it[1]]) for it in items_a}
        e = {it: jnp.exp(s[it] - m[it]) for it in items_a}
        den = {it: jnp.sum(e[it], axis=-1, keepdims=True) + jnp.exp(sinks[it[1]] - m[it]) for it in items_a}
        o = {it: _dot(e[it], v2[it], HIGHEST) / den[it] for it in items_a}
        for u, kvh in items_a:
            for pr in range(GQA_GROUP // 2):
                o_lo = o[u, kvh][(2 * pr) * rows_per_head:(2 * pr) * rows_per_head + t_new]
                o_hi = o[u, kvh][(2 * pr + 1) * rows_per_head:(2 * pr + 1) * rows_per_head + t_new]
                c0 = (kvh * (GQA_GROUP // 2) + pr) * LANES
                cat_ref[bs[u], :, c0:c0 + LANES] = jnp.where(low4, o_lo, o_hi)
        col = lambda pp, part: slice(q_cols + part * r_cols + pp * LANES, q_cols + part * r_cols + (pp + 1) * LANES)
        pcol = lambda pp: slice(pp * LANES, (pp + 1) * LANES)
        qp = {it: p_ref[bs[it[0]]][:, col(it[1], 0)] for it in items_r}
        kp = {it: p_ref[bs[it[0]]][:, col(it[1], 1)] for it in items_r}
        vp = {it: p_ref[bs[it[0]]][:, col(it[1], 2)] for it in items_r}
        r0 = {it: ret_ref[bs[it[0]], it[1]] for it in items_r}
        q_lo = {it: jnp.where(low4, qp[it], 0.0) for it in items_r}
        q_hi = {it: jnp.where(low4, 0.0, qp[it]) for it in items_r}
        sc_lo = {it: _dot_nt(q_lo[it], kp[it], HIGHEST) * d4_ref[2 * it[1]] for it in items_r}
        sc_hi = {it: _dot_nt(q_hi[it], kp[it], HIGHEST) * d4_ref[2 * it[1] + 1] for it in items_r}
        oc_lo = {it: _dot(q_lo[it] * qdec_ref[:, pcol(it[1])], r0[it], HIGHEST) for it in items_r}
        oc_hi = {it: _dot(q_hi[it] * qdec_ref[:, pcol(it[1])], r0[it], HIGHEST) for it in items_r}
        kv_full = {it: _dot_tn(kp[it] * kdec_ref[:, pcol(it[1])], vp[it], HIGHEST) for it in items_r}
        oi_lo = {it: _dot(sc_lo[it], vp[it], HIGHEST) for it in items_r}
        oi_hi = {it: _dot(sc_hi[it], vp[it], HIGHEST) for it in items_r}
        for it in items_r:
            kv_pair = jnp.concatenate([kv_full[it][0:HEAD_DIM, 0:HEAD_DIM], kv_full[it][HEAD_DIM:, HEAD_DIM:]], axis=0)
            rnew_ref[bs[it[0]], it[1]] = cdec_ref[it[1]] * r0[it] + kv_pair
        o_ret = [jnp.where(low4, oi_lo[it], oi_hi[it]) + jnp.concatenate([oc_lo[it], oc_hi[it]], axis=1)
                 for it in items_r]
        y_ret = _group_norm_pairs(o_ret, avg_ref[...])
        for it, y in zip(items_r, y_ret):
            gp = p_ref[bs[it[0]]][:, col(it[1], 3)]
            cat_ref[bs[it[0]], :, q_cols + it[1] * LANES:q_cols + (it[1] + 1) * LANES] = (
                y * gnw_ref[:, pcol(it[1])] * _silu(gp))
        return carry

    lax.fori_loop(0, kbuf_ref.shape[0] // SAMPLE_GROUP, per_group, 0)


def _sample_mixer(p_s, kv_s, k_buf, v_buf, ret0, sinks, gn_w, ret_tabs):
    nb, t_new, p_cols = p_s.shape
    kv_cols = kv_s.shape[2]
    w = k_buf.shape[1]
    bb = SAMPLE_BATCH_TILE
    n_pairs = N_RET_HEADS // 2
    d_mix = (N_ATTN_HEADS + N_RET_HEADS) * HEAD_DIM
    decay_mask, q_dec, k_dec, chunk_decay = ret_tabs
    n_keys = ((w + t_new + SUBLANES - 1) // SUBLANES) * SUBLANES
    cdec = jnp.stack([jnp.broadcast_to(chunk_decay[0, pp * LANES:(pp + 1) * LANES][:, None], (LANES, HEAD_DIM))
                      for pp in range(n_pairs)])
    const1 = lambda shape: pl.BlockSpec(shape, lambda i: (0,) * len(shape))
    body = functools.partial(_sample_mixer_body, t_new)
    return pl.pallas_call(
        body,
        grid=(nb // bb,),
        in_specs=[
            pl.BlockSpec(memory_space=pltpu.SMEM),
            pl.BlockSpec((bb, t_new, p_cols), lambda i: (i, 0, 0)),
            pl.BlockSpec((bb, t_new, kv_cols), lambda i: (i, 0, 0)),
            pl.BlockSpec((bb, w, LANES), lambda i: (i, 0, 0)),
            pl.BlockSpec((bb, w, LANES), lambda i: (i, 0, 0)),
            pl.BlockSpec((bb, n_pairs, LANES, HEAD_DIM), lambda i: (i, 0, 0, 0)),
            const1((N_RET_HEADS, t_new, t_new)),
            const1((t_new, N_RET_HEADS * HEAD_DIM)),
            const1((t_new, N_RET_HEADS * HEAD_DIM)),
            const1((n_pairs, LANES, HEAD_DIM)),
            const1((LANES, LANES)),
            const1((1, N_RET_HEADS * HEAD_DIM)),
        ],
        out_specs=[
            pl.BlockSpec((bb, t_new, d_mix), lambda i: (i, 0, 0)),
            pl.BlockSpec((bb, w, LANES), lambda i: (i, 0, 0)),
            pl.BlockSpec((bb, w, LANES), lambda i: (i, 0, 0)),
            pl.BlockSpec((bb, n_pairs, LANES, HEAD_DIM), lambda i: (i, 0, 0, 0)),
        ],
        out_shape=[
            jax.ShapeDtypeStruct((nb, t_new, d_mix), F32),
            jax.ShapeDtypeStruct((nb, w, LANES), F32),
            jax.ShapeDtypeStruct((nb, w, LANES), F32),
            jax.ShapeDtypeStruct((nb, n_pairs, LANES, HEAD_DIM), F32),
        ],
        scratch_shapes=[
            pltpu.VMEM((SAMPLE_GROUP, n_keys, LANES), F32),
            pltpu.VMEM((SAMPLE_GROUP, n_keys, LANES), F32),
            pltpu.VMEM((SAMPLE_GROUP, N_ATTN_HEADS * SUBLANES, LANES), F32),
        ],
        compiler_params=_cparams(("arbitrary",)),
        name="sample_mixer",
    )(sinks, p_s, kv_s, k_buf, v_buf, ret0, decay_mask, q_dec, k_dec, cdec, _pair_average_matrix(), gn_w.reshape(1, -1))


def _router_body(geo, xp_ref, xs_ref, catp_ref, cats_ref, wout_ref, g1p_ref, g1s_ref, nw_ref, shp_ref, scp_ref,
                 shs_ref, scs_ref, wr_ref, br_ref, tri_ref, x1_ref, h2_ref, idx_ref, gate_ref, rank_ref, cnt_ref):
    i = pl.program_id(0)
    tv = functools.partial(_tile_vec, i, geo)
    cat = _tile_rows(i, geo, catp_ref, cats_ref)
    x1 = _tile_rows(i, geo, xp_ref, xs_ref) + tv(g1p_ref, g1s_ref) * _dot(cat, wout_ref[...])
    x1_ref[...] = x1
    h2 = _rms_norm(x1, nw_ref[...]) * (1.0 + tv(scp_ref, scs_ref)) + tv(shp_ref, shs_ref)
    h2_ref[...] = h2.astype(BF16).reshape(h2_ref.shape)
    logits = _dot_nt(wr_ref[...], h2, HIGHEST) + br_ref[...]
    e_iota = lax.broadcasted_iota(jnp.int32, logits.shape, 0)
    vals, sels = [], []
    for _ in range(TOP_K):
        m = jnp.max(logits, axis=0, keepdims=True)
        sel = jnp.min(jnp.where(logits == m, e_iota, N_EXPERTS), axis=0, keepdims=True)
        vals.append(m)
        sels.append(sel)
        logits = jnp.where(e_iota == sel, -jnp.inf, logits)
    es = [jnp.exp(v - vals[0]) for v in vals]
    den = es[0]
    for e in es[1:]:
        den = den + e

    @pl.when(i == 0)
    def _():
        cnt_ref[...] = jnp.zeros_like(cnt_ref)

    base = cnt_ref[...]
    for k in range(TOP_K):
        hit = e_iota == sels[k]
        onehot = jnp.where(hit, 1.0, 0.0)
        before = _dot(onehot.astype(BF16), tri_ref[...])
        rank = jnp.sum(jnp.where(hit, base + before, 0.0), axis=0, keepdims=True)
        base = base + jnp.sum(onehot, axis=1, keepdims=True)
        idx_ref[k:k + 1, :] = sels[k]
        gate_ref[k:k + 1, :] = es[k] / den
        rank_ref[k:k + 1, :] = rank.astype(jnp.int32)
    cnt_ref[...] = base


def _outproj_router(x_p, x_s, cat_p, cat_s, w_out_bf16, norm_w, mod_p, mod_s, layer, w_router_t, b_router, geo):
    n_p, d = x_p.shape
    n = n_p + x_s.shape[0]
    d_mix = cat_p.shape[1]
    pv = lambda chunk: _prompt_vec_spec(layer, chunk, d)
    sv = lambda chunk: _sample_vec_spec(layer, chunk, geo, d)
    tile = lambda cols: pl.BlockSpec((TOKEN_TILE, cols), lambda i: (i, 0))
    choice = pl.BlockSpec((TOP_K, TOKEN_TILE), lambda i: (0, i))
    tri = jnp.asarray(np.triu(np.ones((TOKEN_TILE, TOKEN_TILE), np.float32), 1)).astype(BF16)
    return pl.pallas_call(
        functools.partial(_router_body, geo),
        grid=(n // TOKEN_TILE,),
        in_specs=[
            _prompt_rows_spec(geo, d), _sample_rows_spec(geo, d),
            _prompt_rows_spec(geo, d_mix), _sample_rows_spec(geo, d_mix),
            pl.BlockSpec((d_mix, d), lambda i: (0, 0)),
            pv(2), sv(2),
            pl.BlockSpec((1, d), lambda i: (0, 0)),
            pv(3), pv(4), sv(3), sv(4),
            pl.BlockSpec((N_EXPERTS, d), lambda i: (0, 0)),
            pl.BlockSpec((N_EXPERTS, 1), lambda i: (0, 0)),
            pl.BlockSpec((TOKEN_TILE, TOKEN_TILE), lambda i: (0, 0)),
        ],
        out_specs=[
            tile(d),
            pl.BlockSpec((TOKEN_TILE, d // LANES, LANES), lambda i: (i, 0, 0)),
            choice, choice, choice,
            pl.BlockSpec((N_EXPERTS, 1), lambda i: (0, 0)),
        ],
        out_shape=[
            jax.ShapeDtypeStruct((n, d), F32),
            jax.ShapeDtypeStruct((n, d // LANES, LANES), BF16),
            jax.ShapeDtypeStruct((TOP_K, n), jnp.int32),
            jax.ShapeDtypeStruct((TOP_K, n), F32),
            jax.ShapeDtypeStruct((TOP_K, n), jnp.int32),
            jax.ShapeDtypeStruct((N_EXPERTS, 1), F32),
        ],
        compiler_params=_cparams(("arbitrary",)),
        name="outproj_router",
    )(x_p, x_s, cat_p, cat_s, w_out_bf16, mod_p, mod_s, norm_w.reshape(1, d), mod_p, mod_p, mod_s, mod_s,
      w_router_t, b_router.reshape(N_EXPERTS, 1), tri)


def _routing_plan(top_idx, rank, counts_f, tm):
    k, n = top_idx.shape
    n_pairs = k * n
    n_tiles = (n_pairs + N_EXPERTS * (tm - 1)) // tm + 1
    counts = counts_f[:, 0].astype(jnp.int32)
    tiles_per = (counts + tm - 1) // tm
    tile_end = jnp.cumsum(tiles_per)
    tile_start = tile_end - tiles_per
    n_used = tile_end[-1]
    n_slots = n_tiles * tm

    def lookup(table, index):
        ids = jnp.arange(table.shape[0], dtype=jnp.int32).reshape((-1,) + (1,) * index.ndim)
        return jnp.sum(jnp.where(index[None] == ids, table.reshape(ids.shape), 0), axis=0)

    slot_of_pair = lookup(tile_start * tm, top_idx) + rank
    pad_counts = jnp.concatenate([tiles_per * tm - counts, (n_slots - n_used * tm).reshape(1)])
    pad_end = jnp.cumsum(pad_counts)
    first_pad_slot = jnp.concatenate([tile_start * tm + counts, (n_used * tm).reshape(1)])
    k_pad = jnp.arange(n_slots - n_pairs, dtype=jnp.int32)
    seg = jnp.sum(k_pad[:, None] >= pad_end[None, :], axis=1).astype(jnp.int32)
    pad_slot = k_pad + lookup(first_pad_slot - (pad_end - pad_counts), seg)
    token_of_pair = jnp.broadcast_to(jnp.arange(n, dtype=jnp.int32)[None, :], (k, n))
    keys = jnp.concatenate([slot_of_pair.reshape(-1), pad_slot])
    vals = jnp.concatenate([token_of_pair.reshape(-1), jnp.zeros_like(pad_slot)])
    _, slot_rows = lax.sort((keys, vals), num_keys=1)
    tile_ids = jnp.arange(n_tiles, dtype=jnp.int32)
    tile_expert = jnp.minimum(jnp.sum(tile_ids[:, None] >= tile_end[None, :], axis=1), N_EXPERTS - 1).astype(jnp.int32)
    last_expert = tile_expert[jnp.maximum(n_used - 1, 0)]
    tile_expert = jnp.where(tile_ids < n_used, tile_expert, last_expert)
    return tile_expert, n_used.reshape(1).astype(jnp.int32), slot_rows.reshape(n_tiles, 1, tm), slot_of_pair


def _issue_row_copies(src_hbm, row_of, dst_row, sem, n_rows):
    def group(g, c):
        for u in range(DMA_UNROLL):
            r = g * DMA_UNROLL + u
            pltpu.make_async_copy(src_hbm.at[row_of(r)], dst_row(r), sem).start(priority=u % 2)
        return c
    lax.fori_loop(0, n_rows // DMA_UNROLL, group, 0)


def _issue_row_copies_static(src_hbm, row_of, dst_row, sem, lo, hi):
    for r in range(lo, hi):
        pltpu.make_async_copy(src_hbm.at[row_of(r)], dst_row(r), sem).start(priority=r % 2)


def _wait_row_copies(src_hbm, dst_buf, sem):
    pltpu.make_async_copy(src_hbm.at[pl.ds(0, dst_buf.shape[0])], dst_buf, sem).wait()


def _expert_body(te_ref, nt_ref, rows0_ref, rows1_ref, rows2_ref, h_hbm, wgu_ref, bg_ref, bu_ref, wd_ref, bd_ref,
                 perm_ref, y_ref, xbuf, sems, wgu_s, wd_s, act_s):
    j = pl.program_id(0)
    n_tiles = pl.num_programs(0)
    n_used = nt_ref[0]
    slot = j % ROW_BUFFERS
    far = (j + ROW_BUFFERS - 1) % ROW_BUFFERS
    tm = EXPERT_TILE
    f2 = wgu_ref.shape[2]
    n_chunks = f2 // MXU_WIDTH
    half = MXU_WIDTH // 2
    next_row = lambda r: rows2_ref[0, 0, r]
    next_dst = lambda r: xbuf.at[far, r]

    @pl.when(j == 0)
    def _():
        _issue_row_copies(h_hbm, lambda r: rows0_ref[0, 0, r], lambda r: xbuf.at[0, r], sems.at[0], tm)
        _issue_row_copies(h_hbm, lambda r: rows1_ref[0, 0, r], lambda r: xbuf.at[1, r], sems.at[1], tm)

    new_expert = jnp.logical_or(j == 0, te_ref[j] != te_ref[jnp.maximum(j - 1, 0)])

    @pl.when(jnp.logical_and(j < n_used, new_expert))
    def _():
        for c in range(n_chunks):
            cols = slice(c * MXU_WIDTH, (c + 1) * MXU_WIDTH)
            wgu_s[:, cols] = _dot(wgu_ref[0, :, cols].astype(BF16), perm_ref[...]).astype(BF16)
        wd_s[...] = wd_ref[0].astype(BF16)

    @pl.when(j < n_used)
    def _():
        _wait_row_copies(h_hbm, xbuf.at[slot], sems.at[slot])
        x = xbuf[slot].reshape(tm, wgu_ref.shape[1])
        issue_chunks = n_chunks // 2
        per_chunk = tm // issue_chunks
        for c in range(n_chunks):
            if c < issue_chunks:
                _issue_row_copies_static(h_hbm, next_row, next_dst, sems.at[far], c * per_chunk, (c + 1) * per_chunk)
            gu = _dot(x, wgu_s[:, c * MXU_WIDTH:(c + 1) * MXU_WIDTH])
            glu = jnp.minimum(gu[:, :half] + bg_ref[0, :, c * half:(c + 1) * half], SWIGLU_LIMIT)
            lin = jnp.clip(gu[:, half:] + bu_ref[0, :, c * half:(c + 1) * half], -SWIGLU_LIMIT, SWIGLU_LIMIT)
            act_s[:, c * half:(c + 1) * half] = (glu * jax.nn.sigmoid(SWIGLU_ALPHA * glu) * (lin + 1.0)).astype(BF16)
        y = _dot(act_s[...], wd_s[...]) + bd_ref[0]
        y_ref[...] = y.astype(BF16).reshape(y_ref.shape)

    @pl.when(j >= n_used)
    def _():
        _wait_row_copies(h_hbm, xbuf.at[slot], sems.at[slot])
        _issue_row_copies(h_hbm, next_row, next_dst, sems.at[far], tm)
        y_ref[...] = jnp.zeros_like(y_ref)

    @pl.when(j == n_tiles - 1)
    def _():
        for ahead in range(1, ROW_BUFFERS):
            late = (j + ahead) % ROW_BUFFERS
            _wait_row_copies(h_hbm, xbuf.at[late], sems.at[late])


def _deinterleave_matrix():
    m = np.zeros((MXU_WIDTH, MXU_WIDTH), np.float32)
    j = np.arange(MXU_WIDTH // 2)
    m[2 * j, j] = 1.0
    m[2 * j + 1, MXU_WIDTH // 2 + j] = 1.0
    return jnp.asarray(m).astype(BF16)


def _routed_experts(h2_rows, tile_expert, n_used, slot_rows, w_gate_up, bg, bu, w_down, bd, layer):
    d = w_gate_up.shape[2]
    f2 = w_gate_up.shape[3]
    f = f2 // 2
    tm = EXPERT_TILE
    n_tiles = slot_rows.shape[0]
    row_tile = h2_rows.shape[1:]
    assert ROW_BUFFERS == 3 and n_tiles >= ROW_BUFFERS
    by_expert = lambda shape: pl.BlockSpec(shape, lambda j, te, nt: (te[j], 0, 0))
    layer_expert = lambda shape: pl.BlockSpec((None,) + shape, lambda j, te, nt: (layer, te[j], 0, 0))
    rows_ahead = lambda k: pl.BlockSpec((1, 1, tm), lambda j, te, nt: (jnp.minimum(j + k, n_tiles - 1), 0, 0),
                                        memory_space=pltpu.SMEM)
    grid_spec = pltpu.PrefetchScalarGridSpec(
        num_scalar_prefetch=2,
        grid=(n_tiles,),
        in_specs=[
            rows_ahead(0), rows_ahead(1), rows_ahead(2),
            pl.BlockSpec(memory_space=pl.ANY),
            layer_expert((1, d, f2)),
            by_expert((1, 1, f)), by_expert((1, 1, f)),
            layer_expert((1, f, d)),
            by_expert((1, 1, d)),
            pl.BlockSpec((MXU_WIDTH, MXU_WIDTH), lambda j, te, nt: (0, 0)),
        ],
        out_specs=pl.BlockSpec((tm,) + row_tile, lambda j, te, nt: (j, 0, 0)),
        scratch_shapes=[
            pltpu.VMEM((ROW_BUFFERS, tm) + row_tile, BF16),
            pltpu.SemaphoreType.DMA((ROW_BUFFERS,)),
            pltpu.VMEM((d, f2), BF16),
            pltpu.VMEM((f, d), BF16),
            pltpu.VMEM((tm, f), BF16),
        ],
    )
    return pl.pallas_call(
        _expert_body,
        grid_spec=grid_spec,
        out_shape=jax.ShapeDtypeStruct((n_tiles * tm,) + row_tile, BF16),
        compiler_params=_cparams(("arbitrary",)),
        name="routed_experts",
    )(tile_expert, n_used, slot_rows, slot_rows, slot_rows, h2_rows, w_gate_up, bg, bu, w_down, bd,
      _deinterleave_matrix())


def _combine_body(geo, final_norm, slots_cur_ref, slots_nxt_ref, x1_ref, gate_ref, g2p_ref, g2s_ref, nfw_ref, y_hbm,
                  outp_ref, outs_ref, ybuf_even, ybuf_odd, sems):
    i = pl.program_id(0)
    n_steps = pl.num_programs(0)
    tm = TOKEN_TILE

    @pl.when(i == 0)
    def _():
        for k in range(TOP_K):
            _issue_row_copies(y_hbm, lambda r, k=k: slots_cur_ref[0, k, r], lambda r, k=k: ybuf_even.at[k, r],
                              sems.at[0], tm)

    def step(cur, nxt, sem_cur, sem_nxt):
        for k in range(TOP_K):
            _wait_row_copies(y_hbm, cur.at[k], sem_cur)
        for k in range(TOP_K):
            _issue_row_copies_static(y_hbm, lambda r, k=k: slots_nxt_ref[0, k, r], lambda r, k=k: nxt.at[k, r],
                                     sem_nxt, 0, tm)
        rows = lambda k: cur[k].reshape(tm, x1_ref.shape[1]).astype(F32)
        acc = rows(0) * gate_ref[:, 0:1]
        for k in range(1, TOP_K):
            acc = acc + rows(k) * gate_ref[:, k:k + 1]
        x2 = x1_ref[...] + _tile_vec(i, geo, g2p_ref, g2s_ref) * acc
        if final_norm:
            x2 = _rms_norm(x2, nfw_ref[...])

        @pl.when(i < geo.n_ptiles)
        def _():
            outp_ref[...] = x2

        @pl.when(i >= geo.n_ptiles)
        def _():
            outs_ref[...] = x2

        @pl.when(i == n_steps - 1)
        def _():
            for k in range(TOP_K):
                _wait_row_copies(y_hbm, nxt.at[k], sem_nxt)

    @pl.when(i % 2 == 0)
    def _():
        step(ybuf_even, ybuf_odd, sems.at[0], sems.at[1])

    @pl.when(i % 2 == 1)
    def _():
        step(ybuf_odd, ybuf_even, sems.at[1], sems.at[0])


def _combine(x1, y_rows, slot_of_pair, gates_t, mod_p, mod_s, layer, final_w, geo, final_norm):
    n, d = x1.shape
    tm = TOKEN_TILE
    n_steps = n // tm
    n_p = geo.n_ptiles * tm
    slots = slot_of_pair.reshape(TOP_K, n_steps, tm).transpose(1, 0, 2)
    return pl.pallas_call(
        functools.partial(_combine_body, geo, final_norm),
        grid=(n_steps,),
        in_specs=[
            pl.BlockSpec((1, TOP_K, tm), lambda i: (i, 0, 0), memory_space=pltpu.SMEM),
            pl.BlockSpec((1, TOP_K, tm), lambda i: (jnp.minimum(i + 1, n_steps - 1), 0, 0), memory_space=pltpu.SMEM),
            pl.BlockSpec((tm, d), lambda i: (i, 0)),
            pl.BlockSpec((tm, TOP_K), lambda i: (i, 0)),
            _prompt_vec_spec(layer, 5, d),
            _sample_vec_spec(layer, 5, geo, d),
            pl.BlockSpec((1, d), lambda i: (0, 0)),
            pl.BlockSpec(memory_space=pl.ANY),
        ],
        out_specs=[_prompt_rows_spec(geo, d), _sample_rows_spec(geo, d)],
        out_shape=[jax.ShapeDtypeStruct((n_p, d), F32), jax.ShapeDtypeStruct((n - n_p, d), F32)],
        scratch_shapes=[pltpu.VMEM((TOP_K, tm) + y_rows.shape[1:], y_rows.dtype),
                        pltpu.VMEM((TOP_K, tm) + y_rows.shape[1:], y_rows.dtype),
                        pltpu.SemaphoreType.DMA((2,))],
        compiler_params=_cparams(("arbitrary",)),
        name="moe_combine",
    )(slots, slots, x1, gates_t, mod_p, mod_s, final_w.reshape(1, d), y_rows)


def kernel(x_prompt, x_sample, state_swa_k, state_swa_v, state_ret, c_prompt, c_sample, norm_mix_w, w_ada, b_ada, w_in, attn_sinks, ret_gn_w, w_out, norm_ffn_w, w_router, b_router, w_gate_up, b_gate_up, w_down, b_down, norm_final_w):
    n_pbatch, t_prompt, d = x_prompt.shape
    n_sbatch, t_sample, _ = x_sample.shape
    depth = w_in.shape[0]
    n_p = n_pbatch * t_prompt
    n_s = n_sbatch * t_sample
    win = state_swa_k.shape[2]
    assert d == (N_ATTN_HEADS + N_RET_HEADS) * HEAD_DIM and w_gate_up.shape[1] == N_EXPERTS
    assert t_prompt % TOKEN_TILE == 0 and n_s % TOKEN_TILE == 0 and TOKEN_TILE % t_sample == 0
    assert t_prompt % WINDOW == 0 and RET_CHUNK == WINDOW and win == WINDOW and t_sample <= SUBLANES
    assert n_sbatch % SAMPLE_BATCH_TILE == 0 and n_pbatch <= SUBLANES
    assert w_gate_up.shape[3] % MXU_WIDTH == 0 and EXPERT_TILE % DMA_UNROLL == 0 and TOKEN_TILE % DMA_UNROLL == 0
    geo = _Geometry(n_p // TOKEN_TILE, t_prompt // TOKEN_TILE, n_pbatch)

    c_all = jnp.concatenate([c_prompt, jnp.zeros((SUBLANES - n_pbatch, d), F32), c_sample], axis=0)
    mod = _ada_modulation(c_all, w_ada, b_ada)
    mod_p = mod[:, :SUBLANES]
    mod_s = jnp.repeat(mod[:, SUBLANES:], t_sample, axis=1)

    rope_tab = _rope_tables(t_prompt, t_sample)
    ret_tabs_p = _retention_tables(RET_CHUNK)
    ret_tabs_s = _retention_tables(t_sample)
    f = w_gate_up.shape[3] // 2

    x_p = x_prompt.reshape(n_p, d)
    x_s = x_sample.reshape(n_s, d)
    kp_l, vp_l, rp_l, ks_l, vs_l, rs_l = [], [], [], [], [], []
    for l in range(depth):
        p_all, kv_all = _input_projection(x_p, x_s, norm_mix_w[l], mod_p, mod_s, l, w_in[l].astype(BF16), rope_tab, geo)
        cat_p, r_pairs = _prompt_mixer(p_all, kv_all, attn_sinks[l], ret_gn_w[l], n_pbatch, t_prompt, ret_tabs_p)
        p_s = p_all[n_p:].astype(F32)
        kv_s = kv_all[n_p:]
        kv_p = kv_all[:n_p]
        cat_s, k_new, v_new, r_new = _sample_mixer(
            p_s.reshape(n_sbatch, t_sample, -1), kv_s.reshape(n_sbatch, t_sample, -1),
            state_swa_k[l].reshape(n_sbatch, win, LANES), state_swa_v[l].reshape(n_sbatch, win, LANES),
            state_ret[l].reshape(n_sbatch, N_RET_HEADS // 2, LANES, HEAD_DIM), attn_sinks[l], ret_gn_w[l], ret_tabs_s)
        x1, h2_rows, top_idx, gates, rank, counts = _outproj_router(
            x_p, x_s, cat_p, cat_s.reshape(n_s, d), w_out[l].astype(BF16), norm_ffn_w[l], mod_p, mod_s, l,
            w_router[l].T, b_router[l], geo)
        tile_expert, n_used, slot_rows, slot_of_pair = _routing_plan(top_idx, rank, counts, EXPERT_TILE)
        b_gu = b_gate_up[l].reshape(N_EXPERTS, 1, f, 2)
        y_rows = _routed_experts(h2_rows, tile_expert, n_used, slot_rows, w_gate_up,
                                 b_gu[..., 0], b_gu[..., 1], w_down, b_down[l][:, None, :], l)
        x_p, x_s = _combine(x1, y_rows, slot_of_pair, gates.T, mod_p, mod_s, l, norm_final_w, geo,
                            final_norm=(l == depth - 1))

        kv_last = kv_p.reshape(n_pbatch, t_prompt, -1)[:, t_prompt - WINDOW:]
        kv_last = kv_last.reshape(n_pbatch, WINDOW, 2, N_KV_HEADS, HEAD_DIM)
        kp_l.append(kv_last[:, :, 0])
        vp_l.append(kv_last[:, :, 1])
        rp_l.append(jnp.stack([r_pairs[:, :, :HEAD_DIM, :HEAD_DIM], r_pairs[:, :, HEAD_DIM:, HEAD_DIM:]], axis=2)
                    .reshape(n_pbatch, N_RET_HEADS, HEAD_DIM, HEAD_DIM))
        ks_l.append(k_new.reshape(n_sbatch, win, N_KV_HEADS, HEAD_DIM))
        vs_l.append(v_new.reshape(n_sbatch, win, N_KV_HEADS, HEAD_DIM))
        rs_l.append(r_new.reshape(n_sbatch, N_RET_HEADS, HEAD_DIM, HEAD_DIM))
    return (x_p.reshape(n_pbatch, t_prompt, d), x_s.reshape(n_sbatch, t_sample, d), jnp.stack(kp_l), jnp.stack(vp_l),
            jnp.stack(rp_l), jnp.stack(ks_l), jnp.stack(vs_l), jnp.stack(rs_l))
```

```python
import functools

import numpy as np
import jax
import jax.numpy as jnp
from jax import lax
from jax.experimental import pallas as pl
from jax.experimental.pallas import tpu as pltpu

F32 = jnp.float32
BF16 = jnp.bfloat16
HIGHEST = lax.Precision.HIGHEST

HEAD_DIM = 64
N_ATTN_HEADS = 8
N_KV_HEADS = 2
GQA_GROUP = N_ATTN_HEADS // N_KV_HEADS
WINDOW = 128
ROPE_THETA = 500000.0
ROPE_DIMS = HEAD_DIM // 4
N_RET_HEADS = 8
RET_CHUNK = 128
RET_THETA = 10000.0
N_EXPERTS = 32
TOP_K = 4
SWIGLU_LIMIT = 7.0
SWIGLU_ALPHA = 1.702
NORM_EPS = 1e-5
PAST_LEN = 16384

LANES = 128
SUBLANES = 8
MXU_WIDTH = 256
VMEM_LIMIT_BYTES = 56 * 1024 * 1024

TOKEN_TILE = 256
EXPERT_TILE = 256
SAMPLE_BATCH_TILE = 8
SAMPLE_GROUP = 4
DMA_UNROLL = 8
ROW_BUFFERS = 3

NEG_INF = -1e30


def _cparams(semantics):
    return pltpu.CompilerParams(dimension_semantics=semantics, vmem_limit_bytes=VMEM_LIMIT_BYTES)


def _lane_is_low_half(shape):
    return lax.broadcasted_iota(jnp.int32, shape, len(shape) - 1) < HEAD_DIM


def _dot(a, b, precision=None):
    return jnp.dot(a, b, preferred_element_type=F32, precision=precision)


def _dot_nt(a, b, precision=None):
    return lax.dot_general(a, b, (((1,), (1,)), ((), ())), preferred_element_type=F32, precision=precision)


def _dot_tn(a, b, precision=None):
    return lax.dot_general(a, b, (((0,), (0,)), ((), ())), preferred_element_type=F32, precision=precision)


def _rms_norm(x, w):
    return x * lax.rsqrt(jnp.mean(x * x, axis=-1, keepdims=True) + NORM_EPS) * w


def _silu(x):
    return x * jax.nn.sigmoid(x)


def _ada_body(c_ref, w_ref, b_ref, o_ref):
    a = _silu(c_ref[...])
    o_ref[0] = _dot(a, w_ref[0], HIGHEST) + b_ref[0]


def _ada_modulation(c_all, w_ada, b_ada):
    depth, d, cols = w_ada.shape
    rows = c_all.shape[0]
    tn = 1024
    return pl.pallas_call(
        _ada_body,
        grid=(depth, cols // tn),
        in_specs=[
            pl.BlockSpec((rows, d), lambda l, j: (0, 0)),
            pl.BlockSpec((1, d, tn), lambda l, j: (l, 0, j)),
            pl.BlockSpec((1, 1, tn), lambda l, j: (l, 0, j)),
        ],
        out_specs=pl.BlockSpec((1, rows, tn), lambda l, j: (l, 0, j)),
        out_shape=jax.ShapeDtypeStruct((depth, rows, cols), F32),
        compiler_params=_cparams(("arbitrary", "arbitrary")),
        name="ada_modulation",
    )(c_all, w_ada, b_ada.reshape(depth, 1, cols))


class _Geometry:
    def __init__(self, n_ptiles, tiles_per_batch, n_pbatch):
        self.n_ptiles = n_ptiles
        self.tiles_per_batch = tiles_per_batch
        self.n_pbatch = n_pbatch


def _tile_vec(i, geo, prompt_ref, sample_ref):
    b = jnp.minimum(i // geo.tiles_per_batch, geo.n_pbatch - 1)
    return jnp.where(i >= geo.n_ptiles, sample_ref[...], prompt_ref[pl.ds(b, 1), :])


def _tile_rows(i, geo, prompt_ref, sample_ref):
    return jnp.where(i >= geo.n_ptiles, sample_ref[...].astype(prompt_ref.dtype), prompt_ref[...])


def _prompt_vec_spec(layer, chunk, d):
    return pl.BlockSpec((None, SUBLANES, d), lambda i: (layer, 0, chunk))


def _sample_vec_spec(layer, chunk, geo, d):
    return pl.BlockSpec((None, TOKEN_TILE, d), lambda i: (layer, jnp.maximum(i - geo.n_ptiles, 0), chunk))


def _prompt_rows_spec(geo, cols):
    return pl.BlockSpec((TOKEN_TILE, cols), lambda i: (jnp.minimum(i, geo.n_ptiles - 1), 0))


def _sample_rows_spec(geo, cols):
    return pl.BlockSpec((TOKEN_TILE, cols), lambda i: (jnp.maximum(i - geo.n_ptiles, 0), 0))


def _rotate(xc, tab_ref, base, shift):
    return (xc * tab_ref[base] + pltpu.roll(xc, shift, 1) * tab_ref[base + 1]
            + pltpu.roll(xc, LANES - shift, 1) * tab_ref[base + 2])


def _inproj_body(geo, xp_ref, xs_ref, nw_ref, shp_ref, scp_ref, shs_ref, scs_ref, w_ref, tab_ref, p_ref, kv_ref):
    i = pl.program_id(0)
    x = _tile_rows(i, geo, xp_ref, xs_ref)
    h = _rms_norm(x, nw_ref[...]) * (1.0 + _tile_vec(i, geo, scp_ref, scs_ref)) + _tile_vec(i, geo, shp_ref, shs_ref)
    proj = _dot(h.astype(BF16), w_ref[...])
    q_cols = N_ATTN_HEADS * HEAD_DIM
    kv_cols = N_KV_HEADS * HEAD_DIM
    r_cols = N_RET_HEADS * HEAD_DIM
    o_ka = q_cols
    o_va = o_ka + kv_cols
    o_qr = o_va + kv_cols
    o_kr = o_qr + r_cols
    o_vr = o_kr + r_cols
    o_g = o_vr + r_cols
    attn_scale = HEAD_DIM ** -0.5
    ret_scale = HEAD_DIM ** -0.5
    half_a = ROPE_DIMS // 2
    half_r = HEAD_DIM // 2

    def emit(p_ref, kv_ref):
        dt = p_ref.dtype
        for c in range(q_cols // LANES):
            xc = proj[:, c * LANES:(c + 1) * LANES]
            p_ref[:, c * LANES:(c + 1) * LANES] = (_rotate(xc, tab_ref, 0, half_a) * attn_scale).astype(dt)
        kv_ref[:, 0:kv_cols] = _rotate(proj[:, o_ka:o_ka + kv_cols], tab_ref, 0, half_a)
        kv_ref[:, kv_cols:2 * kv_cols] = proj[:, o_va:o_va + kv_cols]
        for c in range(r_cols // LANES):
            xq = proj[:, o_qr + c * LANES:o_qr + (c + 1) * LANES]
            xk = proj[:, o_kr + c * LANES:o_kr + (c + 1) * LANES]
            p_ref[:, q_cols + c * LANES:q_cols + (c + 1) * LANES] = _rotate(xq, tab_ref, 3, half_r).astype(dt)
            p_ref[:, q_cols + r_cols + c * LANES:q_cols + r_cols + (c + 1) * LANES] = (
                _rotate(xk, tab_ref, 3, half_r) * ret_scale).astype(dt)
        p_ref[:, q_cols + 2 * r_cols:q_cols + 3 * r_cols] = proj[:, o_vr:o_vr + r_cols].astype(dt)
        p_ref[:, q_cols + 3 * r_cols:q_cols + 4 * r_cols] = proj[:, o_g:o_g + r_cols].astype(dt)

    emit(p_ref, kv_ref)


def _input_projection(x_p, x_s, norm_w, mod_p, mod_s, layer, w_in_bf16, rope_tab, geo):
    n_p, d = x_p.shape
    n_s = x_s.shape[0]
    in_cols = w_in_bf16.shape[1]
    kv_cols = 2 * N_KV_HEADS * HEAD_DIM
    p_cols = in_cols - kv_cols
    tpb = geo.tiles_per_batch
    tab_index = lambda i: (0, jnp.where(i >= geo.n_ptiles, tpb, i % tpb), 0)
    return pl.pallas_call(
        functools.partial(_inproj_body, geo),
        grid=((n_p + n_s) // TOKEN_TILE,),
        in_specs=[
            _prompt_rows_spec(geo, d), _sample_rows_spec(geo, d),
            pl.BlockSpec((1, d), lambda i: (0, 0)),
            _prompt_vec_spec(layer, 0, d), _prompt_vec_spec(layer, 1, d),
            _sample_vec_spec(layer, 0, geo, d), _sample_vec_spec(layer, 1, geo, d),
            pl.BlockSpec((d, in_cols), lambda i: (0, 0)),
            pl.BlockSpec((6, TOKEN_TILE, LANES), tab_index),
        ],
        out_specs=[
            pl.BlockSpec((TOKEN_TILE, p_cols), lambda i: (i, 0)),
            pl.BlockSpec((TOKEN_TILE, kv_cols), lambda i: (i, 0)),
        ],
        out_shape=[
            jax.ShapeDtypeStruct((n_p + n_s, p_cols), BF16),
            jax.ShapeDtypeStruct((n_p + n_s, kv_cols), F32),
        ],
        compiler_params=_cparams(("arbitrary",)),
        name="input_projection",
    )(x_p, x_s, norm_w.reshape(1, d), mod_p, mod_p, mod_s, mod_s, w_in_bf16, rope_tab)


def _rope_tables(t_prompt, t_sample):
    pos = jnp.concatenate([jnp.arange(t_prompt, dtype=jnp.int32),
                           PAST_LEN + (jnp.arange(TOKEN_TILE, dtype=jnp.int32) % t_sample)])
    d = np.arange(LANES) % HEAD_DIM

    def tables(n_dims, theta):
        half = n_dims // 2
        freqs = jnp.power(jnp.float32(theta), -jnp.arange(half, dtype=jnp.float32) / half)
        ang = pos.astype(jnp.float32)[:, None] * freqs[None, :]
        cos, sin = jnp.cos(ang), jnp.sin(ang)
        fidx = np.where(d < n_dims, d % half, 0)
        cos_l = jnp.where(jnp.asarray(d < n_dims)[None, :], cos[:, fidx], 1.0)
        sin_l = sin[:, fidx]
        upper = jnp.asarray((d >= half) & (d < n_dims))[None, :]
        lower = jnp.asarray(d < half)[None, :]
        return [cos_l, jnp.where(upper, sin_l, 0.0), jnp.where(lower, -sin_l, 0.0)]

    return jnp.stack(tables(ROPE_DIMS, ROPE_THETA) + tables(HEAD_DIM, RET_THETA)).astype(F32)


def _retention_tables(c):
    h = N_RET_HEADS
    log_gamma = jnp.log(1.0 - jnp.power(2.0, -5.0 - jnp.arange(h, dtype=jnp.float32)))
    idx = jnp.arange(c, dtype=jnp.float32)
    diff = idx[:, None] - idx[None, :]
    decay_mask = jnp.where(diff >= 0, jnp.exp(log_gamma[:, None, None] * jnp.maximum(diff, 0.0)), 0.0)
    k_dec = jnp.exp(log_gamma[None, :] * (c - 1 - idx)[:, None])
    q_dec = jnp.exp(log_gamma[None, :] * (idx + 1.0)[:, None])
    chunk_decay = jnp.exp(log_gamma * c)
    rep = lambda a: jnp.repeat(a, HEAD_DIM, axis=-1)
    return decay_mask.astype(F32), rep(q_dec).astype(F32), rep(k_dec).astype(F32), rep(chunk_decay[None, :]).astype(F32)


def _group_norm_pairs(items, avg):
    avg_b = avg.astype(BF16)

    def block_means(xs):
        his = [x.astype(BF16) for x in xs]
        los = [(x - hi.astype(F32)).astype(BF16) for x, hi in zip(xs, his)]
        return [_dot(hi, avg_b) + _dot(lo, avg_b) for hi, lo in zip(his, los)]

    dlts = [o - mu for o, mu in zip(items, block_means(items))]
    varis = block_means([d * d for d in dlts])
    return [d * lax.rsqrt(v + NORM_EPS) for d, v in zip(dlts, varis)]


def _group_norm_pair(o, avg):
    return _group_norm_pairs([o], avg)[0]


def _pair_average_matrix():
    r = np.arange(LANES)
    return jnp.asarray(((r[:, None] // HEAD_DIM) == (r[None, :] // HEAD_DIM)).astype(np.float32) / HEAD_DIM)


def _block_diag_mask():
    r = np.arange(LANES)
    return jnp.asarray(((r[:, None] // HEAD_DIM) == (r[None, :] // HEAD_DIM)).astype(np.float32))


def _prompt_mixer_body(sink_ref, p_ref, kvc_ref, kvp_ref, dmask_ref, qdec_ref, kdec_ref, cdec_ref,
                       bd_ref, avg_ref, gnw_ref, cat_ref, r_ref):
    jb = pl.program_id(1)
    blk = WINDOW
    q_cols = N_ATTN_HEADS * HEAD_DIM
    r_cols = N_RET_HEADS * HEAD_DIM
    kvw = N_KV_HEADS * HEAD_DIM

    @pl.when(jb == 0)
    def _():
        r_ref[...] = jnp.zeros_like(r_ref)

    low = _lane_is_low_half((blk, LANES))
    kband = jnp.concatenate([kvp_ref[:, 0:kvw], kvc_ref[:, 0:kvw]], axis=0)
    vband = jnp.concatenate([kvp_ref[:, kvw:2 * kvw], kvc_ref[:, kvw:2 * kvw]], axis=0)
    low2 = _lane_is_low_half((2 * blk, LANES))
    kswap = pltpu.roll(kband, HEAD_DIM, 1)
    vswap = pltpu.roll(vband, HEAD_DIM, 1)
    qi = lax.broadcasted_iota(jnp.int32, (blk, 2 * blk), 0)
    kj = lax.broadcasted_iota(jnp.int32, (blk, 2 * blk), 1)
    dist = blk + qi - kj
    mask = (dist >= 0) & (dist < WINDOW) & ((kj >= blk) | (jb > 0))
    for kvh in range(N_KV_HEADS):
        keep = low2 if kvh == 0 else jnp.logical_not(low2)
        k2 = jnp.where(keep, kband, kswap).astype(BF16)
        v2 = jnp.where(keep, vband, vswap).astype(BF16)
        pieces = []
        for pr in range(GQA_GROUP // 2):
            c0 = (kvh * (GQA_GROUP // 2) + pr) * LANES
            qp = p_ref[:, c0:c0 + LANES]
            pieces.append(jnp.where(low, qp, jnp.zeros_like(qp)))
            pieces.append(jnp.where(low, jnp.zeros_like(qp), qp))
        qs = jnp.concatenate(pieces, axis=0)
        s = _dot_nt(qs, k2)
        es, inv = [], []
        for hh in range(GQA_GROUP):
            sink = sink_ref[kvh * GQA_GROUP + hh]
            sh = jnp.where(mask, s[hh * blk:(hh + 1) * blk], NEG_INF)
            m = jnp.maximum(jnp.max(sh, axis=-1, keepdims=True), sink)
            e = jnp.exp(sh - m)
            den = jnp.sum(e, axis=-1, keepdims=True) + jnp.exp(sink - m)
            es.append(e.astype(BF16))
            inv.append(1.0 / den)
        o = _dot(jnp.concatenate(es, axis=0), v2)
        for pr in range(GQA_GROUP // 2):
            o_lo = o[(2 * pr) * blk:(2 * pr + 1) * blk] * inv[2 * pr]
            o_hi = o[(2 * pr + 1) * blk:(2 * pr + 2) * blk] * inv[2 * pr + 1]
            c0 = (kvh * (GQA_GROUP // 2) + pr) * LANES
            cat_ref[:, c0:c0 + LANES] = jnp.where(low, o_lo, o_hi).astype(BF16)
    n_pairs = N_RET_HEADS // 2
    pairs = range(n_pairs)
    col = lambda pp, part: slice(q_cols + part * r_cols + pp * LANES, q_cols + part * r_cols + (pp + 1) * LANES)
    pcol = lambda pp: slice(pp * LANES, (pp + 1) * LANES)
    qs = [p_ref[:, col(pp, 0)] for pp in pairs]
    ks = [p_ref[:, col(pp, 1)] for pp in pairs]
    vs = [p_ref[:, col(pp, 2)] for pp in pairs]
    r_prev = [r_ref[0, pp] for pp in pairs]
    zero = jnp.zeros_like(qs[0])
    sc_lo = [_dot_nt(jnp.where(low, qs[pp], zero), ks[pp]) for pp in pairs]
    sc_hi = [_dot_nt(jnp.where(low, zero, qs[pp]), ks[pp]) for pp in pairs]
    qd = [(qs[pp].astype(F32) * qdec_ref[:, pcol(pp)]).astype(BF16) for pp in pairs]
    kd = [(ks[pp].astype(F32) * kdec_ref[:, pcol(pp)]).astype(BF16) for pp in pairs]
    o_cross = [_dot(qd[pp], r_prev[pp].astype(BF16)) for pp in pairs]
    kv_new = [_dot_tn(kd[pp], vs[pp]) for pp in pairs]
    o_lo = [_dot((sc_lo[pp] * dmask_ref[2 * pp]).astype(BF16), vs[pp]) for pp in pairs]
    o_hi = [_dot((sc_hi[pp] * dmask_ref[2 * pp + 1]).astype(BF16), vs[pp]) for pp in pairs]
    for pp in pairs:
        r_ref[0, pp] = cdec_ref[pp] * r_prev[pp] + kv_new[pp] * bd_ref[...]
    o_all = jnp.concatenate([jnp.where(low, o_lo[pp], o_hi[pp]) + o_cross[pp] for pp in pairs], axis=0)
    y_all = _group_norm_pair(o_all, avg_ref[...])
    for pp in pairs:
        gp = p_ref[:, col(pp, 3)].astype(F32)
        y = y_all[pp * blk:(pp + 1) * blk] * gnw_ref[:, pcol(pp)] * _silu(gp)
        cat_ref[:, q_cols + pp * LANES:q_cols + (pp + 1) * LANES] = y.astype(BF16)


def _prompt_mixer(p_p, kv_p, sinks, gn_w, n_pbatch, t_prompt, ret_tabs):
    blk = WINDOW
    nb = t_prompt // blk
    p_cols = p_p.shape[1]
    kv_cols = kv_p.shape[1]
    d_mix = (N_ATTN_HEADS + N_RET_HEADS) * HEAD_DIM
    decay_mask, q_dec, k_dec, chunk_decay = ret_tabs
    n_pairs = N_RET_HEADS // 2
    bd = _block_diag_mask()
    cdec = jnp.stack([bd * chunk_decay[0, pp * LANES:(pp + 1) * LANES][:, None] for pp in range(n_pairs)])
    const2 = lambda shape: pl.BlockSpec(shape, lambda b, j: (0,) * len(shape))
    return pl.pallas_call(
        _prompt_mixer_body,
        grid=(n_pbatch, nb),
        in_specs=[
            pl.BlockSpec(memory_space=pltpu.SMEM),
            pl.BlockSpec((blk, p_cols), lambda b, j: (b * nb + j, 0)),
            pl.BlockSpec((blk, kv_cols), lambda b, j: (b * nb + j, 0)),
            pl.BlockSpec((blk, kv_cols), lambda b, j: (b * nb + jnp.maximum(j - 1, 0), 0)),
            const2((N_RET_HEADS, blk, blk)),
            const2((blk, N_RET_HEADS * HEAD_DIM)),
            const2((blk, N_RET_HEADS * HEAD_DIM)),
            const2((n_pairs, LANES, LANES)),
            const2((LANES, LANES)),
            const2((LANES, LANES)),
            const2((1, N_RET_HEADS * HEAD_DIM)),
        ],
        out_specs=[
            pl.BlockSpec((blk, d_mix), lambda b, j: (b * nb + j, 0)),
            pl.BlockSpec((1, n_pairs, LANES, LANES), lambda b, j: (b, 0, 0, 0)),
        ],
        out_shape=[
            jax.ShapeDtypeStruct((n_pbatch * t_prompt, d_mix), BF16),
            jax.ShapeDtypeStruct((n_pbatch, n_pairs, LANES, LANES), F32),
        ],
        compiler_params=_cparams(("arbitrary", "arbitrary")),
        name="prompt_mixer",
    )(sinks, p_p, kv_p, kv_p, decay_mask, q_dec, k_dec, cdec, bd, _pair_average_matrix(), gn_w.reshape(1, -1))


def _sample_mixer_body(t_new, sink_ref, p_ref, kv_ref, kbuf_ref, vbuf_ref, ret_ref, d4_ref, qdec_ref, kdec_ref,
                       cdec_ref, avg_ref, gnw_ref, cat_ref, knew_ref, vnew_ref, rnew_ref, ka2_ref, va2_ref, qs2_ref):
    w = kbuf_ref.shape[1]
    q_cols = N_ATTN_HEADS * HEAD_DIM
    r_cols = N_RET_HEADS * HEAD_DIM
    kvw = N_KV_HEADS * HEAD_DIM
    n_keys = ka2_ref.shape[1]
    rows_per_head = SUBLANES
    ka2_ref[...] = jnp.zeros_like(ka2_ref)
    va2_ref[...] = jnp.zeros_like(va2_ref)
    qs2_ref[...] = jnp.zeros_like(qs2_ref)
    low4 = _lane_is_low_half((t_new, LANES))
    lowk = _lane_is_low_half((n_keys, LANES))
    n_rows = GQA_GROUP * rows_per_head
    row = lax.broadcasted_iota(jnp.int32, (n_rows, n_keys), 0)
    key = lax.broadcasted_iota(jnp.int32, (n_rows, n_keys), 1)
    t_of_row = row % rows_per_head
    mask = (t_of_row < t_new) & (key > t_of_row) & (key <= t_of_row + WINDOW) & (key < w + t_new)
    head_of_row = lax.broadcasted_iota(jnp.int32, (n_rows, 1), 0) // rows_per_head

    def per_group(g, carry):
        units = range(SAMPLE_GROUP)
        bs = [SAMPLE_GROUP * g + u for u in units]
        items_a = [(u, kvh) for u in units for kvh in range(N_KV_HEADS)]
        items_r = [(u, pp) for u in units for pp in range(N_RET_HEADS // 2)]
        for u in units:
            b, ka_ref, va_ref, qs_ref = bs[u], ka2_ref.at[u], va2_ref.at[u], qs2_ref.at[u]
            ka_ref[0:w, :] = kbuf_ref[b]
            va_ref[0:w, :] = vbuf_ref[b]
            ka_ref[w:w + t_new, :] = kv_ref[b][:, 0:kvw]
            va_ref[w:w + t_new, :] = kv_ref[b][:, kvw:2 * kvw]
            knew_ref[b] = ka_ref[t_new:t_new + w, :]
            vnew_ref[b] = va_ref[t_new:t_new + w, :]
            for h in range(N_ATTN_HEADS):
                c0 = (h // 2) * LANES
                qp = p_ref[b][:, c0:c0 + LANES]
                qs_ref[h * rows_per_head:h * rows_per_head + t_new, :] = jnp.where(
                    low4 if h % 2 == 0 else jnp.logical_not(low4), qp, 0.0)
        kall = [ka2_ref[u] for u in units]
        vall = [va2_ref[u] for u in units]
        kswap = [pltpu.roll(kall[u], HEAD_DIM, 1) for u in units]
        vswap = [pltpu.roll(vall[u], HEAD_DIM, 1) for u in units]
        keep = [lowk, jnp.logical_not(lowk)]
        k2 = {(u, kvh): jnp.where(keep[kvh], kall[u], kswap[u]) for u, kvh in items_a}
        v2 = {(u, kvh): jnp.where(keep[kvh], vall[u], vswap[u]) for u, kvh in items_a}
        sinks = []
        for kvh in range(N_KV_HEADS):
            sink = jnp.zeros((n_rows, 1), F32)
            for hh in range(GQA_GROUP):
                sink = jnp.where(head_of_row == hh, sink_ref[kvh * GQA_GROUP + hh], sink)
            sinks.append(sink)
        s = {it: jnp.where(mask, _dot_nt(qs2_ref[it[0], it[1] * n_rows:(it[1] + 1) * n_rows, :], k2[it], HIGHEST),
                           NEG_INF) for it in items_a}
        m = {it: jnp.maximum(jnp.max(s[it], axis=-1, keepdims=True), sinks[it[1]]) for it in items_a}
        e = {it: jnp.exp(s[it] - m[it]) for it in items_a}
        den = {it: jnp.sum(e[it], axis=-1, keepdims=True) + jnp.exp(sinks[it[1]] - m[it]) for it in items_a}
        o = {it: _dot(e[it], v2[it], HIGHEST) / den[it] for it in items_a}
        for u, kvh in items_a:
            for pr in range(GQA_GROUP // 2):
                o_lo = o[u, kvh][(2 * pr) * rows_per_head:(2 * pr) * rows_per_head + t_new]
                o_hi = o[u, kvh][(2 * pr + 1) * rows_per_head:(2 * pr + 1) * rows_per_head + t_new]
                c0 = (kvh * (GQA_GROUP // 2) + pr) * LANES
                cat_ref[bs[u], :, c0:c0 + LANES] = jnp.where(low4, o_lo, o_hi)
        col = lambda pp, part: slice(q_cols + part * r_cols + pp * LANES, q_cols + part * r_cols + (pp + 1) * LANES)
        pcol = lambda pp: slice(pp * LANES, (pp + 1) * LANES)
        qp = {it: p_ref[bs[it[0]]][:, col(it[1], 0)] for it in items_r}
        kp = {it: p_ref[bs[it[0]]][:, col(it[1], 1)] for it in items_r}
        vp = {it: p_ref[bs[it[0]]][:, col(it[1], 2)] for it in items_r}
        r0 = {it: ret_ref[bs[it[0]], it[1]] for it in items_r}
        q_lo = {it: jnp.where(low4, qp[it], 0.0) for it in items_r}
        q_hi = {it: jnp.where(low4, 0.0, qp[it]) for it in items_r}
        sc_lo = {it: _dot_nt(q_lo[it], kp[it], HIGHEST) * d4_ref[2 * it[1]] for it in items_r}
        sc_hi = {it: _dot_nt(q_hi[it], kp[it], HIGHEST) * d4_ref[2 * it[1] + 1] for it in items_r}
        oc_lo = {it: _dot(q_lo[it] * qdec_ref[:, pcol(it[1])], r0[it], HIGHEST) for it in items_r}
        oc_hi = {it: _dot(q_hi[it] * qdec_ref[:, pcol(it[1])], r0[it], HIGHEST) for it in items_r}
        kv_full = {it: _dot_tn(kp[it] * kdec_ref[:, pcol(it[1])], vp[it], HIGHEST) for it in items_r}
        oi_lo = {it: _dot(sc_lo[it], vp[it], HIGHEST) for it in items_r}
        oi_hi = {it: _dot(sc_hi[it], vp[it], HIGHEST) for it in items_r}
        for it in items_r:
            kv_pair = jnp.concatenate([kv_full[it][0:HEAD_DIM, 0:HEAD_DIM], kv_full[it][HEAD_DIM:, HEAD_DIM:]], axis=0)
            rnew_ref[bs[it[0]], it[1]] = cdec_ref[it[1]] * r0[it] + kv_pair
        o_ret = [jnp.where(low4, oi_lo[it], oi_hi[it]) + jnp.concatenate([oc_lo[it], oc_hi[it]], axis=1)
                 for it in items_r]
        y_ret = _group_norm_pairs(o_ret, avg_ref[...])
        for it, y in zip(items_r, y_ret):
            gp = p_ref[bs[it[0]]][:, col(it[1], 3)]
            cat_ref[bs[it[0]], :, q_cols + it[1] * LANES:q_cols + (it[1] + 1) * LANES] = (
                y * gnw_ref[:, pcol(it[1])] * _silu(gp))
        return carry

    lax.fori_loop(0, kbuf_ref.shape[0] // SAMPLE_GROUP, per_group, 0)


def _sample_mixer(p_s, kv_s, k_buf, v_buf, ret0, sinks, gn_w, ret_tabs):
    nb, t_new, p_cols = p_s.shape
    kv_cols = kv_s.shape[2]
    w = k_buf.shape[1]
    bb = SAMPLE_BATCH_TILE
    n_pairs = N_RET_HEADS // 2
    d_mix = (N_ATTN_HEADS + N_RET_HEADS) * HEAD_DIM
    decay_mask, q_dec, k_dec, chunk_decay = ret_tabs
    n_keys = ((w + t_new + SUBLANES - 1) // SUBLANES) * SUBLANES
    cdec = jnp.stack([jnp.broadcast_to(chunk_decay[0, pp * LANES:(pp + 1) * LANES][:, None], (LANES, HEAD_DIM))
                      for pp in range(n_pairs)])
    const1 = lambda shape: pl.BlockSpec(shape, lambda i: (0,) * len(shape))
    body = functools.partial(_sample_mixer_body, t_new)
    return pl.pallas_call(
        body,
        grid=(nb // bb,),
        in_specs=[
            pl.BlockSpec(memory_space=pltpu.SMEM),
            pl.BlockSpec((bb, t_new, p_cols), lambda i: (i, 0, 0)),
            pl.BlockSpec((bb, t_new, kv_cols), lambda i: (i, 0, 0)),
            pl.BlockSpec((bb, w, LANES), lambda i: (i, 0, 0)),
            pl.BlockSpec((bb, w, LANES), lambda i: (i, 0, 0)),
            pl.BlockSpec((bb, n_pairs, LANES, HEAD_DIM), lambda i: (i, 0, 0, 0)),
            const1((N_RET_HEADS, t_new, t_new)),
            const1((t_new, N_RET_HEADS * HEAD_DIM)),
            const1((t_new, N_RET_HEADS * HEAD_DIM)),
            const1((n_pairs, LANES, HEAD_DIM)),
            const1((LANES, LANES)),
            const1((1, N_RET_HEADS * HEAD_DIM)),
        ],
        out_specs=[
            pl.BlockSpec((bb, t_new, d_mix), lambda i: (i, 0, 0)),
            pl.BlockSpec((bb, w, LANES), lambda i: (i, 0, 0)),
            pl.BlockSpec((bb, w, LANES), lambda i: (i, 0, 0)),
            pl.BlockSpec((bb, n_pairs, LANES, HEAD_DIM), lambda i: (i, 0, 0, 0)),
        ],
        out_shape=[
            jax.ShapeDtypeStruct((nb, t_new, d_mix), F32),
            jax.ShapeDtypeStruct((nb, w, LANES), F32),
            jax.ShapeDtypeStruct((nb, w, LANES), F32),
            jax.ShapeDtypeStruct((nb, n_pairs, LANES, HEAD_DIM), F32),
        ],
        scratch_shapes=[
            pltpu.VMEM((SAMPLE_GROUP, n_keys, LANES), F32),
            pltpu.VMEM((SAMPLE_GROUP, n_keys, LANES), F32),
            pltpu.VMEM((SAMPLE_GROUP, N_ATTN_HEADS * SUBLANES, LANES), F32),
        ],
        compiler_params=_cparams(("arbitrary",)),
        name="sample_mixer",
    )(sinks, p_s, kv_s, k_buf, v_buf, ret0, decay_mask, q_dec, k_dec, cdec, _pair_average_matrix(), gn_w.reshape(1, -1))


def _router_body(geo, xp_ref, xs_ref, catp_ref, cats_ref, wout_ref, g1p_ref, g1s_ref, nw_ref, shp_ref, scp_ref,
                 shs_ref, scs_ref, wr_ref, br_ref, tri_ref, x1_ref, h2_ref, idx_ref, gate_ref, rank_ref, cnt_ref):
    i = pl.program_id(0)
    tv = functools.partial(_tile_vec, i, geo)
    cat = _tile_rows(i, geo, catp_ref, cats_ref)
    x1 = _tile_rows(i, geo, xp_ref, xs_ref) + tv(g1p_ref, g1s_ref) * _dot(cat, wout_ref[...])
    x1_ref[...] = x1
    h2 = _rms_norm(x1, nw_ref[...]) * (1.0 + tv(scp_ref, scs_ref)) + tv(shp_ref, shs_ref)
    h2_hi = h2.astype(BF16)
    h2_ref[...] = h2_hi.reshape(h2_ref.shape)
    h2_lo = (h2 - h2_hi.astype(F32)).astype(BF16)
    w_parts = wr_ref[...]
    by_hi = _dot_nt(w_parts, h2_hi)
    logits = by_hi[:N_EXPERTS] + by_hi[N_EXPERTS:] + _dot_nt(w_parts[:N_EXPERTS], h2_lo) + br_ref[...]
    e_iota = lax.broadcasted_iota(jnp.int32, logits.shape, 0)
    vals, sels = [], []
    for _ in range(TOP_K):
        m = jnp.max(logits, axis=0, keepdims=True)
        sel = jnp.min(jnp.where(logits == m, e_iota, N_EXPERTS), axis=0, keepdims=True)
        vals.append(m)
        sels.append(sel)
        logits = jnp.where(e_iota == sel, -jnp.inf, logits)
    es = [jnp.exp(v - vals[0]) for v in vals]
    den = es[0]
    for e in es[1:]:
        den = den + e

    @pl.when(i == 0)
    def _():
        cnt_ref[...] = jnp.zeros_like(cnt_ref)

    hits = [e_iota == sels[k] for k in range(TOP_K)]
    onehots = [jnp.where(h, 1.0, 0.0) for h in hits]
    befores = [_dot(o.astype(BF16), tri_ref[...]) for o in onehots]
    totals = [jnp.sum(o, axis=1, keepdims=True) for o in onehots]
    base = cnt_ref[...]
    for k in range(TOP_K):
        rank = jnp.sum(jnp.where(hits[k], base + befores[k], 0.0), axis=0, keepdims=True)
        base = base + totals[k]
        idx_ref[k:k + 1, :] = sels[k]
        gate_ref[k:k + 1, :] = es[k] / den
        rank_ref[k:k + 1, :] = rank.astype(jnp.int32)
    cnt_ref[...] = base


def _outproj_router(x_p, x_s, cat_p, cat_s, w_out_bf16, norm_w, mod_p, mod_s, layer, w_router_t, b_router, geo):
    n_p, d = x_p.shape
    n = n_p + x_s.shape[0]
    d_mix = cat_p.shape[1]
    pv = lambda chunk: _prompt_vec_spec(layer, chunk, d)
    sv = lambda chunk: _sample_vec_spec(layer, chunk, geo, d)
    tile = lambda cols: pl.BlockSpec((TOKEN_TILE, cols), lambda i: (i, 0))
    choice = pl.BlockSpec((TOP_K, TOKEN_TILE), lambda i: (0, i))
    tri = jnp.asarray(np.triu(np.ones((TOKEN_TILE, TOKEN_TILE), np.float32), 1)).astype(BF16)
    w_hi = w_router_t.astype(BF16)
    w_router_parts = jnp.concatenate([w_hi, (w_router_t - w_hi.astype(F32)).astype(BF16)], axis=0)
    return pl.pallas_call(
        functools.partial(_router_body, geo),
        grid=(n // TOKEN_TILE,),
        in_specs=[
            _prompt_rows_spec(geo, d), _sample_rows_spec(geo, d),
            _prompt_rows_spec(geo, d_mix), _sample_rows_spec(geo, d_mix),
            pl.BlockSpec((d_mix, d), lambda i: (0, 0)),
            pv(2), sv(2),
            pl.BlockSpec((1, d), lambda i: (0, 0)),
            pv(3), pv(4), sv(3), sv(4),
            pl.BlockSpec((2 * N_EXPERTS, d), lambda i: (0, 0)),
            pl.BlockSpec((N_EXPERTS, 1), lambda i: (0, 0)),
            pl.BlockSpec((TOKEN_TILE, TOKEN_TILE), lambda i: (0, 0)),
        ],
        out_specs=[
            tile(d),
            pl.BlockSpec((TOKEN_TILE, d // LANES, LANES), lambda i: (i, 0, 0)),
            choice, choice, choice,
            pl.BlockSpec((N_EXPERTS, 1), lambda i: (0, 0)),
        ],
        out_shape=[
            jax.ShapeDtypeStruct((n, d), F32),
            jax.ShapeDtypeStruct((n, d // LANES, LANES), BF16),
            jax.ShapeDtypeStruct((TOP_K, n), jnp.int32),
            jax.ShapeDtypeStruct((TOP_K, n), F32),
            jax.ShapeDtypeStruct((TOP_K, n), jnp.int32),
            jax.ShapeDtypeStruct((N_EXPERTS, 1), F32),
        ],
        compiler_params=_cparams(("arbitrary",)),
        name="outproj_router",
    )(x_p, x_s, cat_p, cat_s, w_out_bf16, mod_p, mod_s, norm_w.reshape(1, d), mod_p, mod_p, mod_s, mod_s,
      w_router_parts, b_router.reshape(N_EXPERTS, 1), tri)


def _routing_plan(top_idx, rank, counts_f, tm):
    k, n = top_idx.shape
    n_pairs = k * n
    n_tiles = (n_pairs + N_EXPERTS * (tm - 1)) // tm + 1
    counts = counts_f[:, 0].astype(jnp.int32)
    tiles_per = (counts + tm - 1) // tm
    tile_end = jnp.cumsum(tiles_per)
    tile_start = tile_end - tiles_per
    n_used = tile_end[-1]
    n_slots = n_tiles * tm

    def lookup(table, index):
        ids = jnp.arange(table.shape[0], dtype=jnp.int32).reshape((-1,) + (1,) * index.ndim)
        return jnp.sum(jnp.where(index[None] == ids, table.reshape(ids.shape), 0), axis=0)

    slot_of_pair = lookup(tile_start * tm, top_idx) + rank
    pad_counts = jnp.concatenate([tiles_per * tm - counts, (n_slots - n_used * tm).reshape(1)])
    pad_end = jnp.cumsum(pad_counts)
    first_pad_slot = jnp.concatenate([tile_start * tm + counts, (n_used * tm).reshape(1)])
    k_pad = jnp.arange(n_slots - n_pairs, dtype=jnp.int32)
    seg = jnp.sum(k_pad[:, None] >= pad_end[None, :], axis=1).astype(jnp.int32)
    pad_slot = k_pad + lookup(first_pad_slot - (pad_end - pad_counts), seg)
    token_of_pair = jnp.broadcast_to(jnp.arange(n, dtype=jnp.int32)[None, :], (k, n))
    keys = jnp.concatenate([slot_of_pair.reshape(-1), pad_slot])
    vals = jnp.concatenate([token_of_pair.reshape(-1), jnp.zeros_like(pad_slot)])
    _, slot_rows = lax.sort((keys, vals), num_keys=1)
    tile_ids = jnp.arange(n_tiles, dtype=jnp.int32)
    tile_expert = jnp.minimum(jnp.sum(tile_ids[:, None] >= tile_end[None, :], axis=1), N_EXPERTS - 1).astype(jnp.int32)
    last_expert = tile_expert[jnp.maximum(n_used - 1, 0)]
    tile_expert = jnp.where(tile_ids < n_used, tile_expert, last_expert)
    return tile_expert, n_used.reshape(1).astype(jnp.int32), slot_rows.reshape(n_tiles, 1, tm), slot_of_pair


def _issue_row_copies(src_hbm, row_of, dst_row, sem, n_rows):
    def group(g, c):
        for u in range(DMA_UNROLL):
            r = g * DMA_UNROLL + u
            pltpu.make_async_copy(src_hbm.at[row_of(r)], dst_row(r), sem).start(priority=u % 2)
        return c
    lax.fori_loop(0, n_rows // DMA_UNROLL, group, 0)


def _issue_row_copies_static(src_hbm, row_of, dst_row, sem, lo, hi):
    for r in range(lo, hi):
        pltpu.make_async_copy(src_hbm.at[row_of(r)], dst_row(r), sem).start(priority=r % 2)


def _wait_row_copies(src_hbm, dst_buf, sem):
    pltpu.make_async_copy(src_hbm.at[pl.ds(0, dst_buf.shape[0])], dst_buf, sem).wait()


def _expert_body(te_ref, nt_ref, rows0_ref, rows1_ref, rows2_ref, h_hbm, wgu_ref, bg_ref, bu_ref, wd_ref, bd_ref,
                 perm_ref, y_ref, xbuf, sems, wgu_s, wd_s, act_s):
    j = pl.program_id(0)
    n_tiles = pl.num_programs(0)
    n_used = nt_ref[0]
    slot = j % ROW_BUFFERS
    far = (j + ROW_BUFFERS - 1) % ROW_BUFFERS
    tm = EXPERT_TILE
    f2 = wgu_ref.shape[2]
    n_chunks = f2 // MXU_WIDTH
    half = MXU_WIDTH // 2
    next_row = lambda r: rows2_ref[0, 0, r]
    next_dst = lambda r: xbuf.at[far, r]

    @pl.when(j == 0)
    def _():
        _issue_row_copies(h_hbm, lambda r: rows0_ref[0, 0, r], lambda r: xbuf.at[0, r], sems.at[0], tm)
        _issue_row_copies(h_hbm, lambda r: rows1_ref[0, 0, r], lambda r: xbuf.at[1, r], sems.at[1], tm)

    new_expert = jnp.logical_or(j == 0, te_ref[j] != te_ref[jnp.maximum(j - 1, 0)])

    @pl.when(jnp.logical_and(j < n_used, new_expert))
    def _():
        for c in range(n_chunks):
            cols = slice(c * MXU_WIDTH, (c + 1) * MXU_WIDTH)
            wgu_s[:, cols] = _dot(wgu_ref[0, :, cols].astype(BF16), perm_ref[...]).astype(BF16)
        wd_s[...] = wd_ref[0].astype(BF16)

    @pl.when(j < n_used)
    def _():
        _wait_row_copies(h_hbm, xbuf.at[slot], sems.at[slot])
        x = xbuf[slot].reshape(tm, wgu_ref.shape[1])
        issue_chunks = n_chunks // 2
        per_chunk = tm // issue_chunks
        for c in range(n_chunks):
            if c < issue_chunks:
                _issue_row_copies_static(h_hbm, next_row, next_dst, sems.at[far], c * per_chunk, (c + 1) * per_chunk)
            gu = _dot(x, wgu_s[:, c * MXU_WIDTH:(c + 1) * MXU_WIDTH])
            glu = jnp.minimum(gu[:, :half] + bg_ref[0, :, c * half:(c + 1) * half], SWIGLU_LIMIT)
            lin = jnp.clip(gu[:, half:] + bu_ref[0, :, c * half:(c + 1) * half], -SWIGLU_LIMIT, SWIGLU_LIMIT)
            act_s[:, c * half:(c + 1) * half] = (glu * jax.nn.sigmoid(SWIGLU_ALPHA * glu) * (lin + 1.0)).astype(BF16)
        y = _dot(act_s[...], wd_s[...]) + bd_ref[0]
        y_ref[...] = y.astype(BF16).reshape(y_ref.shape)

    @pl.when(j >= n_used)
    def _():
        _wait_row_copies(h_hbm, xbuf.at[slot], sems.at[slot])
        _issue_row_copies(h_hbm, next_row, next_dst, sems.at[far], tm)
        y_ref[...] = jnp.zeros_like(y_ref)

    @pl.when(j == n_tiles - 1)
    def _():
        for ahead in range(1, ROW_BUFFERS):
            late = (j + ahead) % ROW_BUFFERS
            _wait_row_copies(h_hbm, xbuf.at[late], sems.at[late])


def _deinterleave_matrix():
    m = np.zeros((MXU_WIDTH, MXU_WIDTH), np.float32)
    j = np.arange(MXU_WIDTH // 2)
    m[2 * j, j] = 1.0
    m[2 * j + 1, MXU_WIDTH // 2 + j] = 1.0
    return jnp.asarray(m).astype(BF16)


def _routed_experts(h2_rows, tile_expert, n_used, slot_rows, w_gate_up, bg, bu, w_down, bd, layer):
    d = w_gate_up.shape[2]
    f2 = w_gate_up.shape[3]
    f = f2 // 2
    tm = EXPERT_TILE
    n_tiles = slot_rows.shape[0]
    row_tile = h2_rows.shape[1:]
    assert ROW_BUFFERS == 3 and n_tiles >= ROW_BUFFERS
    by_expert = lambda shape: pl.BlockSpec(shape, lambda j, te, nt: (te[j], 0, 0))
    layer_expert = lambda shape: pl.BlockSpec((None,) + shape, lambda j, te, nt: (layer, te[j], 0, 0))
    rows_ahead = lambda k: pl.BlockSpec((1, 1, tm), lambda j, te, nt: (jnp.minimum(j + k, n_tiles - 1), 0, 0),
                                        memory_space=pltpu.SMEM)
    grid_spec = pltpu.PrefetchScalarGridSpec(
        num_scalar_prefetch=2,
        grid=(n_tiles,),
        in_specs=[
            rows_ahead(0), rows_ahead(1), rows_ahead(2),
            pl.BlockSpec(memory_space=pl.ANY),
            layer_expert((1, d, f2)),
            by_expert((1, 1, f)), by_expert((1, 1, f)),
            layer_expert((1, f, d)),
            by_expert((1, 1, d)),
            pl.BlockSpec((MXU_WIDTH, MXU_WIDTH), lambda j, te, nt: (0, 0)),
        ],
        out_specs=pl.BlockSpec((tm,) + row_tile, lambda j, te, nt: (j, 0, 0)),
        scratch_shapes=[
            pltpu.VMEM((ROW_BUFFERS, tm) + row_tile, BF16),
            pltpu.SemaphoreType.DMA((ROW_BUFFERS,)),
            pltpu.VMEM((d, f2), BF16),
            pltpu.VMEM((f, d), BF16),
            pltpu.VMEM((tm, f), BF16),
        ],
    )
    return pl.pallas_call(
        _expert_body,
        grid_spec=grid_spec,
        out_shape=jax.ShapeDtypeStruct((n_tiles * tm,) + row_tile, BF16),
        compiler_params=_cparams(("arbitrary",)),
        name="routed_experts",
    )(tile_expert, n_used, slot_rows, slot_rows, slot_rows, h2_rows, w_gate_up, bg, bu, w_down, bd,
      _deinterleave_matrix())


def _combine_body(geo, final_norm, slots_cur_ref, slots_nxt_ref, x1_ref, gate_ref, g2p_ref, g2s_ref, nfw_ref, y_hbm,
                  outp_ref, outs_ref, ybuf_even, ybuf_odd, sems):
    i = pl.program_id(0)
    n_steps = pl.num_programs(0)
    tm = TOKEN_TILE

    @pl.when(i == 0)
    def _():
        for k in range(TOP_K):
            _issue_row_copies(y_hbm, lambda r, k=k: slots_cur_ref[0, k, r], lambda r, k=k: ybuf_even.at[k, r],
                              sems.at[0], tm)

    def step(cur, nxt, sem_cur, sem_nxt):
        for k in range(TOP_K):
            _wait_row_copies(y_hbm, cur.at[k], sem_cur)
        for k in range(TOP_K):
            _issue_row_copies_static(y_hbm, lambda r, k=k: slots_nxt_ref[0, k, r], lambda r, k=k: nxt.at[k, r],
                                     sem_nxt, 0, tm)
        rows = lambda k: cur[k].reshape(tm, x1_ref.shape[1]).astype(F32)
        acc = rows(0) * gate_ref[:, 0:1]
        for k in range(1, TOP_K):
            acc = acc + rows(k) * gate_ref[:, k:k + 1]
        x2 = x1_ref[...] + _tile_vec(i, geo, g2p_ref, g2s_ref) * acc
        if final_norm:
            x2 = _rms_norm(x2, nfw_ref[...])

        @pl.when(i < geo.n_ptiles)
        def _():
            outp_ref[...] = x2

        @pl.when(i >= geo.n_ptiles)
        def _():
            outs_ref[...] = x2

        @pl.when(i == n_steps - 1)
        def _():
            for k in range(TOP_K):
                _wait_row_copies(y_hbm, nxt.at[k], sem_nxt)

    @pl.when(i % 2 == 0)
    def _():
        step(ybuf_even, ybuf_odd, sems.at[0], sems.at[1])

    @pl.when(i % 2 == 1)
    def _():
        step(ybuf_odd, ybuf_even, sems.at[1], sems.at[0])


def _combine(x1, y_rows, slot_of_pair, gates_t, mod_p, mod_s, layer, final_w, geo, final_norm):
    n, d = x1.shape
    tm = TOKEN_TILE
    n_steps = n // tm
    n_p = geo.n_ptiles * tm
    slots = slot_of_pair.reshape(TOP_K, n_steps, tm).transpose(1, 0, 2)
    return pl.pallas_call(
        functools.partial(_combine_body, geo, final_norm),
        grid=(n_steps,),
        in_specs=[
            pl.BlockSpec((1, TOP_K, tm), lambda i: (i, 0, 0), memory_space=pltpu.SMEM),
            pl.BlockSpec((1, TOP_K, tm), lambda i: (jnp.minimum(i + 1, n_steps - 1), 0, 0), memory_space=pltpu.SMEM),
            pl.BlockSpec((tm, d), lambda i: (i, 0)),
            pl.BlockSpec((tm, TOP_K), lambda i: (i, 0)),
            _prompt_vec_spec(layer, 5, d),
            _sample_vec_spec(layer, 5, geo, d),
            pl.BlockSpec((1, d), lambda i: (0, 0)),
            pl.BlockSpec(memory_space=pl.ANY),
        ],
        out_specs=[_prompt_rows_spec(geo, d), _sample_rows_spec(geo, d)],
        out_shape=[jax.ShapeDtypeStruct((n_p, d), F32), jax.ShapeDtypeStruct((n - n_p, d), F32)],
        scratch_shapes=[pltpu.VMEM((TOP_K, tm) + y_rows.shape[1:], y_rows.dtype),
                        pltpu.VMEM((TOP_K, tm) + y_rows.shape[1:], y_rows.dtype),
                        pltpu.SemaphoreType.DMA((2,))],
        compiler_params=_cparams(("arbitrary",)),
        name="moe_combine",
    )(slots, slots, x1, gates_t, mod_p, mod_s, final_w.reshape(1, d), y_rows)


def kernel(x_prompt, x_sample, state_swa_k, state_swa_v, state_ret, c_prompt, c_sample, norm_mix_w, w_ada, b_ada, w_in, attn_sinks, ret_gn_w, w_out, norm_ffn_w, w_router, b_router, w_gate_up, b_gate_up, w_down, b_down, norm_final_w):
    n_pbatch, t_prompt, d = x_prompt.shape
    n_sbatch, t_sample, _ = x_sample.shape
    depth = w_in.shape[0]
    n_p = n_pbatch * t_prompt
    n_s = n_sbatch * t_sample
    win = state_swa_k.shape[2]
    assert d == (N_ATTN_HEADS + N_RET_HEADS) * HEAD_DIM and w_gate_up.shape[1] == N_EXPERTS
    assert t_prompt % TOKEN_TILE == 0 and n_s % TOKEN_TILE == 0 and TOKEN_TILE % t_sample == 0
    assert t_prompt % WINDOW == 0 and RET_CHUNK == WINDOW and win == WINDOW and t_sample <= SUBLANES
    assert n_sbatch % SAMPLE_BATCH_TILE == 0 and n_pbatch <= SUBLANES
    assert w_gate_up.shape[3] % MXU_WIDTH == 0 and EXPERT_TILE % DMA_UNROLL == 0 and TOKEN_TILE % DMA_UNROLL == 0
    geo = _Geometry(n_p // TOKEN_TILE, t_prompt // TOKEN_TILE, n_pbatch)

    c_all = jnp.concatenate([c_prompt, jnp.zeros((SUBLANES - n_pbatch, d), F32), c_sample], axis=0)
    mod = _ada_modulation(c_all, w_ada, b_ada)
    mod_p = mod[:, :SUBLANES]
    mod_s = jnp.repeat(mod[:, SUBLANES:], t_sample, axis=1)

    rope_tab = _rope_tables(t_prompt, t_sample)
    ret_tabs_p = _retention_tables(RET_CHUNK)
    ret_tabs_s = _retention_tables(t_sample)
    f = w_gate_up.shape[3] // 2

    x_p = x_prompt.reshape(n_p, d)
    x_s = x_sample.reshape(n_s, d)
    kp_l, vp_l, rp_l, ks_l, vs_l, rs_l = [], [], [], [], [], []
    for l in range(depth):
        p_all, kv_all = _input_projection(x_p, x_s, norm_mix_w[l], mod_p, mod_s, l, w_in[l].astype(BF16), rope_tab, geo)
        cat_p, r_pairs = _prompt_mixer(p_all, kv_all, attn_sinks[l], ret_gn_w[l], n_pbatch, t_prompt, ret_tabs_p)
        p_s = p_all[n_p:].astype(F32)
        kv_s = kv_all[n_p:]
        kv_p = kv_all[:n_p]
        cat_s, k_new, v_new, r_new = _sample_mixer(
            p_s.reshape(n_sbatch, t_sample, -1), kv_s.reshape(n_sbatch, t_sample, -1),
            state_swa_k[l].reshape(n_sbatch, win, LANES), state_swa_v[l].reshape(n_sbatch, win, LANES),
            state_ret[l].reshape(n_sbatch, N_RET_HEADS // 2, LANES, HEAD_DIM), attn_sinks[l], ret_gn_w[l], ret_tabs_s)
        x1, h2_rows, top_idx, gates, rank, counts = _outproj_router(
            x_p, x_s, cat_p, cat_s.reshape(n_s, d), w_out[l].astype(BF16), norm_ffn_w[l], mod_p, mod_s, l,
            w_router[l].T, b_router[l], geo)
        tile_expert, n_used, slot_rows, slot_of_pair = _routing_plan(top_idx, rank, counts, EXPERT_TILE)
        b_gu = b_gate_up[l].reshape(N_EXPERTS, 1, f, 2)
        y_rows = _routed_experts(h2_rows, tile_expert, n_used, slot_rows, w_gate_up,
                                 b_gu[..., 0], b_gu[..., 1], w_down, b_down[l][:, None, :], l)
        x_p, x_s = _combine(x1, y_rows, slot_of_pair, gates.T, mod_p, mod_s, l, norm_final_w, geo,
                            final_norm=(l == depth - 1))

        kv_last = kv_p.reshape(n_pbatch, t_prompt, -1)[:, t_prompt - WINDOW:]
        kv_last = kv_last.reshape(n_pbatch, WINDOW, 2, N_KV_HEADS, HEAD_DIM)
        kp_l.append(kv_last[:, :, 0])
        vp_l.append(kv_last[:, :, 1])
        rp_l.append(jnp.stack([r_pairs[:, :, :HEAD_DIM, :HEAD_DIM], r_pairs[:, :, HEAD_DIM:, HEAD_DIM:]], axis=2)
                    .reshape(n_pbatch, N_RET_HEADS, HEAD_DIM, HEAD_DIM))
        ks_l.append(k_new.reshape(n_sbatch, win, N_KV_HEADS, HEAD_DIM))
        vs_l.append(v_new.reshape(n_sbatch, win, N_KV_HEADS, HEAD_DIM))
        rs_l.append(r_new.reshape(n_sbatch, N_RET_HEADS, HEAD_DIM, HEAD_DIM))
    return (x_p.reshape(n_pbatch, t_prompt, d), x_s.reshape(n_sbatch, t_sample, d), jnp.stack(kp_l), jnp.stack(vp_l),
            jnp.stack(rp_l), jnp.stack(ks_l), jnp.stack(vs_l), jnp.stack(rs_l))
```

```python
import functools

import numpy as np
import jax
import jax.numpy as jnp
from jax import lax
from jax.experimental import pallas as pl
from jax.experimental.pallas import tpu as pltpu

F32 = jnp.float32
BF16 = jnp.bfloat16
HIGHEST = lax.Precision.HIGHEST

HEAD_DIM = 64
N_ATTN_HEADS = 8
N_KV_HEADS = 2
GQA_GROUP = N_ATTN_HEADS // N_KV_HEADS
WINDOW = 128
ROPE_THETA = 500000.0
ROPE_DIMS = HEAD_DIM // 4
N_RET_HEADS = 8
RET_CHUNK = 128
RET_THETA = 10000.0
N_EXPERTS = 32
TOP_K = 4
SWIGLU_LIMIT = 7.0
SWIGLU_ALPHA = 1.702
NORM_EPS = 1e-5
PAST_LEN = 16384

LANES = 128
SUBLANES = 8
MXU_WIDTH = 256
VMEM_LIMIT_BYTES = 56 * 1024 * 1024

TOKEN_TILE = 256
EXPERT_TILE = 256
SAMPLE_BATCH_TILE = 8
SAMPLE_GROUP = 4
DMA_UNROLL = 8

NEG_INF = -1e30


def _cparams(semantics):
    return pltpu.CompilerParams(dimension_semantics=semantics, vmem_limit_bytes=VMEM_LIMIT_BYTES)


def _lane_is_low_half(shape):
    return lax.broadcasted_iota(jnp.int32, shape, len(shape) - 1) < HEAD_DIM


def _dot(a, b, precision=None):
    return jnp.dot(a, b, preferred_element_type=F32, precision=precision)


def _dot_nt(a, b, precision=None):
    return lax.dot_general(a, b, (((1,), (1,)), ((), ())), preferred_element_type=F32, precision=precision)


def _dot_tn(a, b, precision=None):
    return lax.dot_general(a, b, (((0,), (0,)), ((), ())), preferred_element_type=F32, precision=precision)


def _rms_norm(x, w):
    return x * lax.rsqrt(jnp.mean(x * x, axis=-1, keepdims=True) + NORM_EPS) * w


def _silu(x):
    return x * jax.nn.sigmoid(x)


def _ada_body(c_ref, w_ref, b_ref, o_ref):
    a = _silu(c_ref[...])
    o_ref[0] = _dot(a, w_ref[0], HIGHEST) + b_ref[0]


def _ada_modulation(c_all, w_ada, b_ada):
    depth, d, cols = w_ada.shape
    rows = c_all.shape[0]
    tn = 1024
    return pl.pallas_call(
        _ada_body,
        grid=(depth, cols // tn),
        in_specs=[
            pl.BlockSpec((rows, d), lambda l, j: (0, 0)),
            pl.BlockSpec((1, d, tn), lambda l, j: (l, 0, j)),
            pl.BlockSpec((1, 1, tn), lambda l, j: (l, 0, j)),
        ],
        out_specs=pl.BlockSpec((1, rows, tn), lambda l, j: (l, 0, j)),
        out_shape=jax.ShapeDtypeStruct((depth, rows, cols), F32),
        compiler_params=_cparams(("arbitrary", "arbitrary")),
        name="ada_modulation",
    )(c_all, w_ada, b_ada.reshape(depth, 1, cols))


class _Geometry:
    def __init__(self, n_ptiles, tiles_per_batch, n_pbatch):
        self.n_ptiles = n_ptiles
        self.tiles_per_batch = tiles_per_batch
        self.n_pbatch = n_pbatch


def _tile_vec(i, geo, prompt_ref, sample_ref):
    b = jnp.minimum(i // geo.tiles_per_batch, geo.n_pbatch - 1)
    return jnp.where(i >= geo.n_ptiles, sample_ref[...], prompt_ref[pl.ds(b, 1), :])


def _tile_rows(i, geo, prompt_ref, sample_ref):
    return jnp.where(i >= geo.n_ptiles, sample_ref[...].astype(prompt_ref.dtype), prompt_ref[...])


def _prompt_vec_spec(layer, chunk, d):
    return pl.BlockSpec((None, SUBLANES, d), lambda i: (layer, 0, chunk))


def _sample_vec_spec(layer, chunk, geo, d):
    return pl.BlockSpec((None, TOKEN_TILE, d), lambda i: (layer, jnp.maximum(i - geo.n_ptiles, 0), chunk))


def _prompt_rows_spec(geo, cols):
    return pl.BlockSpec((TOKEN_TILE, cols), lambda i: (jnp.minimum(i, geo.n_ptiles - 1), 0))


def _sample_rows_spec(geo, cols):
    return pl.BlockSpec((TOKEN_TILE, cols), lambda i: (jnp.maximum(i - geo.n_ptiles, 0), 0))


def _rotate(xc, tab_ref, base, shift):
    return (xc * tab_ref[base] + pltpu.roll(xc, shift, 1) * tab_ref[base + 1]
            + pltpu.roll(xc, LANES - shift, 1) * tab_ref[base + 2])


def _inproj_body(geo, xp_ref, xs_ref, nw_ref, shp_ref, scp_ref, shs_ref, scs_ref, w_ref, tab_ref, p_ref, kv_ref):
    i = pl.program_id(0)
    x = _tile_rows(i, geo, xp_ref, xs_ref)
    h = _rms_norm(x, nw_ref[...]) * (1.0 + _tile_vec(i, geo, scp_ref, scs_ref)) + _tile_vec(i, geo, shp_ref, shs_ref)
    proj = _dot(h.astype(BF16), w_ref[...])
    q_cols = N_ATTN_HEADS * HEAD_DIM
    kv_cols = N_KV_HEADS * HEAD_DIM
    r_cols = N_RET_HEADS * HEAD_DIM
    o_ka = q_cols
    o_va = o_ka + kv_cols
    o_qr = o_va + kv_cols
    o_kr = o_qr + r_cols
    o_vr = o_kr + r_cols
    o_g = o_vr + r_cols
    attn_scale = HEAD_DIM ** -0.5
    ret_scale = HEAD_DIM ** -0.5
    half_a = ROPE_DIMS // 2
    half_r = HEAD_DIM // 2

    def emit(p_ref, kv_ref):
        dt = p_ref.dtype
        for c in range(q_cols // LANES):
            xc = proj[:, c * LANES:(c + 1) * LANES]
            p_ref[:, c * LANES:(c + 1) * LANES] = (_rotate(xc, tab_ref, 0, half_a) * attn_scale).astype(dt)
        kv_ref[:, 0:kv_cols] = _rotate(proj[:, o_ka:o_ka + kv_cols], tab_ref, 0, half_a)
        kv_ref[:, kv_cols:2 * kv_cols] = proj[:, o_va:o_va + kv_cols]
        for c in range(r_cols // LANES):
            xq = proj[:, o_qr + c * LANES:o_qr + (c + 1) * LANES]
            xk = proj[:, o_kr + c * LANES:o_kr + (c + 1) * LANES]
            p_ref[:, q_cols + c * LANES:q_cols + (c + 1) * LANES] = _rotate(xq, tab_ref, 3, half_r).astype(dt)
            p_ref[:, q_cols + r_cols + c * LANES:q_cols + r_cols + (c + 1) * LANES] = (
                _rotate(xk, tab_ref, 3, half_r) * ret_scale).astype(dt)
        p_ref[:, q_cols + 2 * r_cols:q_cols + 3 * r_cols] = proj[:, o_vr:o_vr + r_cols].astype(dt)
        p_ref[:, q_cols + 3 * r_cols:q_cols + 4 * r_cols] = proj[:, o_g:o_g + r_cols].astype(dt)

    emit(p_ref, kv_ref)


def _input_projection(x_p, x_s, norm_w, mod_p, mod_s, layer, w_in_bf16, rope_tab, geo):
    n_p, d = x_p.shape
    n_s = x_s.shape[0]
    in_cols = w_in_bf16.shape[1]
    kv_cols = 2 * N_KV_HEADS * HEAD_DIM
    p_cols = in_cols - kv_cols
    tpb = geo.tiles_per_batch
    tab_index = lambda i: (0, jnp.where(i >= geo.n_ptiles, tpb, i % tpb), 0)
    return pl.pallas_call(
        functools.partial(_inproj_body, geo),
        grid=((n_p + n_s) // TOKEN_TILE,),
        in_specs=[
            _prompt_rows_spec(geo, d), _sample_rows_spec(geo, d),
            pl.BlockSpec((1, d), lambda i: (0, 0)),
            _prompt_vec_spec(layer, 0, d), _prompt_vec_spec(layer, 1, d),
            _sample_vec_spec(layer, 0, geo, d), _sample_vec_spec(layer, 1, geo, d),
            pl.BlockSpec((d, in_cols), lambda i: (0, 0)),
            pl.BlockSpec((6, TOKEN_TILE, LANES), tab_index),
        ],
        out_specs=[
            pl.BlockSpec((TOKEN_TILE, p_cols), lambda i: (i, 0)),
            pl.BlockSpec((TOKEN_TILE, kv_cols), lambda i: (i, 0)),
        ],
        out_shape=[
            jax.ShapeDtypeStruct((n_p + n_s, p_cols), BF16),
            jax.ShapeDtypeStruct((n_p + n_s, kv_cols), F32),
        ],
        compiler_params=_cparams(("arbitrary",)),
        name="input_projection",
    )(x_p, x_s, norm_w.reshape(1, d), mod_p, mod_p, mod_s, mod_s, w_in_bf16, rope_tab)


def _rope_tables(t_prompt, t_sample):
    pos = jnp.concatenate([jnp.arange(t_prompt, dtype=jnp.int32),
                           PAST_LEN + (jnp.arange(TOKEN_TILE, dtype=jnp.int32) % t_sample)])
    d = np.arange(LANES) % HEAD_DIM

    def tables(n_dims, theta):
        half = n_dims // 2
        freqs = jnp.power(jnp.float32(theta), -jnp.arange(half, dtype=jnp.float32) / half)
        ang = pos.astype(jnp.float32)[:, None] * freqs[None, :]
        cos, sin = jnp.cos(ang), jnp.sin(ang)
        fidx = np.where(d < n_dims, d % half, 0)
        cos_l = jnp.where(jnp.asarray(d < n_dims)[None, :], cos[:, fidx], 1.0)
        sin_l = sin[:, fidx]
        upper = jnp.asarray((d >= half) & (d < n_dims))[None, :]
        lower = jnp.asarray(d < half)[None, :]
        return [cos_l, jnp.where(upper, sin_l, 0.0), jnp.where(lower, -sin_l, 0.0)]

    return jnp.stack(tables(ROPE_DIMS, ROPE_THETA) + tables(HEAD_DIM, RET_THETA)).astype(F32)


def _retention_tables(c):
    h = N_RET_HEADS
    log_gamma = jnp.log(1.0 - jnp.power(2.0, -5.0 - jnp.arange(h, dtype=jnp.float32)))
    idx = jnp.arange(c, dtype=jnp.float32)
    diff = idx[:, None] - idx[None, :]
    decay_mask = jnp.where(diff >= 0, jnp.exp(log_gamma[:, None, None] * jnp.maximum(diff, 0.0)), 0.0)
    k_dec = jnp.exp(log_gamma[None, :] * (c - 1 - idx)[:, None])
    q_dec = jnp.exp(log_gamma[None, :] * (idx + 1.0)[:, None])
    chunk_decay = jnp.exp(log_gamma * c)
    rep = lambda a: jnp.repeat(a, HEAD_DIM, axis=-1)
    return decay_mask.astype(F32), rep(q_dec).astype(F32), rep(k_dec).astype(F32), rep(chunk_decay[None, :]).astype(F32)


def _group_norm_pairs(items, avg):
    avg_b = avg.astype(BF16)

    def block_means(xs):
        his = [x.astype(BF16) for x in xs]
        los = [(x - hi.astype(F32)).astype(BF16) for x, hi in zip(xs, his)]
        return [_dot(hi, avg_b) + _dot(lo, avg_b) for hi, lo in zip(his, los)]

    dlts = [o - mu for o, mu in zip(items, block_means(items))]
    varis = block_means([d * d for d in dlts])
    return [d * lax.rsqrt(v + NORM_EPS) for d, v in zip(dlts, varis)]


def _group_norm_pair(o, avg):
    return _group_norm_pairs([o], avg)[0]


def _pair_average_matrix():
    r = np.arange(LANES)
    return jnp.asarray(((r[:, None] // HEAD_DIM) == (r[None, :] // HEAD_DIM)).astype(np.float32) / HEAD_DIM)


def _block_diag_mask():
    r = np.arange(LANES)
    return jnp.asarray(((r[:, None] // HEAD_DIM) == (r[None, :] // HEAD_DIM)).astype(np.float32))


def _prompt_mixer_body(sink_ref, p_ref, kvc_ref, kvp_ref, dmask_ref, qdec_ref, kdec_ref, cdec_ref,
                       bd_ref, avg_ref, gnw_ref, cat_ref, r_ref):
    jb = pl.program_id(1)
    blk = WINDOW
    q_cols = N_ATTN_HEADS * HEAD_DIM
    r_cols = N_RET_HEADS * HEAD_DIM
    kvw = N_KV_HEADS * HEAD_DIM

    @pl.when(jb == 0)
    def _():
        r_ref[...] = jnp.zeros_like(r_ref)

    low = _lane_is_low_half((blk, LANES))
    kband = jnp.concatenate([kvp_ref[:, 0:kvw], kvc_ref[:, 0:kvw]], axis=0)
    vband = jnp.concatenate([kvp_ref[:, kvw:2 * kvw], kvc_ref[:, kvw:2 * kvw]], axis=0)
    low2 = _lane_is_low_half((2 * blk, LANES))
    kswap = pltpu.roll(kband, HEAD_DIM, 1)
    vswap = pltpu.roll(vband, HEAD_DIM, 1)
    qi = lax.broadcasted_iota(jnp.int32, (blk, 2 * blk), 0)
    kj = lax.broadcasted_iota(jnp.int32, (blk, 2 * blk), 1)
    dist = blk + qi - kj
    mask = (dist >= 0) & (dist < WINDOW) & ((kj >= blk) | (jb > 0))
    for kvh in range(N_KV_HEADS):
        keep = low2 if kvh == 0 else jnp.logical_not(low2)
        k2 = jnp.where(keep, kband, kswap).astype(BF16)
        v2 = jnp.where(keep, vband, vswap).astype(BF16)
        pieces = []
        for pr in range(GQA_GROUP // 2):
            c0 = (kvh * (GQA_GROUP // 2) + pr) * LANES
            qp = p_ref[:, c0:c0 + LANES]
            pieces.append(jnp.where(low, qp, jnp.zeros_like(qp)))
            pieces.append(jnp.where(low, jnp.zeros_like(qp), qp))
        qs = jnp.concatenate(pieces, axis=0)
        s = _dot_nt(qs, k2)
        es, inv = [], []
        for hh in range(GQA_GROUP):
            sink = sink_ref[kvh * GQA_GROUP + hh]
            sh = jnp.where(mask, s[hh * blk:(hh + 1) * blk], NEG_INF)
            m = jnp.maximum(jnp.max(sh, axis=-1, keepdims=True), sink)
            e = jnp.exp(sh - m)
            den = jnp.sum(e, axis=-1, keepdims=True) + jnp.exp(sink - m)
            es.append(e.astype(BF16))
            inv.append(1.0 / den)
        o = _dot(jnp.concatenate(es, axis=0), v2)
        for pr in range(GQA_GROUP // 2):
            o_lo = o[(2 * pr) * blk:(2 * pr + 1) * blk] * inv[2 * pr]
            o_hi = o[(2 * pr + 1) * blk:(2 * pr + 2) * blk] * inv[2 * pr + 1]
            c0 = (kvh * (GQA_GROUP // 2) + pr) * LANES
            cat_ref[:, c0:c0 + LANES] = jnp.where(low, o_lo, o_hi).astype(BF16)
    n_pairs = N_RET_HEADS // 2
    pairs = range(n_pairs)
    col = lambda pp, part: slice(q_cols + part * r_cols + pp * LANES, q_cols + part * r_cols + (pp + 1) * LANES)
    pcol = lambda pp: slice(pp * LANES, (pp + 1) * LANES)
    qs = [p_ref[:, col(pp, 0)] for pp in pairs]
    ks = [p_ref[:, col(pp, 1)] for pp in pairs]
    vs = [p_ref[:, col(pp, 2)] for pp in pairs]
    r_prev = [r_ref[0, pp] for pp in pairs]
    zero = jnp.zeros_like(qs[0])
    sc_lo = [_dot_nt(jnp.where(low, qs[pp], zero), ks[pp]) for pp in pairs]
    sc_hi = [_dot_nt(jnp.where(low, zero, qs[pp]), ks[pp]) for pp in pairs]
    qd = [(qs[pp].astype(F32) * qdec_ref[:, pcol(pp)]).astype(BF16) for pp in pairs]
    kd = [(ks[pp].astype(F32) * kdec_ref[:, pcol(pp)]).astype(BF16) for pp in pairs]
    o_cross = [_dot(qd[pp], r_prev[pp].astype(BF16)) for pp in pairs]
    kv_new = [_dot_tn(kd[pp], vs[pp]) for pp in pairs]
    o_lo = [_dot((sc_lo[pp] * dmask_ref[2 * pp]).astype(BF16), vs[pp]) for pp in pairs]
    o_hi = [_dot((sc_hi[pp] * dmask_ref[2 * pp + 1]).astype(BF16), vs[pp]) for pp in pairs]
    for pp in pairs:
        r_ref[0, pp] = cdec_ref[pp] * r_prev[pp] + kv_new[pp] * bd_ref[...]
    o_all = jnp.concatenate([jnp.where(low, o_lo[pp], o_hi[pp]) + o_cross[pp] for pp in pairs], axis=0)
    y_all = _group_norm_pair(o_all, avg_ref[...])
    for pp in pairs:
        gp = p_ref[:, col(pp, 3)].astype(F32)
        y = y_all[pp * blk:(pp + 1) * blk] * gnw_ref[:, pcol(pp)] * _silu(gp)
        cat_ref[:, q_cols + pp * LANES:q_cols + (pp + 1) * LANES] = y.astype(BF16)


def _prompt_mixer(p_p, kv_p, sinks, gn_w, n_pbatch, t_prompt, ret_tabs):
    blk = WINDOW
    nb = t_prompt // blk
    p_cols = p_p.shape[1]
    kv_cols = kv_p.shape[1]
    d_mix = (N_ATTN_HEADS + N_RET_HEADS) * HEAD_DIM
    decay_mask, q_dec, k_dec, chunk_decay = ret_tabs
    n_pairs = N_RET_HEADS // 2
    bd = _block_diag_mask()
    cdec = jnp.stack([bd * chunk_decay[0, pp * LANES:(pp + 1) * LANES][:, None] for pp in range(n_pairs)])
    const2 = lambda shape: pl.BlockSpec(shape, lambda b, j: (0,) * len(shape))
    return pl.pallas_call(
        _prompt_mixer_body,
        grid=(n_pbatch, nb),
        in_specs=[
            pl.BlockSpec(memory_space=pltpu.SMEM),
            pl.BlockSpec((blk, p_cols), lambda b, j: (b * nb + j, 0)),
            pl.BlockSpec((blk, kv_cols), lambda b, j: (b * nb + j, 0)),
            pl.BlockSpec((blk, kv_cols), lambda b, j: (b * nb + jnp.maximum(j - 1, 0), 0)),
            const2((N_RET_HEADS, blk, blk)),
            const2((blk, N_RET_HEADS * HEAD_DIM)),
            const2((blk, N_RET_HEADS * HEAD_DIM)),
            const2((n_pairs, LANES, LANES)),
            const2((LANES, LANES)),
            const2((LANES, LANES)),
            const2((1, N_RET_HEADS * HEAD_DIM)),
        ],
        out_specs=[
            pl.BlockSpec((blk, d_mix), lambda b, j: (b * nb + j, 0)),
            pl.BlockSpec((1, n_pairs, LANES, LANES), lambda b, j: (b, 0, 0, 0)),
        ],
        out_shape=[
            jax.ShapeDtypeStruct((n_pbatch * t_prompt, d_mix), BF16),
            jax.ShapeDtypeStruct((n_pbatch, n_pairs, LANES, LANES), F32),
        ],
        compiler_params=_cparams(("arbitrary", "arbitrary")),
        name="prompt_mixer",
    )(sinks, p_p, kv_p, kv_p, decay_mask, q_dec, k_dec, cdec, bd, _pair_average_matrix(), gn_w.reshape(1, -1))


def _sample_mixer_body(t_new, sink_ref, p_ref, kv_ref, kbuf_ref, vbuf_ref, ret_ref, d4_ref, qdec_ref, kdec_ref,
                       cdec_ref, avg_ref, gnw_ref, cat_ref, knew_ref, vnew_ref, rnew_ref, ka2_ref, va2_ref, qs2_ref):
    w = kbuf_ref.shape[1]
    q_cols = N_ATTN_HEADS * HEAD_DIM
    r_cols = N_RET_HEADS * HEAD_DIM
    kvw = N_KV_HEADS * HEAD_DIM
    n_keys = ka2_ref.shape[1]
    rows_per_head = SUBLANES
    ka2_ref[...] = jnp.zeros_like(ka2_ref)
    va2_ref[...] = jnp.zeros_like(va2_ref)
    qs2_ref[...] = jnp.zeros_like(qs2_ref)
    low4 = _lane_is_low_half((t_new, LANES))
    lowk = _lane_is_low_half((n_keys, LANES))
    n_rows = GQA_GROUP * rows_per_head
    row = lax.broadcasted_iota(jnp.int32, (n_rows, n_keys), 0)
    key = lax.broadcasted_iota(jnp.int32, (n_rows, n_keys), 1)
    t_of_row = row % rows_per_head
    mask = (t_of_row < t_new) & (key > t_of_row) & (key <= t_of_row + WINDOW) & (key < w + t_new)
    head_of_row = lax.broadcasted_iota(jnp.int32, (n_rows, 1), 0) // rows_per_head

    def per_group(g, carry):
        units = range(SAMPLE_GROUP)
        bs = [SAMPLE_GROUP * g + u for u in units]
        items_a = [(u, kvh) for u in units for kvh in range(N_KV_HEADS)]
        items_r = [(u, pp) for u in units for pp in range(N_RET_HEADS // 2)]
        for u in units:
            b, ka_ref, va_ref, qs_ref = bs[u], ka2_ref.at[u], va2_ref.at[u], qs2_ref.at[u]
            ka_ref[0:w, :] = kbuf_ref[b]
            va_ref[0:w, :] = vbuf_ref[b]
            ka_ref[w:w + t_new, :] = kv_ref[b][:, 0:kvw]
            va_ref[w:w + t_new, :] = kv_ref[b][:, kvw:2 * kvw]
            knew_ref[b] = ka_ref[t_new:t_new + w, :]
            vnew_ref[b] = va_ref[t_new:t_new + w, :]
            for h in range(N_ATTN_HEADS):
                c0 = (h // 2) * LANES
                qp = p_ref[b][:, c0:c0 + LANES]
                qs_ref[h * rows_per_head:h * rows_per_head + t_new, :] = jnp.where(
                    low4 if h % 2 == 0 else jnp.logical_not(low4), qp, 0.0)
        kall = [ka2_ref[u] for u in units]
        vall = [va2_ref[u] for u in units]
        kswap = [pltpu.roll(kall[u], HEAD_DIM, 1) for u in units]
        vswap = [pltpu.roll(vall[u], HEAD_DIM, 1) for u in units]
        keep = [lowk, jnp.logical_not(lowk)]
        k2 = {(u, kvh): jnp.where(keep[kvh], kall[u], kswap[u]) for u, kvh in items_a}
        v2 = {(u, kvh): jnp.where(keep[kvh], vall[u], vswap[u]) for u, kvh in items_a}
        sinks = []
        for kvh in range(N_KV_HEADS):
            sink = jnp.zeros((n_rows, 1), F32)
            for hh in range(GQA_GROUP):
                sink = jnp.where(head_of_row == hh, sink_ref[kvh * GQA_GROUP + hh], sink)
            sinks.append(sink)
        s = {it: jnp.where(mask, _dot_nt(qs2_ref[it[0], it[1] * n_rows:(it[1] + 1) * n_rows, :], k2[it], HIGHEST),
                           NEG_INF) for it in items_a}
        m = {it: jnp.maximum(jnp.max(s[it], axis=-1, keepdims=True), sinks[it[1]]) for it in items_a}
        e = {it: jnp.exp(s[it] - m[it]) for it in items_a}
        den = {it: jnp.sum(e[it], axis=-1, keepdims=True) + jnp.exp(sinks[it[1]] - m[it]) for it in items_a}
        o = {it: _dot(e[it], v2[it], HIGHEST) / den[it] for it in items_a}
        for u, kvh in items_a:
            for pr in range(GQA_GROUP // 2):
                o_lo = o[u, kvh][(2 * pr) * rows_per_head:(2 * pr) * rows_per_head + t_new]
                o_hi = o[u, kvh][(2 * pr + 1) * rows_per_head:(2 * pr + 1) * rows_per_head + t_new]
                c0 = (kvh * (GQA_GROUP // 2) + pr) * LANES
                cat_ref[bs[u], :, c0:c0 + LANES] = jnp.where(low4, o_lo, o_hi)
        col = lambda pp, part: slice(q_cols + part * r_cols + pp * LANES, q_cols + part * r_cols + (pp + 1) * LANES)
        pcol = lambda pp: slice(pp * LANES, (pp + 1) * LANES)
        qp = {it: p_ref[bs[it[0]]][:, col(it[1], 0)] for it in items_r}
        kp = {it: p_ref[bs[it[0]]][:, col(it[1], 1)] for it in items_r}
        vp = {it: p_ref[bs[it[0]]][:, col(it[1], 2)] for it in items_r}
        r0 = {it: ret_ref[bs[it[0]], it[1]] for it in items_r}
        q_lo = {it: jnp.where(low4, qp[it], 0.0) for it in items_r}
        q_hi = {it: jnp.where(low4, 0.0, qp[it]) for it in items_r}
        sc_lo = {it: _dot_nt(q_lo[it], kp[it], HIGHEST) * d4_ref[2 * it[1]] for it in items_r}
        sc_hi = {it: _dot_nt(q_hi[it], kp[it], HIGHEST) * d4_ref[2 * it[1] + 1] for it in items_r}
        oc_lo = {it: _dot(q_lo[it] * qdec_ref[:, pcol(it[1])], r0[it], HIGHEST) for it in items_r}
        oc_hi = {it: _dot(q_hi[it] * qdec_ref[:, pcol(it[1])], r0[it], HIGHEST) for it in items_r}
        kv_full = {it: _dot_tn(kp[it] * kdec_ref[:, pcol(it[1])], vp[it], HIGHEST) for it in items_r}
        oi_lo = {it: _dot(sc_lo[it], vp[it], HIGHEST) for it in items_r}
        oi_hi = {it: _dot(sc_hi[it], vp[it], HIGHEST) for it in items_r}
        for it in items_r:
            kv_pair = jnp.concatenate([kv_full[it][0:HEAD_DIM, 0:HEAD_DIM], kv_full[it][HEAD_DIM:, HEAD_DIM:]], axis=0)
            rnew_ref[bs[it[0]], it[1]] = cdec_ref[it[1]] * r0[it] + kv_pair
        o_ret = [jnp.where(low4, oi_lo[it], oi_hi[it]) + jnp.concatenate([oc_lo[it], oc_hi[it]], axis=1)
                 for it in items_r]
        y_ret = _group_norm_pairs(o_ret, avg_ref[...])
        for it, y in zip(items_r, y_ret):
            gp = p_ref[bs[it[0]]][:, col(it[1], 3)]
            cat_ref[bs[it[0]], :, q_cols + it[1] * LANES:q_cols + (it[1] + 1) * LANES] = (
                y * gnw_ref[:, pcol(it[1])] * _silu(gp))
        return carry

    lax.fori_loop(0, kbuf_ref.shape[0] // SAMPLE_GROUP, per_group, 0)


def _sample_mixer(p_s, kv_s, k_buf, v_buf, ret0, sinks, gn_w, ret_tabs):
    nb, t_new, p_cols = p_s.shape
    kv_cols = kv_s.shape[2]
    w = k_buf.shape[1]
    bb = SAMPLE_BATCH_TILE
    n_pairs = N_RET_HEADS // 2
    d_mix = (N_ATTN_HEADS + N_RET_HEADS) * HEAD_DIM
    decay_mask, q_dec, k_dec, chunk_decay = ret_tabs
    n_keys = ((w + t_new + SUBLANES - 1) // SUBLANES) * SUBLANES
    cdec = jnp.stack([jnp.broadcast_to(chunk_decay[0, pp * LANES:(pp + 1) * LANES][:, None], (LANES, HEAD_DIM))
                      for pp in range(n_pairs)])
    const1 = lambda shape: pl.BlockSpec(shape, lambda i: (0,) * len(shape))
    body = functools.partial(_sample_mixer_body, t_new)
    return pl.pallas_call(
        body,
        grid=(nb // bb,),
        in_specs=[
            pl.BlockSpec(memory_space=pltpu.SMEM),
            pl.BlockSpec((bb, t_new, p_cols), lambda i: (i, 0, 0)),
            pl.BlockSpec((bb, t_new, kv_cols), lambda i: (i, 0, 0)),
            pl.BlockSpec((bb, w, LANES), lambda i: (i, 0, 0)),
            pl.BlockSpec((bb, w, LANES), lambda i: (i, 0, 0)),
            pl.BlockSpec((bb, n_pairs, LANES, HEAD_DIM), lambda i: (i, 0, 0, 0)),
            const1((N_RET_HEADS, t_new, t_new)),
            const1((t_new, N_RET_HEADS * HEAD_DIM)),
            const1((t_new, N_RET_HEADS * HEAD_DIM)),
            const1((n_pairs, LANES, HEAD_DIM)),
            const1((LANES, LANES)),
            const1((1, N_RET_HEADS * HEAD_DIM)),
        ],
        out_specs=[
            pl.BlockSpec((bb, t_new, d_mix), lambda i: (i, 0, 0)),
            pl.BlockSpec((bb, w, LANES), lambda i: (i, 0, 0)),
            pl.BlockSpec((bb, w, LANES), lambda i: (i, 0, 0)),
            pl.BlockSpec((bb, n_pairs, LANES, HEAD_DIM), lambda i: (i, 0, 0, 0)),
        ],
        out_shape=[
            jax.ShapeDtypeStruct((nb, t_new, d_mix), F32),
            jax.ShapeDtypeStruct((nb, w, LANES), F32),
            jax.ShapeDtypeStruct((nb, w, LANES), F32),
            jax.ShapeDtypeStruct((nb, n_pairs, LANES, HEAD_DIM), F32),
        ],
        scratch_shapes=[
            pltpu.VMEM((SAMPLE_GROUP, n_keys, LANES), F32),
            pltpu.VMEM((SAMPLE_GROUP, n_keys, LANES), F32),
            pltpu.VMEM((SAMPLE_GROUP, N_ATTN_HEADS * SUBLANES, LANES), F32),
        ],
        compiler_params=_cparams(("arbitrary",)),
        name="sample_mixer",
    )(sinks, p_s, kv_s, k_buf, v_buf, ret0, decay_mask, q_dec, k_dec, cdec, _pair_average_matrix(), gn_w.reshape(1, -1))


def _router_body(geo, xp_ref, xs_ref, catp_ref, cats_ref, wout_ref, g1p_ref, g1s_ref, nw_ref, shp_ref, scp_ref,
                 shs_ref, scs_ref, wr_ref, br_ref, tri_ref, x1_ref, h2_ref, idx_ref, gate_ref, rank_ref, cnt_ref):
    i = pl.program_id(0)
    tv = functools.partial(_tile_vec, i, geo)
    cat = _tile_rows(i, geo, catp_ref, cats_ref)
    x1 = _tile_rows(i, geo, xp_ref, xs_ref) + tv(g1p_ref, g1s_ref) * _dot(cat, wout_ref[...])
    x1_ref[...] = x1
    h2 = _rms_norm(x1, nw_ref[...]) * (1.0 + tv(scp_ref, scs_ref)) + tv(shp_ref, shs_ref)
    h2_hi = h2.astype(BF16)
    h2_ref[...] = h2_hi.reshape(h2_ref.shape)
    h2_lo = (h2 - h2_hi.astype(F32)).astype(BF16)
    w_parts = wr_ref[...]
    by_hi = _dot_nt(w_parts, h2_hi)
    logits = by_hi[:N_EXPERTS] + by_hi[N_EXPERTS:] + _dot_nt(w_parts[:N_EXPERTS], h2_lo) + br_ref[...]
    e_iota = lax.broadcasted_iota(jnp.int32, logits.shape, 0)
    vals, sels = [], []
    for _ in range(TOP_K):
        m = jnp.max(logits, axis=0, keepdims=True)
        sel = jnp.min(jnp.where(logits == m, e_iota, N_EXPERTS), axis=0, keepdims=True)
        vals.append(m)
        sels.append(sel)
        logits = jnp.where(e_iota == sel, -jnp.inf, logits)
    es = [jnp.exp(v - vals[0]) for v in vals]
    den = es[0]
    for e in es[1:]:
        den = den + e

    @pl.when(i == 0)
    def _():
        cnt_ref[...] = jnp.zeros_like(cnt_ref)

    hits = [e_iota == sels[k] for k in range(TOP_K)]
    onehots = [jnp.where(h, 1.0, 0.0) for h in hits]
    befores = [_dot(o.astype(BF16), tri_ref[...]) for o in onehots]
    totals = [jnp.sum(o, axis=1, keepdims=True) for o in onehots]
    base = cnt_ref[...]
    for k in range(TOP_K):
        rank = jnp.sum(jnp.where(hits[k], base + befores[k], 0.0), axis=0, keepdims=True)
        base = base + totals[k]
        idx_ref[k:k + 1, :] = sels[k]
        gate_ref[k:k + 1, :] = es[k] / den
        rank_ref[k:k + 1, :] = rank.astype(jnp.int32)
    cnt_ref[...] = base


def _outproj_router(x_p, x_s, cat_p, cat_s, w_out_bf16, norm_w, mod_p, mod_s, layer, w_router_t, b_router, geo):
    n_p, d = x_p.shape
    n = n_p + x_s.shape[0]
    d_mix = cat_p.shape[1]
    pv = lambda chunk: _prompt_vec_spec(layer, chunk, d)
    sv = lambda chunk: _sample_vec_spec(layer, chunk, geo, d)
    tile = lambda cols: pl.BlockSpec((TOKEN_TILE, cols), lambda i: (i, 0))
    choice = pl.BlockSpec((TOP_K, TOKEN_TILE), lambda i: (0, i))
    tri = jnp.asarray(np.triu(np.ones((TOKEN_TILE, TOKEN_TILE), np.float32), 1)).astype(BF16)
    w_hi = w_router_t.astype(BF16)
    w_router_parts = jnp.concatenate([w_hi, (w_router_t - w_hi.astype(F32)).astype(BF16)], axis=0)
    return pl.pallas_call(
        functools.partial(_router_body, geo),
        grid=(n // TOKEN_TILE,),
        in_specs=[
            _prompt_rows_spec(geo, d), _sample_rows_spec(geo, d),
            _prompt_rows_spec(geo, d_mix), _sample_rows_spec(geo, d_mix),
            pl.BlockSpec((d_mix, d), lambda i: (0, 0)),
            pv(2), sv(2),
            pl.BlockSpec((1, d), lambda i: (0, 0)),
            pv(3), pv(4), sv(3), sv(4),
            pl.BlockSpec((2 * N_EXPERTS, d), lambda i: (0, 0)),
            pl.BlockSpec((N_EXPERTS, 1), lambda i: (0, 0)),
            pl.BlockSpec((TOKEN_TILE, TOKEN_TILE), lambda i: (0, 0)),
        ],
        out_specs=[
            tile(d),
            pl.BlockSpec((TOKEN_TILE, d // LANES, LANES), lambda i: (i, 0, 0)),
            choice, choice, choice,
            pl.BlockSpec((N_EXPERTS, 1), lambda i: (0, 0)),
        ],
        out_shape=[
            jax.ShapeDtypeStruct((n, d), F32),
            jax.ShapeDtypeStruct((n, d // LANES, LANES), BF16),
            jax.ShapeDtypeStruct((TOP_K, n), jnp.int32),
            jax.ShapeDtypeStruct((TOP_K, n), F32),
            jax.ShapeDtypeStruct((TOP_K, n), jnp.int32),
            jax.ShapeDtypeStruct((N_EXPERTS, 1), F32),
        ],
        compiler_params=_cparams(("arbitrary",)),
        name="outproj_router",
    )(x_p, x_s, cat_p, cat_s, w_out_bf16, mod_p, mod_s, norm_w.reshape(1, d), mod_p, mod_p, mod_s, mod_s,
      w_router_parts, b_router.reshape(N_EXPERTS, 1), tri)


def _routing_plan(top_idx, rank, counts_f, tm):
    k, n = top_idx.shape
    n_pairs = k * n
    pad_step = TOP_K * TOKEN_TILE
    n_tiles = (n_pairs + N_EXPERTS * (tm - 1)) // tm + 1
    while (n_tiles * tm - n_pairs) % pad_step:
        n_tiles += 1
    counts = counts_f[:, 0].astype(jnp.int32)
    tiles_per = (counts + tm - 1) // tm
    tile_end = jnp.cumsum(tiles_per)
    tile_start = tile_end - tiles_per
    n_used = tile_end[-1]
    n_slots = n_tiles * tm

    def lookup(table, index):
        ids = jnp.arange(table.shape[0], dtype=jnp.int32).reshape((-1,) + (1,) * index.ndim)
        return jnp.sum(jnp.where(index[None] == ids, table.reshape(ids.shape), 0), axis=0)

    slot_of_pair = lookup(tile_start * tm, top_idx) + rank
    pad_counts = jnp.concatenate([tiles_per * tm - counts, (n_slots - n_used * tm).reshape(1)])
    pad_end = jnp.cumsum(pad_counts)
    first_pad_slot = jnp.concatenate([tile_start * tm + counts, (n_used * tm).reshape(1)])
    k_pad = jnp.arange(n_slots - n_pairs, dtype=jnp.int32)
    seg = jnp.sum(k_pad[:, None] >= pad_end[None, :], axis=1).astype(jnp.int32)
    pad_slot = k_pad + lookup(first_pad_slot - (pad_end - pad_counts), seg)
    tile_ids = jnp.arange(n_tiles, dtype=jnp.int32)
    tile_expert = jnp.minimum(jnp.sum(tile_ids[:, None] >= tile_end[None, :], axis=1), N_EXPERTS - 1).astype(jnp.int32)
    last_expert = tile_expert[jnp.maximum(n_used - 1, 0)]
    tile_expert = jnp.where(tile_ids < n_used, tile_expert, last_expert)
    return tile_expert, n_used.reshape(1).astype(jnp.int32), slot_of_pair, pad_slot


def _issue_row_copies(src_hbm, row_of, dst_row, sem, n_rows):
    def group(g, c):
        for u in range(DMA_UNROLL):
            r = g * DMA_UNROLL + u
            pltpu.make_async_copy(src_hbm.at[row_of(r)], dst_row(r), sem).start(priority=u % 2)
        return c
    lax.fori_loop(0, n_rows // DMA_UNROLL, group, 0)


def _issue_row_copies_static(src_hbm, row_of, dst_row, sem, lo, hi):
    for r in range(lo, hi):
        pltpu.make_async_copy(src_hbm.at[row_of(r)], dst_row(r), sem).start(priority=r % 2)


def _wait_row_copies(src_hbm, dst_buf, sem):
    pltpu.make_async_copy(src_hbm.at[pl.ds(0, dst_buf.shape[0])], dst_buf, sem).wait()


def _dispatch_body(n_token_steps, slots_ref, h2_ref, xs_hbm, zero_s, sem):
    i = pl.program_id(0)
    tm = TOKEN_TILE

    @pl.when(i == 0)
    def _():
        zero_s[...] = jnp.zeros_like(zero_s)

    def scatter(src):
        for k in range(TOP_K):
            for r in range(tm):
                pltpu.make_async_copy(src.at[r], xs_hbm.at[slots_ref[0, k, r]], sem).start(priority=r % 2)
        for k in range(TOP_K):
            pltpu.make_async_copy(src, xs_hbm.at[pl.ds(0, tm)], sem).wait()

    @pl.when(i < n_token_steps)
    def _():
        scatter(h2_ref)

    @pl.when(i >= n_token_steps)
    def _():
        scatter(zero_s)


def _dispatch(h2_rows, slot_of_pair, pad_slot, n_slots):
    n = h2_rows.shape[0]
    row_tile = h2_rows.shape[1:]
    tm = TOKEN_TILE
    n_token_steps = n // tm
    slots = jnp.concatenate([slot_of_pair.reshape(TOP_K, n_token_steps, tm).transpose(1, 0, 2),
                             pad_slot.reshape(-1, TOP_K, tm)], axis=0)
    n_steps = slots.shape[0]
    return pl.pallas_call(
        functools.partial(_dispatch_body, n_token_steps),
        grid=(n_steps,),
        in_specs=[
            pl.BlockSpec((1, TOP_K, tm), lambda i: (i, 0, 0), memory_space=pltpu.SMEM),
            pl.BlockSpec((tm,) + row_tile, lambda i: (jnp.minimum(i, n_token_steps - 1), 0, 0)),
        ],
        out_specs=pl.BlockSpec(memory_space=pl.ANY),
        out_shape=jax.ShapeDtypeStruct((n_slots,) + row_tile, h2_rows.dtype),
        scratch_shapes=[pltpu.VMEM((tm,) + row_tile, h2_rows.dtype), pltpu.SemaphoreType.DMA(())],
        compiler_params=_cparams(("arbitrary",)),
        name="moe_dispatch",
    )(slots, h2_rows)


def _expert_body(te_ref, nt_ref, x_ref, wgu_ref, bg_ref, bu_ref, wd_ref, bd_ref, perm_ref, y_ref, wgu_s, wd_s, act_s):
    j = pl.program_id(0)
    n_used = nt_ref[0]
    tm = EXPERT_TILE
    f2 = wgu_ref.shape[2]
    n_chunks = f2 // MXU_WIDTH
    half = MXU_WIDTH // 2

    new_expert = jnp.logical_or(j == 0, te_ref[j] != te_ref[jnp.maximum(j - 1, 0)])

    @pl.when(jnp.logical_and(j < n_used, new_expert))
    def _():
        for c in range(n_chunks):
            cols = slice(c * MXU_WIDTH, (c + 1) * MXU_WIDTH)
            wgu_s[:, cols] = _dot(wgu_ref[0, :, cols].astype(BF16), perm_ref[...]).astype(BF16)
        wd_s[...] = wd_ref[0].astype(BF16)

    @pl.when(j < n_used)
    def _():
        x = x_ref[...].reshape(tm, wgu_ref.shape[1])
        for c in range(n_chunks):
            gu = _dot(x, wgu_s[:, c * MXU_WIDTH:(c + 1) * MXU_WIDTH])
            glu = jnp.minimum(gu[:, :half] + bg_ref[0, :, c * half:(c + 1) * half], SWIGLU_LIMIT)
            lin = jnp.clip(gu[:, half:] + bu_ref[0, :, c * half:(c + 1) * half], -SWIGLU_LIMIT, SWIGLU_LIMIT)
            act_s[:, c * half:(c + 1) * half] = (glu * jax.nn.sigmoid(SWIGLU_ALPHA * glu) * (lin + 1.0)).astype(BF16)
        y = _dot(act_s[...], wd_s[...]) + bd_ref[0]
        y_ref[...] = y.astype(BF16).reshape(y_ref.shape)

    @pl.when(j >= n_used)
    def _():
        y_ref[...] = jnp.zeros_like(y_ref)


def _deinterleave_matrix():
    m = np.zeros((MXU_WIDTH, MXU_WIDTH), np.float32)
    j = np.arange(MXU_WIDTH // 2)
    m[2 * j, j] = 1.0
    m[2 * j + 1, MXU_WIDTH // 2 + j] = 1.0
    return jnp.asarray(m).astype(BF16)


def _routed_experts(x_rows, tile_expert, n_used, w_gate_up, bg, bu, w_down, bd, layer):
    d = w_gate_up.shape[2]
    f2 = w_gate_up.shape[3]
    f = f2 // 2
    tm = EXPERT_TILE
    n_tiles = x_rows.shape[0] // tm
    row_tile = x_rows.shape[1:]
    by_expert = lambda shape: pl.BlockSpec(shape, lambda j, te, nt: (te[j], 0, 0))
    layer_expert = lambda shape: pl.BlockSpec((None,) + shape, lambda j, te, nt: (layer, te[j], 0, 0))
    grid_spec = pltpu.PrefetchScalarGridSpec(
        num_scalar_prefetch=2,
        grid=(n_tiles,),
        in_specs=[
            pl.BlockSpec((tm,) + row_tile, lambda j, te, nt: (j, 0, 0)),
            layer_expert((1, d, f2)),
            by_expert((1, 1, f)), by_expert((1, 1, f)),
            layer_expert((1, f, d)),
            by_expert((1, 1, d)),
            pl.BlockSpec((MXU_WIDTH, MXU_WIDTH), lambda j, te, nt: (0, 0)),
        ],
        out_specs=pl.BlockSpec((tm,) + row_tile, lambda j, te, nt: (j, 0, 0)),
        scratch_shapes=[
            pltpu.VMEM((d, f2), BF16),
            pltpu.VMEM((f, d), BF16),
            pltpu.VMEM((tm, f), BF16),
        ],
    )
    return pl.pallas_call(
        _expert_body,
        grid_spec=grid_spec,
        out_shape=jax.ShapeDtypeStruct((n_tiles * tm,) + row_tile, BF16),
        compiler_params=_cparams(("arbitrary",)),
        name="routed_experts",
    )(tile_expert, n_used, x_rows, w_gate_up, bg, bu, w_down, bd, _deinterleave_matrix())


def _combine_body(geo, final_norm, slots_cur_ref, slots_nxt_ref, x1_ref, gate_ref, g2p_ref, g2s_ref, nfw_ref, y_hbm,
                  outp_ref, outs_ref, ybuf_even, ybuf_odd, sems):
    i = pl.program_id(0)
    n_steps = pl.num_programs(0)
    tm = TOKEN_TILE

    @pl.when(i == 0)
    def _():
        for k in range(TOP_K):
            _issue_row_copies(y_hbm, lambda r, k=k: slots_cur_ref[0, k, r], lambda r, k=k: ybuf_even.at[k, r],
                              sems.at[0], tm)

    def step(cur, nxt, sem_cur, sem_nxt):
        for k in range(TOP_K):
            _wait_row_copies(y_hbm, cur.at[k], sem_cur)
        for k in range(TOP_K):
            _issue_row_copies_static(y_hbm, lambda r, k=k: slots_nxt_ref[0, k, r], lambda r, k=k: nxt.at[k, r],
                                     sem_nxt, 0, tm)
        rows = lambda k: cur[k].reshape(tm, x1_ref.shape[1]).astype(F32)
        acc = rows(0) * gate_ref[:, 0:1]
        for k in range(1, TOP_K):
            acc = acc + rows(k) * gate_ref[:, k:k + 1]
        x2 = x1_ref[...] + _tile_vec(i, geo, g2p_ref, g2s_ref) * acc
        if final_norm:
            x2 = _rms_norm(x2, nfw_ref[...])

        @pl.when(i < geo.n_ptiles)
        def _():
            outp_ref[...] = x2

        @pl.when(i >= geo.n_ptiles)
        def _():
            outs_ref[...] = x2

        @pl.when(i == n_steps - 1)
        def _():
            for k in range(TOP_K):
                _wait_row_copies(y_hbm, nxt.at[k], sem_nxt)

    @pl.when(i % 2 == 0)
    def _():
        step(ybuf_even, ybuf_odd, sems.at[0], sems.at[1])

    @pl.when(i % 2 == 1)
    def _():
        step(ybuf_odd, ybuf_even, sems.at[1], sems.at[0])


def _combine(x1, y_rows, slot_of_pair, gates_t, mod_p, mod_s, layer, final_w, geo, final_norm):
    n, d = x1.shape
    tm = TOKEN_TILE
    n_steps = n // tm
    n_p = geo.n_ptiles * tm
    slots = slot_of_pair.reshape(TOP_K, n_steps, tm).transpose(1, 0, 2)
    return pl.pallas_call(
        functools.partial(_combine_body, geo, final_norm),
        grid=(n_steps,),
        in_specs=[
            pl.BlockSpec((1, TOP_K, tm), lambda i: (i, 0, 0), memory_space=pltpu.SMEM),
            pl.BlockSpec((1, TOP_K, tm), lambda i: (jnp.minimum(i + 1, n_steps - 1), 0, 0), memory_space=pltpu.SMEM),
            pl.BlockSpec((tm, d), lambda i: (i, 0)),
            pl.BlockSpec((tm, TOP_K), lambda i: (i, 0)),
            _prompt_vec_spec(layer, 5, d),
            _sample_vec_spec(layer, 5, geo, d),
            pl.BlockSpec((1, d), lambda i: (0, 0)),
            pl.BlockSpec(memory_space=pl.ANY),
        ],
        out_specs=[_prompt_rows_spec(geo, d), _sample_rows_spec(geo, d)],
        out_shape=[jax.ShapeDtypeStruct((n_p, d), F32), jax.ShapeDtypeStruct((n - n_p, d), F32)],
        scratch_shapes=[pltpu.VMEM((TOP_K, tm) + y_rows.shape[1:], y_rows.dtype),
                        pltpu.VMEM((TOP_K, tm) + y_rows.shape[1:], y_rows.dtype),
                        pltpu.SemaphoreType.DMA((2,))],
        compiler_params=_cparams(("arbitrary",)),
        name="moe_combine",
    )(slots, slots, x1, gates_t, mod_p, mod_s, final_w.reshape(1, d), y_rows)


def kernel(x_prompt, x_sample, state_swa_k, state_swa_v, state_ret, c_prompt, c_sample, norm_mix_w, w_ada, b_ada, w_in, attn_sinks, ret_gn_w, w_out, norm_ffn_w, w_router, b_router, w_gate_up, b_gate_up, w_down, b_down, norm_final_w):
    n_pbatch, t_prompt, d = x_prompt.shape
    n_sbatch, t_sample, _ = x_sample.shape
    depth = w_in.shape[0]
    n_p = n_pbatch * t_prompt
    n_s = n_sbatch * t_sample
    win = state_swa_k.shape[2]
    assert d == (N_ATTN_HEADS + N_RET_HEADS) * HEAD_DIM and w_gate_up.shape[1] == N_EXPERTS
    assert t_prompt % TOKEN_TILE == 0 and n_s % TOKEN_TILE == 0 and TOKEN_TILE % t_sample == 0
    assert t_prompt % WINDOW == 0 and RET_CHUNK == WINDOW and win == WINDOW and t_sample <= SUBLANES
    assert n_sbatch % SAMPLE_BATCH_TILE == 0 and n_pbatch <= SUBLANES
    assert w_gate_up.shape[3] % MXU_WIDTH == 0 and EXPERT_TILE % DMA_UNROLL == 0 and TOKEN_TILE % DMA_UNROLL == 0
    geo = _Geometry(n_p // TOKEN_TILE, t_prompt // TOKEN_TILE, n_pbatch)

    c_all = jnp.concatenate([c_prompt, jnp.zeros((SUBLANES - n_pbatch, d), F32), c_sample], axis=0)
    mod = _ada_modulation(c_all, w_ada, b_ada)
    mod_p = mod[:, :SUBLANES]
    mod_s = jnp.repeat(mod[:, SUBLANES:], t_sample, axis=1)

    rope_tab = _rope_tables(t_prompt, t_sample)
    ret_tabs_p = _retention_tables(RET_CHUNK)
    ret_tabs_s = _retention_tables(t_sample)
    f = w_gate_up.shape[3] // 2

    x_p = x_prompt.reshape(n_p, d)
    x_s = x_sample.reshape(n_s, d)
    kp_l, vp_l, rp_l, ks_l, vs_l, rs_l = [], [], [], [], [], []
    for l in range(depth):
        p_all, kv_all = _input_projection(x_p, x_s, norm_mix_w[l], mod_p, mod_s, l, w_in[l].astype(BF16), rope_tab, geo)
        cat_p, r_pairs = _prompt_mixer(p_all, kv_all, attn_sinks[l], ret_gn_w[l], n_pbatch, t_prompt, ret_tabs_p)
        p_s = p_all[n_p:].astype(F32)
        kv_s = kv_all[n_p:]
        kv_p = kv_all[:n_p]
        cat_s, k_new, v_new, r_new = _sample_mixer(
            p_s.reshape(n_sbatch, t_sample, -1), kv_s.reshape(n_sbatch, t_sample, -1),
            state_swa_k[l].reshape(n_sbatch, win, LANES), state_swa_v[l].reshape(n_sbatch, win, LANES),
            state_ret[l].reshape(n_sbatch, N_RET_HEADS // 2, LANES, HEAD_DIM), attn_sinks[l], ret_gn_w[l], ret_tabs_s)
        x1, h2_rows, top_idx, gates, rank, counts = _outproj_router(
            x_p, x_s, cat_p, cat_s.reshape(n_s, d), w_out[l].astype(BF16), norm_ffn_w[l], mod_p, mod_s, l,
            w_router[l].T, b_router[l], geo)
        tile_expert, n_used, slot_of_pair, pad_slot = _routing_plan(top_idx, rank, counts, EXPERT_TILE)
        x_rows = _dispatch(h2_rows, slot_of_pair, pad_slot, tile_expert.shape[0] * EXPERT_TILE)
        b_gu = b_gate_up[l].reshape(N_EXPERTS, 1, f, 2)
        y_rows = _routed_experts(x_rows, tile_expert, n_used, w_gate_up,
                                 b_gu[..., 0], b_gu[..., 1], w_down, b_down[l][:, None, :], l)
        x_p, x_s = _combine(x1, y_rows, slot_of_pair, gates.T, mod_p, mod_s, l, norm_final_w, geo,
                            final_norm=(l == depth - 1))

        kv_last = kv_p.reshape(n_pbatch, t_prompt, -1)[:, t_prompt - WINDOW:]
        kv_last = kv_last.reshape(n_pbatch, WINDOW, 2, N_KV_HEADS, HEAD_DIM)
        kp_l.append(kv_last[:, :, 0])
        vp_l.append(kv_last[:, :, 1])
        rp_l.append(jnp.stack([r_pairs[:, :, :HEAD_DIM, :HEAD_DIM], r_pairs[:, :, HEAD_DIM:, HEAD_DIM:]], axis=2)
                    .reshape(n_pbatch, N_RET_HEADS, HEAD_DIM, HEAD_DIM))
        ks_l.append(k_new.reshape(n_sbatch, win, N_KV_HEADS, HEAD_DIM))
        vs_l.append(v_new.reshape(n_sbatch, win, N_KV_HEADS, HEAD_DIM))
        rs_l.append(r_new.reshape(n_sbatch, N_RET_HEADS, HEAD_DIM, HEAD_DIM))
    return (x_p.reshape(n_pbatch, t_prompt, d), x_s.reshape(n_sbatch, t_sample, d), jnp.stack(kp_l), jnp.stack(vp_l),
            jnp.stack(rp_l), jnp.stack(ks_l), jnp.stack(vs_l), jnp.stack(rs_l))
```

```python
import functools

import numpy as np
import jax
import jax.numpy as jnp
from jax import lax
from jax.experimental import pallas as pl
from jax.experimental.pallas import tpu as pltpu

F32 = jnp.float32
BF16 = jnp.bfloat16
HIGHEST = lax.Precision.HIGHEST

HEAD_DIM = 64
N_ATTN_HEADS = 8
N_KV_HEADS = 2
GQA_GROUP = N_ATTN_HEADS // N_KV_HEADS
WINDOW = 128
ROPE_THETA = 500000.0
ROPE_DIMS = HEAD_DIM // 4
N_RET_HEADS = 8
RET_CHUNK = 128
RET_THETA = 10000.0
N_EXPERTS = 32
TOP_K = 4
SWIGLU_LIMIT = 7.0
SWIGLU_ALPHA = 1.702
NORM_EPS = 1e-5
PAST_LEN = 16384

LANES = 128
SUBLANES = 8
MXU_WIDTH = 256
VMEM_LIMIT_BYTES = 56 * 1024 * 1024

TOKEN_TILE = 256
EXPERT_TILE = 256
SAMPLE_BATCH_TILE = 8
SAMPLE_GROUP = 4
DMA_UNROLL = 8

NEG_INF = -1e30


def _cparams(semantics):
    return pltpu.CompilerParams(dimension_semantics=semantics, vmem_limit_bytes=VMEM_LIMIT_BYTES)


def _lane_is_low_half(shape):
    return lax.broadcasted_iota(jnp.int32, shape, len(shape) - 1) < HEAD_DIM


def _dot(a, b, precision=None):
    return jnp.dot(a, b, preferred_element_type=F32, precision=precision)


def _dot_nt(a, b, precision=None):
    return lax.dot_general(a, b, (((1,), (1,)), ((), ())), preferred_element_type=F32, precision=precision)


def _dot_tn(a, b, precision=None):
    return lax.dot_general(a, b, (((0,), (0,)), ((), ())), preferred_element_type=F32, precision=precision)


def _rms_norm(x, w):
    return x * lax.rsqrt(jnp.mean(x * x, axis=-1, keepdims=True) + NORM_EPS) * w


def _silu(x):
    return x * jax.nn.sigmoid(x)


def _ada_body(c_ref, w_ref, b_ref, o_ref):
    a = _silu(c_ref[...])
    o_ref[0] = _dot(a, w_ref[0], HIGHEST) + b_ref[0]


def _ada_modulation(c_all, w_ada, b_ada):
    depth, d, cols = w_ada.shape
    rows = c_all.shape[0]
    tn = 1024
    return pl.pallas_call(
        _ada_body,
        grid=(depth, cols // tn),
        in_specs=[
            pl.BlockSpec((rows, d), lambda l, j: (0, 0)),
            pl.BlockSpec((1, d, tn), lambda l, j: (l, 0, j)),
            pl.BlockSpec((1, 1, tn), lambda l, j: (l, 0, j)),
        ],
        out_specs=pl.BlockSpec((1, rows, tn), lambda l, j: (l, 0, j)),
        out_shape=jax.ShapeDtypeStruct((depth, rows, cols), F32),
        compiler_params=_cparams(("arbitrary", "arbitrary")),
        name="ada_modulation",
    )(c_all, w_ada, b_ada.reshape(depth, 1, cols))


class _Geometry:
    def __init__(self, n_ptiles, tiles_per_batch, n_pbatch):
        self.n_ptiles = n_ptiles
        self.tiles_per_batch = tiles_per_batch
        self.n_pbatch = n_pbatch


def _tile_vec(i, geo, prompt_ref, sample_ref):
    b = jnp.minimum(i // geo.tiles_per_batch, geo.n_pbatch - 1)
    return jnp.where(i >= geo.n_ptiles, sample_ref[...], prompt_ref[pl.ds(b, 1), :])


def _tile_rows(i, geo, prompt_ref, sample_ref):
    return jnp.where(i >= geo.n_ptiles, sample_ref[...].astype(prompt_ref.dtype), prompt_ref[...])


def _prompt_vec_spec(layer, chunk, d):
    return pl.BlockSpec((None, SUBLANES, d), lambda i: (layer, 0, chunk))


def _sample_vec_spec(layer, chunk, geo, d):
    return pl.BlockSpec((None, TOKEN_TILE, d), lambda i: (layer, jnp.maximum(i - geo.n_ptiles, 0), chunk))


def _prompt_rows_spec(geo, cols):
    return pl.BlockSpec((TOKEN_TILE, cols), lambda i: (jnp.minimum(i, geo.n_ptiles - 1), 0))


def _sample_rows_spec(geo, cols):
    return pl.BlockSpec((TOKEN_TILE, cols), lambda i: (jnp.maximum(i - geo.n_ptiles, 0), 0))


def _rotate(xc, tab_ref, base, shift):
    return (xc * tab_ref[base] + pltpu.roll(xc, shift, 1) * tab_ref[base + 1]
            + pltpu.roll(xc, LANES - shift, 1) * tab_ref[base + 2])


def _inproj_body(geo, xp_ref, xs_ref, nw_ref, shp_ref, scp_ref, shs_ref, scs_ref, w_ref, tab_ref, p_ref, kv_ref):
    i = pl.program_id(0)
    x = _tile_rows(i, geo, xp_ref, xs_ref)
    h = _rms_norm(x, nw_ref[...]) * (1.0 + _tile_vec(i, geo, scp_ref, scs_ref)) + _tile_vec(i, geo, shp_ref, shs_ref)
    proj = _dot(h.astype(BF16), w_ref[...])
    q_cols = N_ATTN_HEADS * HEAD_DIM
    kv_cols = N_KV_HEADS * HEAD_DIM
    r_cols = N_RET_HEADS * HEAD_DIM
    o_ka = q_cols
    o_va = o_ka + kv_cols
    o_qr = o_va + kv_cols
    o_kr = o_qr + r_cols
    o_vr = o_kr + r_cols
    o_g = o_vr + r_cols
    attn_scale = HEAD_DIM ** -0.5
    ret_scale = HEAD_DIM ** -0.5
    half_a = ROPE_DIMS // 2
    half_r = HEAD_DIM // 2

    def emit(p_ref, kv_ref):
        dt = p_ref.dtype
        for c in range(q_cols // LANES):
            xc = proj[:, c * LANES:(c + 1) * LANES]
            p_ref[:, c * LANES:(c + 1) * LANES] = (_rotate(xc, tab_ref, 0, half_a) * attn_scale).astype(dt)
        kv_ref[:, 0:kv_cols] = _rotate(proj[:, o_ka:o_ka + kv_cols], tab_ref, 0, half_a)
        kv_ref[:, kv_cols:2 * kv_cols] = proj[:, o_va:o_va + kv_cols]
        for c in range(r_cols // LANES):
            xq = proj[:, o_qr + c * LANES:o_qr + (c + 1) * LANES]
            xk = proj[:, o_kr + c * LANES:o_kr + (c + 1) * LANES]
            p_ref[:, q_cols + c * LANES:q_cols + (c + 1) * LANES] = _rotate(xq, tab_ref, 3, half_r).astype(dt)
            p_ref[:, q_cols + r_cols + c * LANES:q_cols + r_cols + (c + 1) * LANES] = (
                _rotate(xk, tab_ref, 3, half_r) * ret_scale).astype(dt)
        p_ref[:, q_cols + 2 * r_cols:q_cols + 3 * r_cols] = proj[:, o_vr:o_vr + r_cols].astype(dt)
        p_ref[:, q_cols + 3 * r_cols:q_cols + 4 * r_cols] = proj[:, o_g:o_g + r_cols].astype(dt)

    emit(p_ref, kv_ref)


def _input_projection(x_p, x_s, norm_w, mod_p, mod_s, layer, w_in_bf16, rope_tab, geo):
    n_p, d = x_p.shape
    n_s = x_s.shape[0]
    in_cols = w_in_bf16.shape[1]
    kv_cols = 2 * N_KV_HEADS * HEAD_DIM
    p_cols = in_cols - kv_cols
    tpb = geo.tiles_per_batch
    tab_index = lambda i: (0, jnp.where(i >= geo.n_ptiles, tpb, i % tpb), 0)
    return pl.pallas_call(
        functools.partial(_inproj_body, geo),
        grid=((n_p + n_s) // TOKEN_TILE,),
        in_specs=[
            _prompt_rows_spec(geo, d), _sample_rows_spec(geo, d),
            pl.BlockSpec((1, d), lambda i: (0, 0)),
            _prompt_vec_spec(layer, 0, d), _prompt_vec_spec(layer, 1, d),
            _sample_vec_spec(layer, 0, geo, d), _sample_vec_spec(layer, 1, geo, d),
            pl.BlockSpec((d, in_cols), lambda i: (0, 0)),
            pl.BlockSpec((6, TOKEN_TILE, LANES), tab_index),
        ],
        out_specs=[
            pl.BlockSpec((TOKEN_TILE, p_cols), lambda i: (i, 0)),
            pl.BlockSpec((TOKEN_TILE, kv_cols), lambda i: (i, 0)),
        ],
        out_shape=[
            jax.ShapeDtypeStruct((n_p + n_s, p_cols), BF16),
            jax.ShapeDtypeStruct((n_p + n_s, kv_cols), F32),
        ],
        compiler_params=_cparams(("arbitrary",)),
        name="input_projection",
    )(x_p, x_s, norm_w.reshape(1, d), mod_p, mod_p, mod_s, mod_s, w_in_bf16, rope_tab)


def _rope_tables(t_prompt, t_sample):
    pos = jnp.concatenate([jnp.arange(t_prompt, dtype=jnp.int32),
                           PAST_LEN + (jnp.arange(TOKEN_TILE, dtype=jnp.int32) % t_sample)])
    d = np.arange(LANES) % HEAD_DIM

    def tables(n_dims, theta):
        half = n_dims // 2
        freqs = jnp.power(jnp.float32(theta), -jnp.arange(half, dtype=jnp.float32) / half)
        ang = pos.astype(jnp.float32)[:, None] * freqs[None, :]
        cos, sin = jnp.cos(ang), jnp.sin(ang)
        fidx = np.where(d < n_dims, d % half, 0)
        cos_l = jnp.where(jnp.asarray(d < n_dims)[None, :], cos[:, fidx], 1.0)
        sin_l = sin[:, fidx]
        upper = jnp.asarray((d >= half) & (d < n_dims))[None, :]
        lower = jnp.asarray(d < half)[None, :]
        return [cos_l, jnp.where(upper, sin_l, 0.0), jnp.where(lower, -sin_l, 0.0)]

    return jnp.stack(tables(ROPE_DIMS, ROPE_THETA) + tables(HEAD_DIM, RET_THETA)).astype(F32)


def _retention_tables(c):
    h = N_RET_HEADS
    log_gamma = jnp.log(1.0 - jnp.power(2.0, -5.0 - jnp.arange(h, dtype=jnp.float32)))
    idx = jnp.arange(c, dtype=jnp.float32)
    diff = idx[:, None] - idx[None, :]
    decay_mask = jnp.where(diff >= 0, jnp.exp(log_gamma[:, None, None] * jnp.maximum(diff, 0.0)), 0.0)
    k_dec = jnp.exp(log_gamma[None, :] * (c - 1 - idx)[:, None])
    q_dec = jnp.exp(log_gamma[None, :] * (idx + 1.0)[:, None])
    chunk_decay = jnp.exp(log_gamma * c)
    rep = lambda a: jnp.repeat(a, HEAD_DIM, axis=-1)
    return decay_mask.astype(F32), rep(q_dec).astype(F32), rep(k_dec).astype(F32), rep(chunk_decay[None, :]).astype(F32)


def _group_norm_pairs(items, avg):
    avg_b = avg.astype(BF16)

    def block_means(xs):
        his = [x.astype(BF16) for x in xs]
        los = [(x - hi.astype(F32)).astype(BF16) for x, hi in zip(xs, his)]
        return [_dot(hi, avg_b) + _dot(lo, avg_b) for hi, lo in zip(his, los)]

    dlts = [o - mu for o, mu in zip(items, block_means(items))]
    varis = block_means([d * d for d in dlts])
    return [d * lax.rsqrt(v + NORM_EPS) for d, v in zip(dlts, varis)]


def _group_norm_pair(o, avg):
    return _group_norm_pairs([o], avg)[0]


def _pair_average_matrix():
    r = np.arange(LANES)
    return jnp.asarray(((r[:, None] // HEAD_DIM) == (r[None, :] // HEAD_DIM)).astype(np.float32) / HEAD_DIM)


def _block_diag_mask():
    r = np.arange(LANES)
    return jnp.asarray(((r[:, None] // HEAD_DIM) == (r[None, :] // HEAD_DIM)).astype(np.float32))


def _prompt_mixer_body(sink_ref, p_ref, kvc_ref, kvp_ref, dmask_ref, qdec_ref, kdec_ref, cdec_ref,
                       bd_ref, avg_ref, gnw_ref, cat_ref, r_ref):
    jb = pl.program_id(1)
    blk = WINDOW
    q_cols = N_ATTN_HEADS * HEAD_DIM
    r_cols = N_RET_HEADS * HEAD_DIM
    kvw = N_KV_HEADS * HEAD_DIM

    @pl.when(jb == 0)
    def _():
        r_ref[...] = jnp.zeros_like(r_ref)

    low = _lane_is_low_half((blk, LANES))
    kband = jnp.concatenate([kvp_ref[:, 0:kvw], kvc_ref[:, 0:kvw]], axis=0)
    vband = jnp.concatenate([kvp_ref[:, kvw:2 * kvw], kvc_ref[:, kvw:2 * kvw]], axis=0)
    low2 = _lane_is_low_half((2 * blk, LANES))
    kswap = pltpu.roll(kband, HEAD_DIM, 1)
    vswap = pltpu.roll(vband, HEAD_DIM, 1)
    qi = lax.broadcasted_iota(jnp.int32, (blk, 2 * blk), 0)
    kj = lax.broadcasted_iota(jnp.int32, (blk, 2 * blk), 1)
    dist = blk + qi - kj
    mask = (dist >= 0) & (dist < WINDOW) & ((kj >= blk) | (jb > 0))
    for kvh in range(N_KV_HEADS):
        keep = low2 if kvh == 0 else jnp.logical_not(low2)
        k2 = jnp.where(keep, kband, kswap).astype(BF16)
        v2 = jnp.where(keep, vband, vswap).astype(BF16)
        pieces = []
        for pr in range(GQA_GROUP // 2):
            c0 = (kvh * (GQA_GROUP // 2) + pr) * LANES
            qp = p_ref[:, c0:c0 + LANES]
            pieces.append(jnp.where(low, qp, jnp.zeros_like(qp)))
            pieces.append(jnp.where(low, jnp.zeros_like(qp), qp))
        qs = jnp.concatenate(pieces, axis=0)
        s = _dot_nt(qs, k2)
        es, inv = [], []
        for hh in range(GQA_GROUP):
            sink = sink_ref[kvh * GQA_GROUP + hh]
            sh = jnp.where(mask, s[hh * blk:(hh + 1) * blk], NEG_INF)
            m = jnp.maximum(jnp.max(sh, axis=-1, keepdims=True), sink)
            e = jnp.exp(sh - m)
            den = jnp.sum(e, axis=-1, keepdims=True) + jnp.exp(sink - m)
            es.append(e.astype(BF16))
            inv.append(1.0 / den)
        o = _dot(jnp.concatenate(es, axis=0), v2)
        for pr in range(GQA_GROUP // 2):
            o_lo = o[(2 * pr) * blk:(2 * pr + 1) * blk] * inv[2 * pr]
            o_hi = o[(2 * pr + 1) * blk:(2 * pr + 2) * blk] * inv[2 * pr + 1]
            c0 = (kvh * (GQA_GROUP // 2) + pr) * LANES
            cat_ref[:, c0:c0 + LANES] = jnp.where(low, o_lo, o_hi).astype(BF16)
    n_pairs = N_RET_HEADS // 2
    pairs = range(n_pairs)
    col = lambda pp, part: slice(q_cols + part * r_cols + pp * LANES, q_cols + part * r_cols + (pp + 1) * LANES)
    pcol = lambda pp: slice(pp * LANES, (pp + 1) * LANES)
    qs = [p_ref[:, col(pp, 0)] for pp in pairs]
    ks = [p_ref[:, col(pp, 1)] for pp in pairs]
    vs = [p_ref[:, col(pp, 2)] for pp in pairs]
    r_prev = [r_ref[0, pp] for pp in pairs]
    zero = jnp.zeros_like(qs[0])
    sc_lo = [_dot_nt(jnp.where(low, qs[pp], zero), ks[pp]) for pp in pairs]
    sc_hi = [_dot_nt(jnp.where(low, zero, qs[pp]), ks[pp]) for pp in pairs]
    qd = [(qs[pp].astype(F32) * qdec_ref[:, pcol(pp)]).astype(BF16) for pp in pairs]
    kd = [(ks[pp].astype(F32) * kdec_ref[:, pcol(pp)]).astype(BF16) for pp in pairs]
    o_cross = [_dot(qd[pp], r_prev[pp].astype(BF16)) for pp in pairs]
    kv_new = [_dot_tn(kd[pp], vs[pp]) for pp in pairs]
    o_lo = [_dot((sc_lo[pp] * dmask_ref[2 * pp]).astype(BF16), vs[pp]) for pp in pairs]
    o_hi = [_dot((sc_hi[pp] * dmask_ref[2 * pp + 1]).astype(BF16), vs[pp]) for pp in pairs]
    for pp in pairs:
        r_ref[0, pp] = cdec_ref[pp] * r_prev[pp] + kv_new[pp] * bd_ref[...]
    o_all = jnp.concatenate([jnp.where(low, o_lo[pp], o_hi[pp]) + o_cross[pp] for pp in pairs], axis=0)
    y_all = _group_norm_pair(o_all, avg_ref[...])
    for pp in pairs:
        gp = p_ref[:, col(pp, 3)].astype(F32)
        y = y_all[pp * blk:(pp + 1) * blk] * gnw_ref[:, pcol(pp)] * _silu(gp)
        cat_ref[:, q_cols + pp * LANES:q_cols + (pp + 1) * LANES] = y.astype(BF16)


def _prompt_mixer(p_p, kv_p, sinks, gn_w, n_pbatch, t_prompt, ret_tabs):
    blk = WINDOW
    nb = t_prompt // blk
    p_cols = p_p.shape[1]
    kv_cols = kv_p.shape[1]
    d_mix = (N_ATTN_HEADS + N_RET_HEADS) * HEAD_DIM
    decay_mask, q_dec, k_dec, chunk_decay = ret_tabs
    n_pairs = N_RET_HEADS // 2
    bd = _block_diag_mask()
    cdec = jnp.stack([bd * chunk_decay[0, pp * LANES:(pp + 1) * LANES][:, None] for pp in range(n_pairs)])
    const2 = lambda shape: pl.BlockSpec(shape, lambda b, j: (0,) * len(shape))
    return pl.pallas_call(
        _prompt_mixer_body,
        grid=(n_pbatch, nb),
        in_specs=[
            pl.BlockSpec(memory_space=pltpu.SMEM),
            pl.BlockSpec((blk, p_cols), lambda b, j: (b * nb + j, 0)),
            pl.BlockSpec((blk, kv_cols), lambda b, j: (b * nb + j, 0)),
            pl.BlockSpec((blk, kv_cols), lambda b, j: (b * nb + jnp.maximum(j - 1, 0), 0)),
            const2((N_RET_HEADS, blk, blk)),
            const2((blk, N_RET_HEADS * HEAD_DIM)),
            const2((blk, N_RET_HEADS * HEAD_DIM)),
            const2((n_pairs, LANES, LANES)),
            const2((LANES, LANES)),
            const2((LANES, LANES)),
            const2((1, N_RET_HEADS * HEAD_DIM)),
        ],
        out_specs=[
            pl.BlockSpec((blk, d_mix), lambda b, j: (b * nb + j, 0)),
            pl.BlockSpec((1, n_pairs, LANES, LANES), lambda b, j: (b, 0, 0, 0)),
        ],
        out_shape=[
            jax.ShapeDtypeStruct((n_pbatch * t_prompt, d_mix), BF16),
            jax.ShapeDtypeStruct((n_pbatch, n_pairs, LANES, LANES), F32),
        ],
        compiler_params=_cparams(("arbitrary", "arbitrary")),
        name="prompt_mixer",
    )(sinks, p_p, kv_p, kv_p, decay_mask, q_dec, k_dec, cdec, bd, _pair_average_matrix(), gn_w.reshape(1, -1))


def _sample_mixer_body(t_new, sink_ref, p_ref, kv_ref, kbuf_ref, vbuf_ref, ret_ref, d4_ref, qdec_ref, kdec_ref,
                       cdec_ref, avg_ref, gnw_ref, cat_ref, knew_ref, vnew_ref, rnew_ref, ka2_ref, va2_ref, qs2_ref):
    w = kbuf_ref.shape[1]
    q_cols = N_ATTN_HEADS * HEAD_DIM
    r_cols = N_RET_HEADS * HEAD_DIM
    kvw = N_KV_HEADS * HEAD_DIM
    n_keys = ka2_ref.shape[1]
    rows_per_head = SUBLANES
    ka2_ref[...] = jnp.zeros_like(ka2_ref)
    va2_ref[...] = jnp.zeros_like(va2_ref)
    qs2_ref[...] = jnp.zeros_like(qs2_ref)
    low4 = _lane_is_low_half((t_new, LANES))
    lowk = _lane_is_low_half((n_keys, LANES))
    n_rows = GQA_GROUP * rows_per_head
    row = lax.broadcasted_iota(jnp.int32, (n_rows, n_keys), 0)
    key = lax.broadcasted_iota(jnp.int32, (n_rows, n_keys), 1)
    t_of_row = row % rows_per_head
    mask = (t_of_row < t_new) & (key > t_of_row) & (key <= t_of_row + WINDOW) & (key < w + t_new)
    head_of_row = lax.broadcasted_iota(jnp.int32, (n_rows, 1), 0) // rows_per_head

    def per_group(g, carry):
        units = range(SAMPLE_GROUP)
        bs = [SAMPLE_GROUP * g + u for u in units]
        items_a = [(u, kvh) for u in units for kvh in range(N_KV_HEADS)]
        items_r = [(u, pp) for u in units for pp in range(N_RET_HEADS // 2)]
        for u in units:
            b, ka_ref, va_ref, qs_ref = bs[u], ka2_ref.at[u], va2_ref.at[u], qs2_ref.at[u]
            ka_ref[0:w, :] = kbuf_ref[b]
            va_ref[0:w, :] = vbuf_ref[b]
            ka_ref[w:w + t_new, :] = kv_ref[b][:, 0:kvw]
            va_ref[w:w + t_new, :] = kv_ref[b][:, kvw:2 * kvw]
            knew_ref[b] = ka_ref[t_new:t_new + w, :]
            vnew_ref[b] = va_ref[t_new:t_new + w, :]
            for h in range(N_ATTN_HEADS):
                c0 = (h // 2) * LANES
                qp = p_ref[b][:, c0:c0 + LANES]
                qs_ref[h * rows_per_head:h * rows_per_head + t_new, :] = jnp.where(
                    low4 if h % 2 == 0 else jnp.logical_not(low4), qp, 0.0)
        kall = [ka2_ref[u] for u in units]
        vall = [va2_ref[u] for u in units]
        kswap = [pltpu.roll(kall[u], HEAD_DIM, 1) for u in units]
        vswap = [pltpu.roll(vall[u], HEAD_DIM, 1) for u in units]
        keep = [lowk, jnp.logical_not(lowk)]
        k2 = {(u, kvh): jnp.where(keep[kvh], kall[u], kswap[u]) for u, kvh in items_a}
        v2 = {(u, kvh): jnp.where(keep[kvh], vall[u], vswap[u]) for u, kvh in items_a}
        sinks = []
        for kvh in range(N_KV_HEADS):
            sink = jnp.zeros((n_rows, 1), F32)
            for hh in range(GQA_GROUP):
                sink = jnp.where(head_of_row == hh, sink_ref[kvh * GQA_GROUP + hh], sink)
            sinks.append(sink)
        s = {it: jnp.where(mask, _dot_nt(qs2_ref[it[0], it[1] * n_rows:(it[1] + 1) * n_rows, :], k2[it], HIGHEST),
                           NEG_INF) for it in items_a}
        m = {it: jnp.maximum(jnp.max(s[it], axis=-1, keepdims=True), sinks[it[1]]) for it in items_a}
        e = {it: jnp.exp(s[it] - m[it]) for it in items_a}
        den = {it: jnp.sum(e[it], axis=-1, keepdims=True) + jnp.exp(sinks[it[1]] - m[it]) for it in items_a}
        o = {it: _dot(e[it], v2[it], HIGHEST) / den[it] for it in items_a}
        for u, kvh in items_a:
            for pr in range(GQA_GROUP // 2):
                o_lo = o[u, kvh][(2 * pr) * rows_per_head:(2 * pr) * rows_per_head + t_new]
                o_hi = o[u, kvh][(2 * pr + 1) * rows_per_head:(2 * pr + 1) * rows_per_head + t_new]
                c0 = (kvh * (GQA_GROUP // 2) + pr) * LANES
                cat_ref[bs[u], :, c0:c0 + LANES] = jnp.where(low4, o_lo, o_hi)
        col = lambda pp, part: slice(q_cols + part * r_cols + pp * LANES, q_cols + part * r_cols + (pp + 1) * LANES)
        pcol = lambda pp: slice(pp * LANES, (pp + 1) * LANES)
        qp = {it: p_ref[bs[it[0]]][:, col(it[1], 0)] for it in items_r}
        kp = {it: p_ref[bs[it[0]]][:, col(it[1], 1)] for it in items_r}
        vp = {it: p_ref[bs[it[0]]][:, col(it[1], 2)] for it in items_r}
        r0 = {it: ret_ref[bs[it[0]], it[1]] for it in items_r}
        q_lo = {it: jnp.where(low4, qp[it], 0.0) for it in items_r}
        q_hi = {it: jnp.where(low4, 0.0, qp[it]) for it in items_r}
        sc_lo = {it: _dot_nt(q_lo[it], kp[it], HIGHEST) * d4_ref[2 * it[1]] for it in items_r}
        sc_hi = {it: _dot_nt(q_hi[it], kp[it], HIGHEST) * d4_ref[2 * it[1] + 1] for it in items_r}
        oc_lo = {it: _dot(q_lo[it] * qdec_ref[:, pcol(it[1])], r0[it], HIGHEST) for it in items_r}
        oc_hi = {it: _dot(q_hi[it] * qdec_ref[:, pcol(it[1])], r0[it], HIGHEST) for it in items_r}
        kv_full = {it: _dot_tn(kp[it] * kdec_ref[:, pcol(it[1])], vp[it], HIGHEST) for it in items_r}
        oi_lo = {it: _dot(sc_lo[it], vp[it], HIGHEST) for it in items_r}
        oi_hi = {it: _dot(sc_hi[it], vp[it], HIGHEST) for it in items_r}
        for it in items_r:
            kv_pair = jnp.concatenate([kv_full[it][0:HEAD_DIM, 0:HEAD_DIM], kv_full[it][HEAD_DIM:, HEAD_DIM:]], axis=0)
            rnew_ref[bs[it[0]], it[1]] = cdec_ref[it[1]] * r0[it] + kv_pair
        o_ret = [jnp.where(low4, oi_lo[it], oi_hi[it]) + jnp.concatenate([oc_lo[it], oc_hi[it]], axis=1)
                 for it in items_r]
        y_ret = _group_norm_pairs(o_ret, avg_ref[...])
        for it, y in zip(items_r, y_ret):
            gp = p_ref[bs[it[0]]][:, col(it[1], 3)]
            cat_ref[bs[it[0]], :, q_cols + it[1] * LANES:q_cols + (it[1] + 1) * LANES] = (
                y * gnw_ref[:, pcol(it[1])] * _silu(gp))
        return carry

    lax.fori_loop(0, kbuf_ref.shape[0] // SAMPLE_GROUP, per_group, 0)


def _sample_mixer(p_s, kv_s, k_buf, v_buf, ret0, sinks, gn_w, ret_tabs):
    nb, t_new, p_cols = p_s.shape
    kv_cols = kv_s.shape[2]
    w = k_buf.shape[1]
    bb = SAMPLE_BATCH_TILE
    n_pairs = N_RET_HEADS // 2
    d_mix = (N_ATTN_HEADS + N_RET_HEADS) * HEAD_DIM
    decay_mask, q_dec, k_dec, chunk_decay = ret_tabs
    n_keys = ((w + t_new + SUBLANES - 1) // SUBLANES) * SUBLANES
    cdec = jnp.stack([jnp.broadcast_to(chunk_decay[0, pp * LANES:(pp + 1) * LANES][:, None], (LANES, HEAD_DIM))
                      for pp in range(n_pairs)])
    const1 = lambda shape: pl.BlockSpec(shape, lambda i: (0,) * len(shape))
    body = functools.partial(_sample_mixer_body, t_new)
    return pl.pallas_call(
        body,
        grid=(nb // bb,),
        in_specs=[
            pl.BlockSpec(memory_space=pltpu.SMEM),
            pl.BlockSpec((bb, t_new, p_cols), lambda i: (i, 0, 0)),
            pl.BlockSpec((bb, t_new, kv_cols), lambda i: (i, 0, 0)),
            pl.BlockSpec((bb, w, LANES), lambda i: (i, 0, 0)),
            pl.BlockSpec((bb, w, LANES), lambda i: (i, 0, 0)),
            pl.BlockSpec((bb, n_pairs, LANES, HEAD_DIM), lambda i: (i, 0, 0, 0)),
            const1((N_RET_HEADS, t_new, t_new)),
            const1((t_new, N_RET_HEADS * HEAD_DIM)),
            const1((t_new, N_RET_HEADS * HEAD_DIM)),
            const1((n_pairs, LANES, HEAD_DIM)),
            const1((LANES, LANES)),
            const1((1, N_RET_HEADS * HEAD_DIM)),
        ],
        out_specs=[
            pl.BlockSpec((bb, t_new, d_mix), lambda i: (i, 0, 0)),
            pl.BlockSpec((bb, w, LANES), lambda i: (i, 0, 0)),
            pl.BlockSpec((bb, w, LANES), lambda i: (i, 0, 0)),
            pl.BlockSpec((bb, n_pairs, LANES, HEAD_DIM), lambda i: (i, 0, 0, 0)),
        ],
        out_shape=[
            jax.ShapeDtypeStruct((nb, t_new, d_mix), F32),
            jax.ShapeDtypeStruct((nb, w, LANES), F32),
            jax.ShapeDtypeStruct((nb, w, LANES), F32),
            jax.ShapeDtypeStruct((nb, n_pairs, LANES, HEAD_DIM), F32),
        ],
        scratch_shapes=[
            pltpu.VMEM((SAMPLE_GROUP, n_keys, LANES), F32),
            pltpu.VMEM((SAMPLE_GROUP, n_keys, LANES), F32),
            pltpu.VMEM((SAMPLE_GROUP, N_ATTN_HEADS * SUBLANES, LANES), F32),
        ],
        compiler_params=_cparams(("arbitrary",)),
        name="sample_mixer",
    )(sinks, p_s, kv_s, k_buf, v_buf, ret0, decay_mask, q_dec, k_dec, cdec, _pair_average_matrix(), gn_w.reshape(1, -1))


def _router_body(geo, xp_ref, xs_ref, catp_ref, cats_ref, wout_ref, g1p_ref, g1s_ref, nw_ref, shp_ref, scp_ref,
                 shs_ref, scs_ref, wr_ref, br_ref, tri_ref, x1_ref, h2_ref, idx_ref, gate_ref, rank_ref, cnt_ref):
    i = pl.program_id(0)
    tv = functools.partial(_tile_vec, i, geo)
    cat = _tile_rows(i, geo, catp_ref, cats_ref)
    x1 = _tile_rows(i, geo, xp_ref, xs_ref) + tv(g1p_ref, g1s_ref) * _dot(cat, wout_ref[...])
    x1_ref[...] = x1
    h2 = _rms_norm(x1, nw_ref[...]) * (1.0 + tv(scp_ref, scs_ref)) + tv(shp_ref, shs_ref)
    h2_hi = h2.astype(BF16)
    h2_ref[...] = h2_hi.reshape(h2_ref.shape)
    h2_lo = (h2 - h2_hi.astype(F32)).astype(BF16)
    w_parts = wr_ref[...]
    by_hi = _dot_nt(w_parts, h2_hi)
    logits = by_hi[:N_EXPERTS] + by_hi[N_EXPERTS:] + _dot_nt(w_parts[:N_EXPERTS], h2_lo) + br_ref[...]
    e_iota = lax.broadcasted_iota(jnp.int32, logits.shape, 0)
    vals, sels = [], []
    for _ in range(TOP_K):
        m = jnp.max(logits, axis=0, keepdims=True)
        sel = jnp.min(jnp.where(logits == m, e_iota, N_EXPERTS), axis=0, keepdims=True)
        vals.append(m)
        sels.append(sel)
        logits = jnp.where(e_iota == sel, -jnp.inf, logits)
    es = [jnp.exp(v - vals[0]) for v in vals]
    den = es[0]
    for e in es[1:]:
        den = den + e

    @pl.when(i == 0)
    def _():
        cnt_ref[...] = jnp.zeros_like(cnt_ref)

    hits = [e_iota == sels[k] for k in range(TOP_K)]
    onehots = [jnp.where(h, 1.0, 0.0) for h in hits]
    befores = [_dot(o.astype(BF16), tri_ref[...]) for o in onehots]
    totals = [jnp.sum(o, axis=1, keepdims=True) for o in onehots]
    base = cnt_ref[...]
    for k in range(TOP_K):
        rank = jnp.sum(jnp.where(hits[k], base + befores[k], 0.0), axis=0, keepdims=True)
        base = base + totals[k]
        idx_ref[k:k + 1, :] = sels[k]
        gate_ref[k:k + 1, :] = es[k] / den
        rank_ref[k:k + 1, :] = rank.astype(jnp.int32)
    cnt_ref[...] = base


def _outproj_router(x_p, x_s, cat_p, cat_s, w_out_bf16, norm_w, mod_p, mod_s, layer, w_router_t, b_router, geo):
    n_p, d = x_p.shape
    n = n_p + x_s.shape[0]
    d_mix = cat_p.shape[1]
    pv = lambda chunk: _prompt_vec_spec(layer, chunk, d)
    sv = lambda chunk: _sample_vec_spec(layer, chunk, geo, d)
    tile = lambda cols: pl.BlockSpec((TOKEN_TILE, cols), lambda i: (i, 0))
    choice = pl.BlockSpec((TOP_K, TOKEN_TILE), lambda i: (0, i))
    tri = jnp.asarray(np.triu(np.ones((TOKEN_TILE, TOKEN_TILE), np.float32), 1)).astype(BF16)
    w_hi = w_router_t.astype(BF16)
    w_router_parts = jnp.concatenate([w_hi, (w_router_t - w_hi.astype(F32)).astype(BF16)], axis=0)
    return pl.pallas_call(
        functools.partial(_router_body, geo),
        grid=(n // TOKEN_TILE,),
        in_specs=[
            _prompt_rows_spec(geo, d), _sample_rows_spec(geo, d),
            _prompt_rows_spec(geo, d_mix), _sample_rows_spec(geo, d_mix),
            pl.BlockSpec((d_mix, d), lambda i: (0, 0)),
            pv(2), sv(2),
            pl.BlockSpec((1, d), lambda i: (0, 0)),
            pv(3), pv(4), sv(3), sv(4),
            pl.BlockSpec((2 * N_EXPERTS, d), lambda i: (0, 0)),
            pl.BlockSpec((N_EXPERTS, 1), lambda i: (0, 0)),
            pl.BlockSpec((TOKEN_TILE, TOKEN_TILE), lambda i: (0, 0)),
        ],
        out_specs=[
            tile(d),
            pl.BlockSpec((TOKEN_TILE, d // LANES, LANES), lambda i: (i, 0, 0)),
            choice, choice, choice,
            pl.BlockSpec((N_EXPERTS, 1), lambda i: (0, 0)),
        ],
        out_shape=[
            jax.ShapeDtypeStruct((n, d), F32),
            jax.ShapeDtypeStruct((n, d // LANES, LANES), BF16),
            jax.ShapeDtypeStruct((TOP_K, n), jnp.int32),
            jax.ShapeDtypeStruct((TOP_K, n), F32),
            jax.ShapeDtypeStruct((TOP_K, n), jnp.int32),
            jax.ShapeDtypeStruct((N_EXPERTS, 1), F32),
        ],
        compiler_params=_cparams(("arbitrary",)),
        name="outproj_router",
    )(x_p, x_s, cat_p, cat_s, w_out_bf16, mod_p, mod_s, norm_w.reshape(1, d), mod_p, mod_p, mod_s, mod_s,
      w_router_parts, b_router.reshape(N_EXPERTS, 1), tri)


def _routing_plan(top_idx, rank, counts_f, tm):
    k, n = top_idx.shape
    n_pairs = k * n
    pad_step = TOP_K * TOKEN_TILE
    n_tiles = (n_pairs + N_EXPERTS * (tm - 1)) // tm + 1
    while (n_tiles * tm - n_pairs) % pad_step:
        n_tiles += 1
    counts = counts_f[:, 0].astype(jnp.int32)
    tiles_per = (counts + tm - 1) // tm
    tile_end = jnp.cumsum(tiles_per)
    tile_start = tile_end - tiles_per
    n_used = tile_end[-1]
    n_slots = n_tiles * tm

    def lookup(table, index):
        ids = jnp.arange(table.shape[0], dtype=jnp.int32).reshape((-1,) + (1,) * index.ndim)
        return jnp.sum(jnp.where(index[None] == ids, table.reshape(ids.shape), 0), axis=0)

    slot_of_pair = lookup(tile_start * tm, top_idx) + rank
    pad_counts = jnp.concatenate([tiles_per * tm - counts, (n_slots - n_used * tm).reshape(1)])
    pad_end = jnp.cumsum(pad_counts)
    first_pad_slot = jnp.concatenate([tile_start * tm + counts, (n_used * tm).reshape(1)])
    k_pad = jnp.arange(n_slots - n_pairs, dtype=jnp.int32)
    seg = jnp.sum(k_pad[:, None] >= pad_end[None, :], axis=1).astype(jnp.int32)
    pad_slot = k_pad + lookup(first_pad_slot - (pad_end - pad_counts), seg)
    tile_ids = jnp.arange(n_tiles, dtype=jnp.int32)
    tile_expert = jnp.minimum(jnp.sum(tile_ids[:, None] >= tile_end[None, :], axis=1), N_EXPERTS - 1).astype(jnp.int32)
    last_expert = tile_expert[jnp.maximum(n_used - 1, 0)]
    tile_expert = jnp.where(tile_ids < n_used, tile_expert, last_expert)
    return tile_expert, n_used.reshape(1).astype(jnp.int32), slot_of_pair, pad_slot


def _issue_row_copies(src_hbm, row_of, dst_row, sem, n_rows):
    def group(g, c):
        for u in range(DMA_UNROLL):
            r = g * DMA_UNROLL + u
            pltpu.make_async_copy(src_hbm.at[row_of(r)], dst_row(r), sem).start(priority=u % 2)
        return c
    lax.fori_loop(0, n_rows // DMA_UNROLL, group, 0)


def _issue_row_copies_static(src_hbm, row_of, dst_row, sem, lo, hi):
    for r in range(lo, hi):
        pltpu.make_async_copy(src_hbm.at[row_of(r)], dst_row(r), sem).start(priority=r % 2)


def _wait_row_copies(src_hbm, dst_buf, sem):
    pltpu.make_async_copy(src_hbm.at[pl.ds(0, dst_buf.shape[0])], dst_buf, sem).wait()


def _dispatch_body(n_token_steps, slots_ref, h2_ref, xs_hbm, stage, sems):
    i = pl.program_id(0)
    n_steps = pl.num_programs(0)
    tm = TOKEN_TILE

    def wait_all(par):
        for k in range(TOP_K):
            pltpu.make_async_copy(stage.at[par], xs_hbm.at[pl.ds(0, tm)], sems.at[par]).wait()

    def step(par):
        buf = stage.at[par]

        @pl.when(i < n_token_steps)
        def _():
            buf[...] = h2_ref[...]

        @pl.when(i >= n_token_steps)
        def _():
            buf[...] = jnp.zeros_like(buf)

        for k in range(TOP_K):
            for r in range(tm):
                pltpu.make_async_copy(buf.at[r], xs_hbm.at[slots_ref[0, k, r]], sems.at[par]).start(priority=r % 2)

        @pl.when(i > 0)
        def _():
            wait_all(1 - par)

        @pl.when(i == n_steps - 1)
        def _():
            wait_all(par)

    @pl.when(i % 2 == 0)
    def _():
        step(0)

    @pl.when(i % 2 == 1)
    def _():
        step(1)


def _dispatch(h2_rows, slot_of_pair, pad_slot, n_slots):
    n = h2_rows.shape[0]
    row_tile = h2_rows.shape[1:]
    tm = TOKEN_TILE
    n_token_steps = n // tm
    slots = jnp.concatenate([slot_of_pair.reshape(TOP_K, n_token_steps, tm).transpose(1, 0, 2),
                             pad_slot.reshape(-1, TOP_K, tm)], axis=0)
    n_steps = slots.shape[0]
    return pl.pallas_call(
        functools.partial(_dispatch_body, n_token_steps),
        grid=(n_steps,),
        in_specs=[
            pl.BlockSpec((1, TOP_K, tm), lambda i: (i, 0, 0), memory_space=pltpu.SMEM),
            pl.BlockSpec((tm,) + row_tile, lambda i: (jnp.minimum(i, n_token_steps - 1), 0, 0)),
        ],
        out_specs=pl.BlockSpec(memory_space=pl.ANY),
        out_shape=jax.ShapeDtypeStruct((n_slots,) + row_tile, h2_rows.dtype),
        scratch_shapes=[pltpu.VMEM((2, tm) + row_tile, h2_rows.dtype), pltpu.SemaphoreType.DMA((2,))],
        compiler_params=_cparams(("arbitrary",)),
        name="moe_dispatch",
    )(slots, h2_rows)


def _expert_body(te_ref, nt_ref, x_ref, wgu_ref, bg_ref, bu_ref, wd_ref, bd_ref, perm_ref, y_ref, wgu_s, wd_s, act_s):
    j = pl.program_id(0)
    n_used = nt_ref[0]
    tm = EXPERT_TILE
    f2 = wgu_ref.shape[2]
    n_chunks = f2 // MXU_WIDTH
    half = MXU_WIDTH // 2

    new_expert = jnp.logical_or(j == 0, te_ref[j] != te_ref[jnp.maximum(j - 1, 0)])

    @pl.when(jnp.logical_and(j < n_used, new_expert))
    def _():
        for c in range(n_chunks):
            cols = slice(c * MXU_WIDTH, (c + 1) * MXU_WIDTH)
            wgu_s[:, cols] = _dot(wgu_ref[0, :, cols].astype(BF16), perm_ref[...]).astype(BF16)
        wd_s[...] = wd_ref[0].astype(BF16)

    @pl.when(j < n_used)
    def _():
        x = x_ref[...].reshape(tm, wgu_ref.shape[1])
        for c in range(n_chunks):
            gu = _dot(x, wgu_s[:, c * MXU_WIDTH:(c + 1) * MXU_WIDTH])
            glu = jnp.minimum(gu[:, :half] + bg_ref[0, :, c * half:(c + 1) * half], SWIGLU_LIMIT)
            lin = jnp.clip(gu[:, half:] + bu_ref[0, :, c * half:(c + 1) * half], -SWIGLU_LIMIT, SWIGLU_LIMIT)
            act_s[:, c * half:(c + 1) * half] = (glu * jax.nn.sigmoid(SWIGLU_ALPHA * glu) * (lin + 1.0)).astype(BF16)
        y = _dot(act_s[...], wd_s[...]) + bd_ref[0]
        y_ref[...] = y.astype(BF16).reshape(y_ref.shape)

    @pl.when(j >= n_used)
    def _():
        y_ref[...] = jnp.zeros_like(y_ref)


def _deinterleave_matrix():
    m = np.zeros((MXU_WIDTH, MXU_WIDTH), np.float32)
    j = np.arange(MXU_WIDTH // 2)
    m[2 * j, j] = 1.0
    m[2 * j + 1, MXU_WIDTH // 2 + j] = 1.0
    return jnp.asarray(m).astype(BF16)


def _routed_experts(x_rows, tile_expert, n_used, w_gate_up, bg, bu, w_down, bd, layer):
    d = w_gate_up.shape[2]
    f2 = w_gate_up.shape[3]
    f = f2 // 2
    tm = EXPERT_TILE
    n_tiles = x_rows.shape[0] // tm
    row_tile = x_rows.shape[1:]
    by_expert = lambda shape: pl.BlockSpec(shape, lambda j, te, nt: (te[j], 0, 0))
    layer_expert = lambda shape: pl.BlockSpec((None,) + shape, lambda j, te, nt: (layer, te[j], 0, 0))
    grid_spec = pltpu.PrefetchScalarGridSpec(
        num_scalar_prefetch=2,
        grid=(n_tiles,),
        in_specs=[
            pl.BlockSpec((tm,) + row_tile, lambda j, te, nt: (j, 0, 0)),
            layer_expert((1, d, f2)),
            by_expert((1, 1, f)), by_expert((1, 1, f)),
            layer_expert((1, f, d)),
            by_expert((1, 1, d)),
            pl.BlockSpec((MXU_WIDTH, MXU_WIDTH), lambda j, te, nt: (0, 0)),
        ],
        out_specs=pl.BlockSpec((tm,) + row_tile, lambda j, te, nt: (j, 0, 0)),
        scratch_shapes=[
            pltpu.VMEM((d, f2), BF16),
            pltpu.VMEM((f, d), BF16),
            pltpu.VMEM((tm, f), BF16),
        ],
    )
    return pl.pallas_call(
        _expert_body,
        grid_spec=grid_spec,
        out_shape=jax.ShapeDtypeStruct((n_tiles * tm,) + row_tile, BF16),
        compiler_params=_cparams(("arbitrary",)),
        name="routed_experts",
    )(tile_expert, n_used, x_rows, w_gate_up, bg, bu, w_down, bd, _deinterleave_matrix())


def _combine_body(geo, final_norm, slots_cur_ref, slots_nxt_ref, x1_ref, gate_ref, g2p_ref, g2s_ref, nfw_ref, y_hbm,
                  outp_ref, outs_ref, ybuf_even, ybuf_odd, sems):
    i = pl.program_id(0)
    n_steps = pl.num_programs(0)
    tm = TOKEN_TILE

    @pl.when(i == 0)
    def _():
        for k in range(TOP_K):
            _issue_row_copies(y_hbm, lambda r, k=k: slots_cur_ref[0, k, r], lambda r, k=k: ybuf_even.at[k, r],
                              sems.at[0], tm)

    def step(cur, nxt, sem_cur, sem_nxt):
        for k in range(TOP_K):
            _wait_row_copies(y_hbm, cur.at[k], sem_cur)
        for k in range(TOP_K):
            _issue_row_copies_static(y_hbm, lambda r, k=k: slots_nxt_ref[0, k, r], lambda r, k=k: nxt.at[k, r],
                                     sem_nxt, 0, tm)
        rows = lambda k: cur[k].reshape(tm, x1_ref.shape[1]).astype(F32)
        acc = rows(0) * gate_ref[:, 0:1]
        for k in range(1, TOP_K):
            acc = acc + rows(k) * gate_ref[:, k:k + 1]
        x2 = x1_ref[...] + _tile_vec(i, geo, g2p_ref, g2s_ref) * acc
        if final_norm:
            x2 = _rms_norm(x2, nfw_ref[...])

        @pl.when(i < geo.n_ptiles)
        def _():
            outp_ref[...] = x2

        @pl.when(i >= geo.n_ptiles)
        def _():
            outs_ref[...] = x2

        @pl.when(i == n_steps - 1)
        def _():
            for k in range(TOP_K):
                _wait_row_copies(y_hbm, nxt.at[k], sem_nxt)

    @pl.when(i % 2 == 0)
    def _():
        step(ybuf_even, ybuf_odd, sems.at[0], sems.at[1])

    @pl.when(i % 2 == 1)
    def _():
        step(ybuf_odd, ybuf_even, sems.at[1], sems.at[0])


def _combine(x1, y_rows, slot_of_pair, gates_t, mod_p, mod_s, layer, final_w, geo, final_norm):
    n, d = x1.shape
    tm = TOKEN_TILE
    n_steps = n // tm
    n_p = geo.n_ptiles * tm
    slots = slot_of_pair.reshape(TOP_K, n_steps, tm).transpose(1, 0, 2)
    return pl.pallas_call(
        functools.partial(_combine_body, geo, final_norm),
        grid=(n_steps,),
        in_specs=[
            pl.BlockSpec((1, TOP_K, tm), lambda i: (i, 0, 0), memory_space=pltpu.SMEM),
            pl.BlockSpec((1, TOP_K, tm), lambda i: (jnp.minimum(i + 1, n_steps - 1), 0, 0), memory_space=pltpu.SMEM),
            pl.BlockSpec((tm, d), lambda i: (i, 0)),
            pl.BlockSpec((tm, TOP_K), lambda i: (i, 0)),
            _prompt_vec_spec(layer, 5, d),
            _sample_vec_spec(layer, 5, geo, d),
            pl.BlockSpec((1, d), lambda i: (0, 0)),
            pl.BlockSpec(memory_space=pl.ANY),
        ],
        out_specs=[_prompt_rows_spec(geo, d), _sample_rows_spec(geo, d)],
        out_shape=[jax.ShapeDtypeStruct((n_p, d), F32), jax.ShapeDtypeStruct((n - n_p, d), F32)],
        scratch_shapes=[pltpu.VMEM((TOP_K, tm) + y_rows.shape[1:], y_rows.dtype),
                        pltpu.VMEM((TOP_K, tm) + y_rows.shape[1:], y_rows.dtype),
                        pltpu.SemaphoreType.DMA((2,))],
        compiler_params=_cparams(("arbitrary",)),
        name="moe_combine",
    )(slots, slots, x1, gates_t, mod_p, mod_s, final_w.reshape(1, d), y_rows)


def kernel(x_prompt, x_sample, state_swa_k, state_swa_v, state_ret, c_prompt, c_sample, norm_mix_w, w_ada, b_ada, w_in, attn_sinks, ret_gn_w, w_out, norm_ffn_w, w_router, b_router, w_gate_up, b_gate_up, w_down, b_down, norm_final_w):
    n_pbatch, t_prompt, d = x_prompt.shape
    n_sbatch, t_sample, _ = x_sample.shape
    depth = w_in.shape[0]
    n_p = n_pbatch * t_prompt
    n_s = n_sbatch * t_sample
    win = state_swa_k.shape[2]
    assert d == (N_ATTN_HEADS + N_RET_HEADS) * HEAD_DIM and w_gate_up.shape[1] == N_EXPERTS
    assert t_prompt % TOKEN_TILE == 0 and n_s % TOKEN_TILE == 0 and TOKEN_TILE % t_sample == 0
    assert t_prompt % WINDOW == 0 and RET_CHUNK == WINDOW and win == WINDOW and t_sample <= SUBLANES
    assert n_sbatch % SAMPLE_BATCH_TILE == 0 and n_pbatch <= SUBLANES
    assert w_gate_up.shape[3] % MXU_WIDTH == 0 and EXPERT_TILE % DMA_UNROLL == 0 and TOKEN_TILE % DMA_UNROLL == 0
    geo = _Geometry(n_p // TOKEN_TILE, t_prompt // TOKEN_TILE, n_pbatch)

    c_all = jnp.concatenate([c_prompt, jnp.zeros((SUBLANES - n_pbatch, d), F32), c_sample], axis=0)
    mod = _ada_modulation(c_all, w_ada, b_ada)
    mod_p = mod[:, :SUBLANES]
    mod_s = jnp.repeat(mod[:, SUBLANES:], t_sample, axis=1)

    rope_tab = _rope_tables(t_prompt, t_sample)
    ret_tabs_p = _retention_tables(RET_CHUNK)
    ret_tabs_s = _retention_tables(t_sample)
    f = w_gate_up.shape[3] // 2

    x_p = x_prompt.reshape(n_p, d)
    x_s = x_sample.reshape(n_s, d)
    kp_l, vp_l, rp_l, ks_l, vs_l, rs_l = [], [], [], [], [], []
    for l in range(depth):
        p_all, kv_all = _input_projection(x_p, x_s, norm_mix_w[l], mod_p, mod_s, l, w_in[l].astype(BF16), rope_tab, geo)
        cat_p, r_pairs = _prompt_mixer(p_all, kv_all, attn_sinks[l], ret_gn_w[l], n_pbatch, t_prompt, ret_tabs_p)
        p_s = p_all[n_p:].astype(F32)
        kv_s = kv_all[n_p:]
        kv_p = kv_all[:n_p]
        cat_s, k_new, v_new, r_new = _sample_mixer(
            p_s.reshape(n_sbatch, t_sample, -1), kv_s.reshape(n_sbatch, t_sample, -1),
            state_swa_k[l].reshape(n_sbatch, win, LANES), state_swa_v[l].reshape(n_sbatch, win, LANES),
            state_ret[l].reshape(n_sbatch, N_RET_HEADS // 2, LANES, HEAD_DIM), attn_sinks[l], ret_gn_w[l], ret_tabs_s)
        x1, h2_rows, top_idx, gates, rank, counts = _outproj_router(
            x_p, x_s, cat_p, cat_s.reshape(n_s, d), w_out[l].astype(BF16), norm_ffn_w[l], mod_p, mod_s, l,
            w_router[l].T, b_router[l], geo)
        tile_expert, n_used, slot_of_pair, pad_slot = _routing_plan(top_idx, rank, counts, EXPERT_TILE)
        x_rows = _dispatch(h2_rows, slot_of_pair, pad_slot, tile_expert.shape[0] * EXPERT_TILE)
        b_gu = b_gate_up[l].reshape(N_EXPERTS, 1, f, 2)
        y_rows = _routed_experts(x_rows, tile_expert, n_used, w_gate_up,
                                 b_gu[..., 0], b_gu[..., 1], w_down, b_down[l][:, None, :], l)
        x_p, x_s = _combine(x1, y_rows, slot_of_pair, gates.T, mod_p, mod_s, l, norm_final_w, geo,
                            final_norm=(l == depth - 1))

        kv_last = kv_p.reshape(n_pbatch, t_prompt, -1)[:, t_prompt - WINDOW:]
        kv_last = kv_last.reshape(n_pbatch, WINDOW, 2, N_KV_HEADS, HEAD_DIM)
        kp_l.append(kv_last[:, :, 0])
        vp_l.append(kv_last[:, :, 1])
        rp_l.append(jnp.stack([r_pairs[:, :, :HEAD_DIM, :HEAD_DIM], r_pairs[:, :, HEAD_DIM:, HEAD_DIM:]], axis=2)
                    .reshape(n_pbatch, N_RET_HEADS, HEAD_DIM, HEAD_DIM))
        ks_l.append(k_new.reshape(n_sbatch, win, N_KV_HEADS, HEAD_DIM))
        vs_l.append(v_new.reshape(n_sbatch, win, N_KV_HEADS, HEAD_DIM))
        rs_l.append(r_new.reshape(n_sbatch, N_RET_HEADS, HEAD_DIM, HEAD_DIM))
    return (x_p.reshape(n_pbatch, t_prompt, d), x_s.reshape(n_sbatch, t_sample, d), jnp.stack(kp_l), jnp.stack(vp_l),
            jnp.stack(rp_l), jnp.stack(ks_l), jnp.stack(vs_l), jnp.stack(rs_l))
```
